```python
import math
import jax, jax.numpy as jnp
from jax import lax
import numpy as np

D_MODEL = 1024
BATCH = 8
SEQ = 4096
DEPTH = 2

N_A_LAYERS = DEPTH // 2
N_B_LAYERS = DEPTH - N_A_LAYERS
N_EVEN = (DEPTH + 1) // 2
N_ODD = DEPTH // 2
CONV_WIDTH = 31
N_HEADS = 16
HEAD_DIM = D_MODEL // N_HEADS
DILATED_GROUPS = ((128, 1), (512, 4), (2048, 16))
N_GROUPS = len(DILATED_GROUPS)
Q_WIDTH = N_GROUPS * N_HEADS * HEAD_DIM
BLK = 128
N_BUCKETS = 32
MAX_DISTANCE = 2048
D_FF = 2752
N_EXPERTS = 8
TOP_K = 2
D_FF_EXPERT = 3584
EPS = 1e-6
NEG = -1e30

kernel_name = "yoco_conformer_dilated_moe_trunk"


def rmsnorm(x, g):
    xf = x.astype(jnp.float32)
    y = xf * lax.rsqrt(jnp.mean(xf * xf, axis=-1, keepdims=True) + EPS)
    return (y * g.astype(jnp.float32)).astype(x.dtype)


def layernorm(x, g, b):
    xf = x.astype(jnp.float32)
    mu = jnp.mean(xf, axis=-1, keepdims=True)
    var = jnp.mean(jnp.square(xf - mu), axis=-1, keepdims=True)
    y = (xf - mu) * lax.rsqrt(var + EPS)
    return (y * g.astype(jnp.float32) + b.astype(jnp.float32)).astype(x.dtype)


def swiglu(h, w_gate, w_up, w_down):
    return (jax.nn.silu(h @ w_gate) * (h @ w_up)) @ w_down


def t5_bucket(dist):
    n = np.asarray(dist)
    max_exact = N_BUCKETS // 2
    large = max_exact + (np.log(np.maximum(n, 1) / max_exact)
                         / np.log(MAX_DISTANCE / max_exact)
                         * (N_BUCKETS - max_exact)).astype(np.int32)
    large = np.minimum(large, N_BUCKETS - 1)
    return np.where(n < max_exact, n, large).astype(np.int32)


def make_band_bias(rel_bias, g):
    window, d = DILATED_GROUPS[g]
    span = window // d
    sub = BLK + np.arange(BLK)[:, None] - np.arange(2 * BLK)[None, :]
    in_band = (sub >= 0) & (sub <= span)
    bucket = t5_bucket(np.clip(sub, 0, span) * d)
    table_g = rel_bias[:, g * N_HEADS:(g + 1) * N_HEADS]
    bias = jnp.transpose(table_g[bucket], (2, 0, 1)).astype(jnp.float32)
    return jnp.where(jnp.asarray(in_band), bias, NEG)


def to_residue_layout(t, d):
    b, s = t.shape[0], t.shape[1]
    p = -(-s // (d * BLK)) * d * BLK
    t = jnp.pad(t, ((0, 0), (0, p - s)) + ((0, 0),) * (t.ndim - 2))
    t = t.reshape((b, p // d, d) + t.shape[2:])
    return jnp.swapaxes(t, 1, 2)


def from_residue_layout(t, s):
    b, d, l = t.shape[0], t.shape[1], t.shape[2]
    t = jnp.swapaxes(t, 1, 2)
    return t.reshape((b, l * d) + t.shape[3:])[:, :s]


def conv_module(x, norm_g, w_in, b_in, dw_w, dw_b, ln_g, ln_b, w_out, b_out):
    h = rmsnorm(x, norm_g)
    u = h @ w_in + b_in
    a, gate = jnp.split(u, 2, axis=-1)
    u = a * jax.nn.sigmoid(gate)
    u = lax.conv_general_dilated(u, dw_w[:, None, :], window_strides=(1,),
                                 padding=[(CONV_WIDTH - 1, 0)],
                                 dimension_numbers=('NWC', 'WIO', 'NWC'),
                                 feature_group_count=D_MODEL) + dw_b
    u = jax.nn.silu(layernorm(u, ln_g, ln_b))
    return u @ w_out + b_out


def shared_kv(x, kv_norm, w_kv, k_norm):
    b, s, _ = x.shape
    h = rmsnorm(x, kv_norm)
    kv = (h @ w_kv).reshape(b, s, 2, N_GROUPS, N_HEADS, HEAD_DIM)
    k = rmsnorm(kv[:, :, 0], k_norm[:, None, :])
    v = kv[:, :, 1]
    ks, vs = [], []
    for g, (_, d) in enumerate(DILATED_GROUPS):
        pad = ((0, 0), (0, 0), (BLK, 0), (0, 0), (0, 0))
        ks.append(jnp.pad(to_residue_layout(k[:, :, g], d), pad))
        vs.append(jnp.pad(to_residue_layout(v[:, :, g], d), pad))
    return ks, vs


def dilated_branch(q, k_res, v_res, band_bias, d):
    s = q.shape[1]
    q_res = to_residue_layout(q, d)
    b, _, l = q_res.shape[0], q_res.shape[1], q_res.shape[2]
    nb = l // BLK
    scale = HEAD_DIM ** -0.5

    def block(i):
        qb = lax.dynamic_slice_in_dim(q_res, i * BLK, BLK, axis=2).astype(jnp.float32)
        kb = lax.dynamic_slice_in_dim(k_res, i * BLK, 2 * BLK, axis=2).astype(jnp.float32)
        vb = lax.dynamic_slice_in_dim(v_res, i * BLK, 2 * BLK, axis=2).astype(jnp.float32)
        sc = jnp.einsum('brqhd,brkhd->brhqk', qb, kb) * scale + band_bias
        key_ok = (i * BLK - BLK + jnp.arange(2 * BLK)) >= 0
        sc = jnp.where(key_ok, sc, NEG)
        m = jnp.max(sc, axis=-1, keepdims=True)
        p = jnp.exp(sc - m)
        den = jnp.sum(p, axis=-1, keepdims=True)
        o = jnp.einsum('brhqk,brkhd->brqhd', p, vb) / jnp.swapaxes(den, 2, 3)
        lse = jnp.swapaxes((m + jnp.log(den))[..., 0], 2, 3)
        return o, lse

    o, lse = lax.map(block, jnp.arange(nb))
    o = jnp.moveaxis(o, 0, 2).reshape(b, d, l, N_HEADS, HEAD_DIM)
    lse = jnp.moveaxis(lse, 0, 2).reshape(b, d, l, N_HEADS)
    return from_residue_layout(o, s), from_residue_layout(lse, s)


def dilated_attention(x, norm_g, w_q, q_norm, w_o, ks, vs, band_biases):
    b, s, _ = x.shape
    h = rmsnorm(x, norm_g)
    q = (h @ w_q).reshape(b, s, N_GROUPS, N_HEADS, HEAD_DIM)
    q = rmsnorm(q, q_norm[:, None, :])
    outs, lses = [], []
    for g, (_, d) in enumerate(DILATED_GROUPS):
        o_g, lse_g = dilated_branch(q[:, :, g], ks[g], vs[g], band_biases[g], d)
        outs.append(o_g)
        lses.append(lse_g)
    wgt = jax.nn.softmax(jnp.stack(lses, axis=-1), axis=-1)
    o = jnp.einsum('bshg,bshgd->bshd', wgt, jnp.stack(outs, axis=3))
    return o.reshape(b, s, N_HEADS * HEAD_DIM).astype(x.dtype) @ w_o


def moe_ffn(x, norm_g, w_router, w_gate, w_up, w_down):
    h = rmsnorm(x, norm_g).reshape(-1, D_MODEL)
    logits = (h @ w_router).astype(jnp.float32)
    top_val, top_idx = lax.top_k(logits, TOP_K)
    gates = jax.nn.softmax(top_val, axis=-1)
    combine = jnp.sum(jax.nn.one_hot(top_idx, N_EXPERTS, dtype=jnp.float32)
                      * gates[..., None], axis=1)
    y = jnp.zeros(h.shape, jnp.float32)
    for e in range(N_EXPERTS):
        y = y + combine[:, e:e + 1] * swiglu(h, w_gate[e], w_up[e], w_down[e]).astype(jnp.float32)
    return y.astype(x.dtype).reshape(x.shape)


def setup_inputs(seed: int = 0) -> dict:
    key = jax.random.key(seed)
    ks = jax.random.split(key, 32)

    def nrm(k, shape, scale):
        return jax.random.normal(k, shape, jnp.float32) * scale

    def gain(k, shape):
        return 1.0 + 0.02 * jax.random.normal(k, shape, jnp.float32)

    D = D_MODEL
    return {
        "x": nrm(ks[0], (BATCH, SEQ, D), 1.0),
        "rel_bias": nrm(ks[1], (N_BUCKETS, N_GROUPS * N_HEADS), 0.5),
        "conv_norm": gain(ks[2], (N_A_LAYERS, D)),
        "conv_w_in": nrm(ks[3], (N_A_LAYERS, D, 2 * D), D ** -0.5),
        "conv_b_in": nrm(ks[4], (N_A_LAYERS, 2 * D), 0.02),
        "conv_dw_w": nrm(ks[5], (N_A_LAYERS, CONV_WIDTH, D), CONV_WIDTH ** -0.5),
        "conv_dw_b": nrm(ks[6], (N_A_LAYERS, D), 0.02),
        "conv_ln_g": gain(ks[7], (N_A_LAYERS, D)),
        "conv_ln_b": nrm(ks[8], (N_A_LAYERS, D), 0.02),
        "conv_w_out": nrm(ks[9], (N_A_LAYERS, D, D), D ** -0.5),
        "conv_b_out": nrm(ks[10], (N_A_LAYERS, D), 0.02),
        "kv_norm": gain(ks[11], (D,)),
        "w_kv": nrm(ks[12], (D, 2 * Q_WIDTH), D ** -0.5),
        "k_norm": gain(ks[13], (N_GROUPS, HEAD_DIM)),
        "attn_norm": gain(ks[14], (N_B_LAYERS, D)),
        "w_q": nrm(ks[15], (N_B_LAYERS, D, Q_WIDTH), D ** -0.5),
        "q_norm": gain(ks[16], (N_B_LAYERS, N_GROUPS, HEAD_DIM)),
        "w_o": nrm(ks[17], (N_B_LAYERS, N_HEADS * HEAD_DIM, D), (N_HEADS * HEAD_DIM) ** -0.5),
        "ffn_norm": gain(ks[18], (DEPTH, D)),
        "ffn_w_gate": nrm(ks[19], (N_EVEN, D, D_FF), D ** -0.5),
        "ffn_w_up": nrm(ks[20], (N_EVEN, D, D_FF), D ** -0.5),
        "ffn_w_down": nrm(ks[21], (N_EVEN, D_FF, D), D_FF ** -0.5),
        "moe_router": nrm(ks[22], (N_ODD, D, N_EXPERTS), D ** -0.5),
        "moe_w_gate": nrm(ks[23], (N_ODD, N_EXPERTS, D, D_FF_EXPERT), D ** -0.5),
        "moe_w_up": nrm(ks[24], (N_ODD, N_EXPERTS, D, D_FF_EXPERT), D ** -0.5),
        "moe_w_down": nrm(ks[25], (N_ODD, N_EXPERTS, D_FF_EXPERT, D), D_FF_EXPERT ** -0.5),
    }


def reference(x, rel_bias, conv_norm, conv_w_in, conv_b_in, conv_dw_w, conv_dw_b,
              conv_ln_g, conv_ln_b, conv_w_out, conv_b_out, kv_norm, w_kv, k_norm,
              attn_norm, w_q, q_norm, w_o, ffn_norm, ffn_w_gate, ffn_w_up, ffn_w_down,
              moe_router, moe_w_gate, moe_w_up, moe_w_down):
    band_biases = [make_band_bias(rel_bias, g) for g in range(N_GROUPS)]
    ks, vs = None, None
    for l in range(DEPTH):
        if l == N_A_LAYERS:
            ks, vs = shared_kv(x, kv_norm, w_kv, k_norm)
        if l < N_A_LAYERS:
            x = x + conv_module(x, conv_norm[l], conv_w_in[l], conv_b_in[l], conv_dw_w[l],
                                conv_dw_b[l], conv_ln_g[l], conv_ln_b[l], conv_w_out[l],
                                conv_b_out[l])
        else:
            j = l - N_A_LAYERS
            x = x + dilated_attention(x, attn_norm[j], w_q[j], q_norm[j], w_o[j],
                                      ks, vs, band_biases)
        if l % 2 == 0:
            e = l // 2
            x = x + swiglu(rmsnorm(x, ffn_norm[l]), ffn_w_gate[e], ffn_w_up[e], ffn_w_down[e])
        else:
            e = l // 2
            x = x + moe_ffn(x, ffn_norm[l], moe_router[e], moe_w_gate[e], moe_w_up[e],
                            moe_w_down[e])
    return x
```

```python
import functools

import numpy as np
import jax
import jax.numpy as jnp
from jax import lax
from jax.experimental import pallas as pl
from jax.experimental.pallas import tpu as pltpu

F32 = jnp.float32
BF16 = jnp.bfloat16

EPS = 1e-6
NEG = -1e30
CONV_WIDTH = 31
CONV_HALO = 32
N_HEADS = 16
HEAD_DIM = 64
DILATED_GROUPS = ((128, 1), (512, 4), (2048, 16))
BLK = 128
N_BUCKETS = 32
MAX_DISTANCE = 2048
N_EXPERTS = 8
LANE = 128
VMEM_LIMIT_BYTES = 56 * 1024 * 1024

ROW_TILE = 512
CONV_CHUNK = 64
MOE_ROW_TILE = 1024
MOE_FF_TILE = 512
MOE_TOKEN_TILE = 256


def _params(*sem):
    return pltpu.CompilerParams(dimension_semantics=sem, vmem_limit_bytes=VMEM_LIMIT_BYTES)


def _resident(shape):
    nd = len(shape)
    return pl.BlockSpec(shape, lambda *_: (0,) * nd, pipeline_mode=pl.Buffered(1))


def _rms(x, g):
    return x * lax.rsqrt(jnp.mean(x * x, axis=-1, keepdims=True) + EPS) * g


def _conv_in_kernel(x_ref, g_ref, w_ref, b_ref, u_ref):
    h = _rms(x_ref[...], g_ref[...])
    y = jnp.dot(h.astype(BF16), w_ref[...], preferred_element_type=F32) + b_ref[...]
    d = u_ref.shape[-1]
    u_ref[...] = (y[:, :d] * jax.nn.sigmoid(y[:, d:])).astype(u_ref.dtype)


def conv_in(x2, g, w_in, b_in):
    t, d = x2.shape
    return pl.pallas_call(
        _conv_in_kernel,
        out_shape=jax.ShapeDtypeStruct((t, d), BF16),
        grid=(t // ROW_TILE,),
        in_specs=[pl.BlockSpec((ROW_TILE, d), lambda i: (i, 0)),
                  _resident((1, d)), _resident((d, 2 * d)), _resident((1, 2 * d))],
        out_specs=pl.BlockSpec((ROW_TILE, d), lambda i: (i, 0)),
        compiler_params=_params("parallel"),
        name="conv_in",
    )(x2, g, w_in, b_in)


def _conv_out_kernel(u_ref, x_ref, dww_ref, dwb_ref, lng_ref, lnb_ref, w_ref, b_ref, o_ref,
                     buf_ref, conv_ref):
    ts, d = x_ref.shape
    j = pl.program_id(1)

    @pl.when(j == 0)
    def _():
        buf_ref[0:CONV_HALO, :] = jnp.zeros((CONV_HALO, d), F32)

    @pl.when(j > 0)
    def _():
        buf_ref[0:CONV_HALO, :] = buf_ref[ts:ts + CONV_HALO, :]

    buf_ref[CONV_HALO:CONV_HALO + ts, :] = u_ref[...].astype(F32)
    first_tap = CONV_HALO - (CONV_WIDTH - 1)

    def lane_chunk(c, carry):
        lanes = pl.ds(pl.multiple_of(c * LANE, LANE), LANE)
        taps = [dww_ref[k:k + 1, lanes] for k in range(CONV_WIDTH)]
        bias = dwb_ref[:, lanes]
        for r0 in range(0, ts, CONV_CHUNK):
            acc = jnp.broadcast_to(bias, (CONV_CHUNK, LANE))
            for k in range(CONV_WIDTH):
                lo = r0 + first_tap + k
                acc = acc + taps[k] * buf_ref[lo:lo + CONV_CHUNK, lanes]
            conv_ref[r0:r0 + CONV_CHUNK, lanes] = acc
        return carry

    lax.fori_loop(0, d // LANE, lane_chunk, 0)
    acc = conv_ref[...]
    mu = jnp.mean(acc, axis=-1, keepdims=True)
    cen = acc - mu
    var = jnp.mean(cen * cen, axis=-1, keepdims=True)
    y = cen * lax.rsqrt(var + EPS) * lng_ref[...] + lnb_ref[...]
    act = (y * jax.nn.sigmoid(y)).astype(BF16)
    o_ref[...] = x_ref[...] + b_ref[...] + jnp.dot(act, w_ref[...], preferred_element_type=F32)


def conv_out(u3, x3, dw_w, dw_b, ln_g, ln_b, w_out, b_out):
    b, s, d = x3.shape
    ts = ROW_TILE
    row = pl.BlockSpec((None, ts, d), lambda bi, j: (bi, j, 0))
    return pl.pallas_call(
        _conv_out_kernel,
        out_shape=jax.ShapeDtypeStruct((b, s, d), F32),
        grid=(b, s // ts),
        in_specs=[row, row, _resident((CONV_WIDTH, d)), _resident((1, d)), _resident((1, d)),
                  _resident((1, d)), _resident((d, d)), _resident((1, d))],
        out_specs=row,
        scratch_shapes=[pltpu.VMEM((ts + CONV_HALO, d), F32), pltpu.VMEM((ts, d), F32)],
        compiler_params=_params("parallel", "arbitrary"),
        name="conv_out",
    )(u3, x3, dw_w, dw_b, ln_g, ln_b, w_out, b_out)


def _ffn_kernel(x_ref, g_ref, wg_ref, wu_ref, wd_ref, o_ref):
    x = x_ref[...]
    h = _rms(x, g_ref[...]).astype(BF16)
    a = jnp.dot(h, wg_ref[...], preferred_element_type=F32)
    u = jnp.dot(h, wu_ref[...], preferred_element_type=F32)
    act = (a * jax.nn.sigmoid(a) * u).astype(BF16)
    o_ref[...] = x + jnp.dot(act, wd_ref[...], preferred_element_type=F32)


def ffn_dense(x2, g, w_gate, w_up, w_down):
    t, d = x2.shape
    f = w_gate.shape[1]
    row = pl.BlockSpec((ROW_TILE, d), lambda i: (i, 0))
    return pl.pallas_call(
        _ffn_kernel,
        out_shape=jax.ShapeDtypeStruct((t, d), F32),
        grid=(t // ROW_TILE,),
        in_specs=[row, _resident((1, d)), _resident((d, f)), _resident((d, f)), _resident((f, d))],
        out_specs=row,
        compiler_params=_params("parallel"),
        name="ffn_dense",
    )(x2, g, w_gate, w_up, w_down)


def _qkv_kernel(x_ref, gq_ref, gkv_ref, wq_ref, wkv_ref, q_ref, k_ref, v_ref):
    x = x_ref[...]
    xn = x * lax.rsqrt(jnp.mean(x * x, axis=-1, keepdims=True) + EPS)
    hq = (xn * gq_ref[...]).astype(BF16)
    hkv = (xn * gkv_ref[...]).astype(BF16)
    qw = q_ref.shape[-1]
    q_ref[...] = jnp.dot(hq, wq_ref[...], preferred_element_type=F32).astype(q_ref.dtype)
    k_ref[...] = jnp.dot(hkv, wkv_ref[:, :qw], preferred_element_type=F32).astype(k_ref.dtype)
    v_ref[...] = jnp.dot(hkv, wkv_ref[:, qw:], preferred_element_type=F32).astype(v_ref.dtype)


def qkv_proj(x2, gq, gkv, w_q, w_kv):
    t, d = x2.shape
    qw = w_q.shape[1]
    tm = ROW_TILE
    out = pl.BlockSpec((tm, qw), lambda i: (i, 0))
    shp = jax.ShapeDtypeStruct((t, qw), BF16)
    return pl.pallas_call(
        _qkv_kernel,
        out_shape=(shp, shp, shp),
        grid=(t // tm,),
        in_specs=[pl.BlockSpec((tm, d), lambda i: (i, 0)), _resident((1, d)), _resident((1, d)),
                  _resident((d, qw)), _resident((d, 2 * qw))],
        out_specs=(out, out, out),
        compiler_params=_params("parallel"),
        name="qkv_proj",
    )(x2, gq, gkv, w_q, w_kv)


def _t5_bucket(dist):
    n = np.asarray(dist)
    max_exact = N_BUCKETS // 2
    large = max_exact + (np.log(np.maximum(n, 1) / max_exact)
                         / np.log(MAX_DISTANCE / max_exact)
                         * (N_BUCKETS - max_exact)).astype(np.int32)
    large = np.minimum(large, N_BUCKETS - 1)
    return np.where(n < max_exact, n, large).astype(np.int32)


def band_bias(rel_bias, g):
    window, d = DILATED_GROUPS[g]
    span = window // d
    sub = BLK + np.arange(BLK)[:, None] - np.arange(2 * BLK)[None, :]
    in_band = (sub >= 0) & (sub <= span)
    bucket = _t5_bucket(np.clip(sub, 0, span) * d)
    table = rel_bias[:, g * N_HEADS:(g + 1) * N_HEADS]
    bias = jnp.transpose(table[bucket], (2, 0, 1)).astype(F32)
    general = jnp.where(jnp.asarray(in_band), bias, NEG)
    has_prev = np.arange(2 * BLK)[None, :] >= BLK
    first = jnp.where(jnp.asarray(in_band & has_prev), bias, NEG)
    return jnp.stack([first, general])


def _attn_kernel(q_ref, k_ref, v_ref, bias_ref, qg_ref, kg_ref, o_ref, lse_ref, kprev_ref, vprev_ref):
    i = pl.program_id(1)

    @pl.when(i == 0)
    def _():
        kprev_ref[...] = jnp.zeros(kprev_ref.shape, kprev_ref.dtype)
        vprev_ref[...] = jnp.zeros(vprev_ref.shape, vprev_ref.dtype)

    variant = jnp.minimum(i, 1)
    qg = qg_ref[...] * (HEAD_DIM ** -0.5)
    kg = kg_ref[...]
    for h in range(N_HEADS):
        sl = slice(h * HEAD_DIM, (h + 1) * HEAD_DIM)
        qh = _rms(q_ref[:, sl].astype(F32), qg).astype(BF16)
        kh = _rms(k_ref[:, sl].astype(F32), kg).astype(BF16)
        kcat = jnp.concatenate([kprev_ref[:, sl], kh], axis=0)
        vcat = jnp.concatenate([vprev_ref[:, sl], v_ref[:, sl]], axis=0)
        s = lax.dot_general(qh, kcat, (((1,), (1,)), ((), ())), preferred_element_type=F32)
        s = s + bias_ref[variant, h]
        m = jnp.max(s, axis=-1, keepdims=True)
        p = jnp.exp(s - m)
        den = jnp.sum(p, axis=-1, keepdims=True)
        o = jnp.dot(p.astype(BF16), vcat, preferred_element_type=F32) / den
        o_ref[:, sl] = o.astype(o_ref.dtype)
        lse_ref[:, h:h + 1] = m + jnp.log(den)
        kprev_ref[:, sl] = kh
    vprev_ref[...] = v_ref[...]


def dilated_attention_group(q, k, v, bias, qg, kg):
    r, l, w = q.shape
    blk = pl.BlockSpec((None, BLK, w), lambda a, i: (a, i, 0))
    return pl.pallas_call(
        _attn_kernel,
        out_shape=(jax.ShapeDtypeStruct((r, l, w), BF16),
                   jax.ShapeDtypeStruct((r, l, N_HEADS), F32)),
        grid=(r, l // BLK),
        in_specs=[blk, blk, blk, _resident(bias.shape), _resident((1, HEAD_DIM)),
                  _resident((1, HEAD_DIM))],
        out_specs=(blk, pl.BlockSpec((None, BLK, N_HEADS), lambda a, i: (a, i, 0))),
        scratch_shapes=[pltpu.VMEM((BLK, w), BF16), pltpu.VMEM((BLK, w), BF16)],
        compiler_params=_params("parallel", "arbitrary"),
        name="dilated_attn",
    )(q, k, v, bias, qg, kg)


def _to_residue(t, d):
    b, s, w = t.shape
    return jnp.swapaxes(t.reshape(b, s // d, d, w), 1, 2).reshape(b * d, s // d, w)


def _from_residue(t, b, d):
    _, l, w = t.shape
    return jnp.swapaxes(t.reshape(b, d, l, w), 1, 2).reshape(b, l * d, w)


def _attn_out_kernel(x_ref, o0_ref, o1_ref, o2_ref, l0_ref, l1_ref, l2_ref, e_ref, w_ref, o_ref):
    l0, l1, l2 = l0_ref[...], l1_ref[...], l2_ref[...]
    m = jnp.maximum(jnp.maximum(l0, l1), l2)
    e0, e1, e2 = jnp.exp(l0 - m), jnp.exp(l1 - m), jnp.exp(l2 - m)
    den = e0 + e1 + e2
    acc = None
    for e, o in ((e0, o0_ref), (e1, o1_ref), (e2, o2_ref)):
        wgt = jnp.dot((e / den).astype(BF16), e_ref[...], preferred_element_type=F32)
        term = wgt * o[...].astype(F32)
        acc = term if acc is None else acc + term
    o_ref[...] = x_ref[...] + jnp.dot(acc.astype(BF16), w_ref[...], preferred_element_type=F32)


def attn_out(x2, outs, lses, w_o):
    t, d = x2.shape
    tm = ROW_TILE
    expand = jnp.asarray(np.repeat(np.eye(N_HEADS, dtype=np.float32), HEAD_DIM, axis=1), BF16)
    row = pl.BlockSpec((tm, d), lambda i: (i, 0))
    lrow = pl.BlockSpec((tm, N_HEADS), lambda i: (i, 0))
    return pl.pallas_call(
        _attn_out_kernel,
        out_shape=jax.ShapeDtypeStruct((t, d), F32),
        grid=(t // tm,),
        in_specs=[row, row, row, row, lrow, lrow, lrow,
                  _resident((N_HEADS, d)), _resident((d, d))],
        out_specs=row,
        compiler_params=_params("parallel"),
        name="attn_out",
    )(x2, *outs, *lses, expand, w_o)


def _router_kernel(x_ref, g_ref, w_ref, idx_ref, gate_ref):
    h = _rms(x_ref[...], g_ref[...])
    w = w_ref[...]
    h_hi = h.astype(BF16)
    h_lo = (h - h_hi.astype(F32)).astype(BF16)
    w_hi = w.astype(BF16)
    w_lo = (w - w_hi.astype(F32)).astype(BF16)
    logits = (jnp.dot(h_hi, w_hi, preferred_element_type=F32)
              + jnp.dot(h_lo, w_hi, preferred_element_type=F32)
              + jnp.dot(h_hi, w_lo, preferred_element_type=F32))
    ne = N_EXPERTS
    lane = lax.broadcasted_iota(jnp.int32, logits.shape, 1)
    logits = jnp.where(lane < ne, logits, -jnp.inf)
    m1 = jnp.max(logits, axis=-1, keepdims=True)
    i1 = jnp.min(jnp.where(logits == m1, lane, ne), axis=-1, keepdims=True)
    rest = jnp.where(lane == i1, -jnp.inf, logits)
    m2 = jnp.max(rest, axis=-1, keepdims=True)
    i2 = jnp.min(jnp.where(rest == m2, lane, ne), axis=-1, keepdims=True)
    e2 = jnp.exp(m2 - m1)
    g1 = 1.0 / (1.0 + e2)
    g2 = e2 / (1.0 + e2)
    col = lax.broadcasted_iota(jnp.int32, idx_ref.shape, 1)
    idx_ref[...] = jnp.where(col == 0, i1, i2)
    gate_ref[...] = jnp.where(col == 0, g1, g2)


def router(x2, g, w_router):
    t, d = x2.shape
    tm = ROW_TILE
    two = pl.BlockSpec((tm, 2), lambda i: (i, 0))
    return pl.pallas_call(
        _router_kernel,
        out_shape=(jax.ShapeDtypeStruct((t, 2), jnp.int32), jax.ShapeDtypeStruct((t, 2), F32)),
        grid=(t // tm,),
        in_specs=[pl.BlockSpec((tm, d), lambda i: (i, 0)), _resident((1, d)),
                  _resident((d, LANE))],
        out_specs=(two, two),
        compiler_params=_params("parallel"),
        name="moe_router",
    )(x2, g, _pad_cols(w_router, LANE))


def routing_plan(idx, row_tile):
    t = idx.shape[0]
    e = idx.reshape(-1)
    onehot = (e[:, None] == jnp.arange(N_EXPERTS, dtype=jnp.int32)[None, :]).astype(jnp.int32)
    csum = jnp.cumsum(onehot, axis=0)
    rank = jnp.take_along_axis(csum, e[:, None], axis=1)[:, 0] - 1
    counts = csum[-1]
    tiles = (counts + row_tile - 1) // row_tile
    tile_end = jnp.cumsum(tiles)
    start = (tile_end - tiles) * row_tile
    pos = (start[e] + rank).astype(jnp.int32)
    n_tiles = 2 * t // row_tile + N_EXPERTS
    tile_id = jnp.arange(n_tiles, dtype=jnp.int32)
    tile_expert = jnp.minimum(jnp.sum(tile_id[:, None] >= tile_end[None, :], axis=1),
                              N_EXPERTS - 1).astype(jnp.int32)
    tile_valid = (tile_id < tile_end[-1]).astype(jnp.int32)
    return pos, tile_expert, tile_valid, n_tiles


def _dispatch_kernel(pos_ref, x_ref, g_ref, xs_in_ref, xs_ref, h_ref, sem):
    del xs_in_ref
    tt = x_ref.shape[0]
    h_ref[...] = _rms(x_ref[...], g_ref[...])

    def row_copy(a):
        return pltpu.make_async_copy(h_ref.at[pl.ds(a // 2, 1)], xs_ref.at[pl.ds(pos_ref[0, 0, a], 1)], sem)

    def start(a, c):
        row_copy(a).start()
        return c

    def wait(a, c):
        row_copy(a).wait()
        return c

    lax.fori_loop(0, 2 * tt, start, 0)
    lax.fori_loop(0, 2 * tt, wait, 0)


def dispatch(x2, g, pos, n_rows):
    t, d = x2.shape
    tt = MOE_TOKEN_TILE
    pos3 = pos.reshape(t // tt, 1, 2 * tt)
    xs0 = jnp.zeros((n_rows, d), F32)
    return pl.pallas_call(
        _dispatch_kernel,
        out_shape=jax.ShapeDtypeStruct((n_rows, d), F32),
        grid=(t // tt,),
        in_specs=[pl.BlockSpec((1, 1, 2 * tt), lambda i: (i, 0, 0), memory_space=pltpu.SMEM),
                  pl.BlockSpec((tt, d), lambda i: (i, 0)), _resident((1, d)),
                  pl.BlockSpec(memory_space=pl.ANY)],
        out_specs=pl.BlockSpec(memory_space=pl.ANY),
        scratch_shapes=[pltpu.VMEM((tt, d), F32), pltpu.SemaphoreType.DMA(())],
        input_output_aliases={3: 0},
        compiler_params=_params("arbitrary"),
        name="moe_dispatch",
    )(pos3, x2, g, xs0)


def _experts_kernel(te_ref, tv_ref, x_ref, wg_ref, wu_ref, wd_ref, o_ref, xb_ref):
    i, j = pl.program_id(0), pl.program_id(1)

    @pl.when(j == 0)
    def _():
        xb_ref[...] = x_ref[...].astype(BF16)
        o_ref[...] = jnp.zeros(o_ref.shape, o_ref.dtype)

    @pl.when(tv_ref[i] > 0)
    def _():
        xb = xb_ref[...]
        a = jnp.dot(xb, wg_ref[...], preferred_element_type=F32)
        u = jnp.dot(xb, wu_ref[...], preferred_element_type=F32)
        act = (a * jax.nn.sigmoid(a) * u).astype(BF16)
        o_ref[...] += jnp.dot(act, wd_ref[...], preferred_element_type=F32)


def experts(xs, tile_expert, tile_valid, w_gate, w_up, w_down):
    n_rows, d = xs.shape
    f = w_gate.shape[-1]
    tm, tf = MOE_ROW_TILE, MOE_FF_TILE
    grid_spec = pltpu.PrefetchScalarGridSpec(
        num_scalar_prefetch=2,
        grid=(n_rows // tm, f // tf),
        in_specs=[pl.BlockSpec((tm, d), lambda i, j, te, tv: (i, 0)),
                  pl.BlockSpec((None, d, tf), lambda i, j, te, tv: (te[i], 0, j)),
                  pl.BlockSpec((None, d, tf), lambda i, j, te, tv: (te[i], 0, j)),
                  pl.BlockSpec((None, tf, d), lambda i, j, te, tv: (te[i], j, 0))],
        out_specs=pl.BlockSpec((tm, d), lambda i, j, te, tv: (i, 0)),
        scratch_shapes=[pltpu.VMEM((tm, d), BF16)],
    )
    return pl.pallas_call(
        _experts_kernel,
        out_shape=jax.ShapeDtypeStruct((n_rows, d), F32),
        grid_spec=grid_spec,
        compiler_params=_params("arbitrary", "arbitrary"),
        name="moe_experts",
    )(tile_expert, tile_valid, xs, w_gate, w_up, w_down)


def _combine_kernel(pos_ref, x_ref, gate_ref, ys_ref, o_ref, buf_ref, sem):
    tt = x_ref.shape[0]

    def row_copy(a):
        return pltpu.make_async_copy(ys_ref.at[pl.ds(pos_ref[0, 0, a], 1)],
                                     buf_ref.at[a % 2, pl.ds(a // 2, 1)], sem)

    def start(a, c):
        row_copy(a).start()
        return c

    def wait(a, c):
        row_copy(a).wait()
        return c

    lax.fori_loop(0, 2 * tt, start, 0)
    lax.fori_loop(0, 2 * tt, wait, 0)
    gates = gate_ref[...]
    o_ref[...] = x_ref[...] + gates[:, 0:1] * buf_ref[0] + gates[:, 1:2] * buf_ref[1]


def combine(x2, gates, pos, ys):
    t, d = x2.shape
    tt = MOE_TOKEN_TILE
    pos3 = pos.reshape(t // tt, 1, 2 * tt)
    row = pl.BlockSpec((tt, d), lambda i: (i, 0))
    return pl.pallas_call(
        _combine_kernel,
        out_shape=jax.ShapeDtypeStruct((t, d), F32),
        grid=(t // tt,),
        in_specs=[pl.BlockSpec((1, 1, 2 * tt), lambda i: (i, 0, 0), memory_space=pltpu.SMEM),
                  row, pl.BlockSpec((tt, 2), lambda i: (i, 0)),
                  pl.BlockSpec(memory_space=pl.ANY)],
        out_specs=row,
        scratch_shapes=[pltpu.VMEM((2, tt, d), F32), pltpu.SemaphoreType.DMA(())],
        compiler_params=_params("arbitrary"),
        name="moe_combine",
    )(pos3, x2, gates, ys)


def _pad_cols(w, mult):
    pad = (-w.shape[-1]) % mult
    return jnp.pad(w, ((0, 0),) * (w.ndim - 1) + ((0, pad),))


def kernel(x, rel_bias, conv_norm, conv_w_in, conv_b_in, conv_dw_w, conv_dw_b, conv_ln_g, conv_ln_b, conv_w_out, conv_b_out, kv_norm, w_kv, k_norm, attn_norm, w_q, q_norm, w_o, ffn_norm, ffn_w_gate, ffn_w_up, ffn_w_down, moe_router, moe_w_gate, moe_w_up, moe_w_down):
    b, s, d = x.shape
    t = b * s
    row = lambda v: v.reshape(1, -1)

    x2 = x.reshape(t, d)
    u = conv_in(x2, row(conv_norm[0]), conv_w_in[0].astype(BF16), row(conv_b_in[0]))
    x3 = conv_out(u.reshape(b, s, d), x, conv_dw_w[0], row(conv_dw_b[0]), row(conv_ln_g[0]),
                  row(conv_ln_b[0]), conv_w_out[0].astype(BF16), row(conv_b_out[0]))
    x2 = x3.reshape(t, d)
    wg = _pad_cols(ffn_w_gate[0], LANE).astype(BF16)
    wu = _pad_cols(ffn_w_up[0], LANE).astype(BF16)
    wd = jnp.pad(ffn_w_down[0], ((0, wg.shape[1] - ffn_w_down.shape[1]), (0, 0))).astype(BF16)
    x2 = ffn_dense(x2, row(ffn_norm[0]), wg, wu, wd)

    q, k, v = qkv_proj(x2, row(attn_norm[0]), row(kv_norm), w_q[0].astype(BF16), w_kv.astype(BF16))
    hw = N_HEADS * HEAD_DIM
    outs, lses = [], []
    for g, (_, dil) in enumerate(DILATED_GROUPS):
        sl = slice(g * hw, (g + 1) * hw)
        qr, kr, vr = (_to_residue(a[:, sl].reshape(b, s, hw), dil) for a in (q, k, v))
        o_g, lse_g = dilated_attention_group(qr, kr, vr, band_bias(rel_bias, g),
                                             row(q_norm[0, g]), row(k_norm[g]))
        outs.append(_from_residue(o_g, b, dil).reshape(t, hw))
        lses.append(_from_residue(lse_g, b, dil).reshape(t, N_HEADS))
    x2 = attn_out(x2, outs, lses, w_o[0].astype(BF16))

    g_moe = row(ffn_norm[1])
    idx, gates = router(x2, g_moe, moe_router[0])
    pos, tile_expert, tile_valid, n_tiles = routing_plan(idx, MOE_ROW_TILE)
    xs = dispatch(x2, g_moe, pos, n_tiles * MOE_ROW_TILE)
    ys = experts(xs, tile_expert, tile_valid, moe_w_gate[0].astype(BF16), moe_w_up[0].astype(BF16),
                 moe_w_down[0].astype(BF16))
    return combine(x2, gates, pos, ys).reshape(b, s, d)
```

```python
import functools

import numpy as np
import jax
import jax.numpy as jnp
from jax import lax
from jax.experimental import pallas as pl
from jax.experimental.pallas import tpu as pltpu

F32 = jnp.float32
BF16 = jnp.bfloat16

EPS = 1e-6
NEG = -1e30
CONV_WIDTH = 31
CONV_HALO = 32
N_HEADS = 16
HEAD_DIM = 64
DILATED_GROUPS = ((128, 1), (512, 4), (2048, 16))
BLK = 128
N_BUCKETS = 32
MAX_DISTANCE = 2048
N_EXPERTS = 8
LANE = 128
VMEM_LIMIT_BYTES = 56 * 1024 * 1024

ROW_TILE = 512
CONV_CHUNK = 64
MOE_ROW_TILE = 1024
MOE_FF_TILE = 512
MOE_TOKEN_TILE = 256
ROW_DMA_UNROLL = 8


def _params(*sem, row_dma=False):
    return pltpu.CompilerParams(dimension_semantics=sem, vmem_limit_bytes=VMEM_LIMIT_BYTES,
                                disable_bounds_checks=row_dma)


def _resident(shape):
    nd = len(shape)
    return pl.BlockSpec(shape, lambda *_: (0,) * nd, pipeline_mode=pl.Buffered(1))


def _rms(x, g):
    return x * lax.rsqrt(jnp.mean(x * x, axis=-1, keepdims=True) + EPS) * g


def _conv_in_kernel(x_ref, g_ref, w_ref, b_ref, u_ref):
    h = _rms(x_ref[...], g_ref[...])
    y = jnp.dot(h.astype(BF16), w_ref[...], preferred_element_type=F32) + b_ref[...]
    d = u_ref.shape[-1]
    u_ref[...] = (y[:, :d] * jax.nn.sigmoid(y[:, d:])).astype(u_ref.dtype)


def conv_in(x2, g, w_in, b_in):
    t, d = x2.shape
    return pl.pallas_call(
        _conv_in_kernel,
        out_shape=jax.ShapeDtypeStruct((t, d), BF16),
        grid=(t // ROW_TILE,),
        in_specs=[pl.BlockSpec((ROW_TILE, d), lambda i: (i, 0)),
                  _resident((1, d)), _resident((d, 2 * d)), _resident((1, 2 * d))],
        out_specs=pl.BlockSpec((ROW_TILE, d), lambda i: (i, 0)),
        compiler_params=_params("parallel"),
        name="conv_in",
    )(x2, g, w_in, b_in)


def _conv_out_kernel(u_ref, x_ref, dww_ref, dwb_ref, lng_ref, lnb_ref, w_ref, b_ref, o_ref,
                     buf_ref, conv_ref):
    ts, d = x_ref.shape
    j = pl.program_id(1)

    @pl.when(j == 0)
    def _():
        buf_ref[0:CONV_HALO, :] = jnp.zeros((CONV_HALO, d), F32)

    @pl.when(j > 0)
    def _():
        buf_ref[0:CONV_HALO, :] = buf_ref[ts:ts + CONV_HALO, :]

    buf_ref[CONV_HALO:CONV_HALO + ts, :] = u_ref[...].astype(F32)
    first_tap = CONV_HALO - (CONV_WIDTH - 1)

    def lane_chunk(c, carry):
        lanes = pl.ds(pl.multiple_of(c * LANE, LANE), LANE)
        taps = [dww_ref[k:k + 1, lanes] for k in range(CONV_WIDTH)]
        bias = dwb_ref[:, lanes]
        for r0 in range(0, ts, CONV_CHUNK):
            acc = jnp.broadcast_to(bias, (CONV_CHUNK, LANE))
            for k in range(CONV_WIDTH):
                lo = r0 + first_tap + k
                acc = acc + taps[k] * buf_ref[lo:lo + CONV_CHUNK, lanes]
            conv_ref[r0:r0 + CONV_CHUNK, lanes] = acc
        return carry

    lax.fori_loop(0, d // LANE, lane_chunk, 0)
    acc = conv_ref[...]
    mu = jnp.mean(acc, axis=-1, keepdims=True)
    cen = acc - mu
    var = jnp.mean(cen * cen, axis=-1, keepdims=True)
    y = cen * lax.rsqrt(var + EPS) * lng_ref[...] + lnb_ref[...]
    act = (y * jax.nn.sigmoid(y)).astype(BF16)
    o_ref[...] = x_ref[...] + b_ref[...] + jnp.dot(act, w_ref[...], preferred_element_type=F32)


def conv_out(u3, x3, dw_w, dw_b, ln_g, ln_b, w_out, b_out):
    b, s, d = x3.shape
    ts = ROW_TILE
    row = pl.BlockSpec((None, ts, d), lambda bi, j: (bi, j, 0))
    return pl.pallas_call(
        _conv_out_kernel,
        out_shape=jax.ShapeDtypeStruct((b, s, d), F32),
        grid=(b, s // ts),
        in_specs=[row, row, _resident((CONV_WIDTH, d)), _resident((1, d)), _resident((1, d)),
                  _resident((1, d)), _resident((d, d)), _resident((1, d))],
        out_specs=row,
        scratch_shapes=[pltpu.VMEM((ts + CONV_HALO, d), F32), pltpu.VMEM((ts, d), F32)],
        compiler_params=_params("parallel", "arbitrary"),
        name="conv_out",
    )(u3, x3, dw_w, dw_b, ln_g, ln_b, w_out, b_out)


def _ffn_kernel(x_ref, g_ref, wg_ref, wu_ref, wd_ref, o_ref):
    x = x_ref[...]
    h = _rms(x, g_ref[...]).astype(BF16)
    a = jnp.dot(h, wg_ref[...], preferred_element_type=F32)
    u = jnp.dot(h, wu_ref[...], preferred_element_type=F32)
    act = (a * jax.nn.sigmoid(a) * u).astype(BF16)
    o_ref[...] = x + jnp.dot(act, wd_ref[...], preferred_element_type=F32)


def ffn_dense(x2, g, w_gate, w_up, w_down):
    t, d = x2.shape
    f = w_gate.shape[1]
    row = pl.BlockSpec((ROW_TILE, d), lambda i: (i, 0))
    return pl.pallas_call(
        _ffn_kernel,
        out_shape=jax.ShapeDtypeStruct((t, d), F32),
        grid=(t // ROW_TILE,),
        in_specs=[row, _resident((1, d)), _resident((d, f)), _resident((d, f)), _resident((f, d))],
        out_specs=row,
        compiler_params=_params("parallel"),
        name="ffn_dense",
    )(x2, g, w_gate, w_up, w_down)


def _head_rms_store(y, gain_ref, out_ref):
    low = lax.broadcasted_iota(jnp.int32, (1, LANE), 1) < HEAD_DIM
    for c in range(y.shape[-1] // LANE):
        cols = slice(c * LANE, (c + 1) * LANE)
        blk = y[:, cols]
        sq = blk * blk
        ms_lo = jnp.sum(jnp.where(low, sq, 0.0), axis=-1, keepdims=True) * (1.0 / HEAD_DIM)
        ms_hi = jnp.sum(jnp.where(low, 0.0, sq), axis=-1, keepdims=True) * (1.0 / HEAD_DIM)
        inv = jnp.where(low, lax.rsqrt(ms_lo + EPS), lax.rsqrt(ms_hi + EPS))
        out_ref[:, cols] = (blk * inv * gain_ref[:, cols]).astype(out_ref.dtype)


def _qkv_kernel(x_ref, gq_ref, gkv_ref, wq_ref, wkv_ref, qn_ref, kn_ref, q_ref, k_ref, v_ref):
    x = x_ref[...]
    xn = x * lax.rsqrt(jnp.mean(x * x, axis=-1, keepdims=True) + EPS)
    hq = (xn * gq_ref[...]).astype(BF16)
    hkv = (xn * gkv_ref[...]).astype(BF16)
    qw = q_ref.shape[-1]
    _head_rms_store(jnp.dot(hq, wq_ref[...], preferred_element_type=F32), qn_ref, q_ref)
    _head_rms_store(jnp.dot(hkv, wkv_ref[:, :qw], preferred_element_type=F32), kn_ref, k_ref)
    v_ref[...] = jnp.dot(hkv, wkv_ref[:, qw:], preferred_element_type=F32).astype(v_ref.dtype)


def qkv_proj(x2, gq, gkv, w_q, w_kv, qn, kn):
    t, d = x2.shape
    qw = w_q.shape[1]
    tm = ROW_TILE
    out = pl.BlockSpec((tm, qw), lambda i: (i, 0))
    shp = jax.ShapeDtypeStruct((t, qw), BF16)
    return pl.pallas_call(
        _qkv_kernel,
        out_shape=(shp, shp, shp),
        grid=(t // tm,),
        in_specs=[pl.BlockSpec((tm, d), lambda i: (i, 0)), _resident((1, d)), _resident((1, d)),
                  _resident((d, qw)), _resident((d, 2 * qw)), _resident((1, qw)), _resident((1, qw))],
        out_specs=(out, out, out),
        compiler_params=_params("parallel"),
        name="qkv_proj",
    )(x2, gq, gkv, w_q, w_kv, qn, kn)


def _t5_bucket(dist):
    n = np.asarray(dist)
    max_exact = N_BUCKETS // 2
    large = max_exact + (np.log(np.maximum(n, 1) / max_exact)
                         / np.log(MAX_DISTANCE / max_exact)
                         * (N_BUCKETS - max_exact)).astype(np.int32)
    large = np.minimum(large, N_BUCKETS - 1)
    return np.where(n < max_exact, n, large).astype(np.int32)


def band_bias(rel_bias, g):
    window, d = DILATED_GROUPS[g]
    span = window // d
    sub = BLK + np.arange(BLK)[:, None] - np.arange(2 * BLK)[None, :]
    in_band = (sub >= 0) & (sub <= span)
    bucket = _t5_bucket(np.clip(sub, 0, span) * d)
    table = rel_bias[:, g * N_HEADS:(g + 1) * N_HEADS]
    bias = jnp.transpose(table[bucket], (2, 0, 1)).astype(F32)
    general = jnp.where(jnp.asarray(in_band), bias, NEG)
    has_prev = np.arange(2 * BLK)[None, :] >= BLK
    first = jnp.where(jnp.asarray(in_band & has_prev), bias, NEG)
    return jnp.stack([first, general])


def _attn_kernel(q_ref, k_ref, v_ref, bias_ref, hm_ref, o_ref, lse_ref,
                 kprev_ref, vprev_ref, s_ref, p_ref, m_ref):
    i = pl.program_id(1)

    @pl.when(i == 0)
    def _():
        kprev_ref[...] = jnp.zeros(kprev_ref.shape, kprev_ref.dtype)
        vprev_ref[...] = jnp.zeros(vprev_ref.shape, vprev_ref.dtype)

    variant = jnp.minimum(i, 1)
    n_cols = q_ref.shape[-1] // LANE

    for c in range(n_cols):
        cols = slice(c * LANE, (c + 1) * LANE)
        q2 = q_ref[:, cols]
        kcat = jnp.concatenate([kprev_ref[:, cols], k_ref[:, cols]], axis=0)
        for half in range(2):
            h = 2 * c + half
            s = lax.dot_general(q2 * hm_ref[half:half + 1, :], kcat, (((1,), (1,)), ((), ())),
                                preferred_element_type=F32) + bias_ref[variant, h]
            s_ref[h] = s
            m_ref[h] = jnp.broadcast_to(jnp.max(s, axis=-1, keepdims=True), (BLK, LANE))

    for h in range(N_HEADS):
        m = m_ref[h]
        p_ref[h] = jnp.exp(s_ref[h] - jnp.concatenate([m, m], axis=-1)).astype(p_ref.dtype)

    lane = lax.broadcasted_iota(jnp.int32, (BLK, LANE), 1)
    low = lane < HEAD_DIM
    ones = jnp.ones((2 * BLK, LANE), BF16)
    m_tile = jnp.zeros((BLK, LANE), F32)
    l_tile = jnp.ones((BLK, LANE), F32)
    for c in range(n_cols):
        cols = slice(c * LANE, (c + 1) * LANE)
        vcat = jnp.concatenate([vprev_ref[:, cols], v_ref[:, cols]], axis=0)
        acc = []
        for half in range(2):
            rhs = jnp.concatenate([vcat * hm_ref[half:half + 1, :], ones], axis=-1)
            acc.append(jnp.dot(p_ref[2 * c + half], rhs, preferred_element_type=F32))
        den = jnp.where(low, acc[0][:, LANE:], acc[1][:, LANE:])
        o_ref[:, cols] = ((acc[0][:, :LANE] + acc[1][:, :LANE]) / den).astype(o_ref.dtype)
        for half in range(2):
            h = 2 * c + half
            m_tile = jnp.where(lane == h, m_ref[h], m_tile)
            l_tile = jnp.where(lane == h, acc[half][:, LANE:], l_tile)
    lse_ref[...] = m_tile + jnp.log(l_tile)
    kprev_ref[...] = k_ref[...]
    vprev_ref[...] = v_ref[...]


def dilated_attention_group(q, k, v, bias):
    r, l, w = q.shape
    blk = pl.BlockSpec((None, BLK, w), lambda a, i: (a, i, 0))
    lane = np.arange(LANE)[None, :] < HEAD_DIM
    head_mask = jnp.asarray(np.concatenate([lane, ~lane]).astype(np.float32), BF16)
    return pl.pallas_call(
        _attn_kernel,
        out_shape=(jax.ShapeDtypeStruct((r, l, w), BF16),
                   jax.ShapeDtypeStruct((r, l, LANE), F32)),
        grid=(r, l // BLK),
        in_specs=[blk, blk, blk, _resident(bias.shape), _resident((2, LANE))],
        out_specs=(blk, pl.BlockSpec((None, BLK, LANE), lambda a, i: (a, i, 0))),
        scratch_shapes=[pltpu.VMEM((BLK, w), BF16), pltpu.VMEM((BLK, w), BF16),
                        pltpu.VMEM((N_HEADS, BLK, 2 * BLK), F32),
                        pltpu.VMEM((N_HEADS, BLK, 2 * BLK), BF16),
                        pltpu.VMEM((N_HEADS, BLK, LANE), F32)],
        compiler_params=_params("parallel", "arbitrary"),
        name="dilated_attn",
    )(q, k, v, bias, head_mask)


def _to_residue(t, d):
    b, s, w = t.shape
    return jnp.swapaxes(t.reshape(b, s // d, d, w), 1, 2).reshape(b * d, s // d, w)


def _from_residue(t, b, d):
    _, l, w = t.shape
    return jnp.swapaxes(t.reshape(b, d, l, w), 1, 2).reshape(b, l * d, w)


def _attn_out_kernel(x_ref, o0_ref, o1_ref, o2_ref, l0_ref, l1_ref, l2_ref, e_ref, w_ref, o_ref):
    l0, l1, l2 = l0_ref[...], l1_ref[...], l2_ref[...]
    m = jnp.maximum(jnp.maximum(l0, l1), l2)
    e0, e1, e2 = jnp.exp(l0 - m), jnp.exp(l1 - m), jnp.exp(l2 - m)
    den = e0 + e1 + e2
    acc = None
    for e, o in ((e0, o0_ref), (e1, o1_ref), (e2, o2_ref)):
        wgt = jnp.dot((e / den).astype(BF16), e_ref[...], preferred_element_type=F32)
        term = wgt * o[...].astype(F32)
        acc = term if acc is None else acc + term
    o_ref[...] = x_ref[...] + jnp.dot(acc.astype(BF16), w_ref[...], preferred_element_type=F32)


def attn_out(x2, outs, lses, w_o):
    t, d = x2.shape
    tm = ROW_TILE
    expand = np.zeros((LANE, d), np.float32)
    expand[:N_HEADS] = np.repeat(np.eye(N_HEADS, dtype=np.float32), HEAD_DIM, axis=1)
    expand = jnp.asarray(expand, BF16)
    row = pl.BlockSpec((tm, d), lambda i: (i, 0))
    lrow = pl.BlockSpec((tm, LANE), lambda i: (i, 0))
    return pl.pallas_call(
        _attn_out_kernel,
        out_shape=jax.ShapeDtypeStruct((t, d), F32),
        grid=(t // tm,),
        in_specs=[row, row, row, row, lrow, lrow, lrow,
                  _resident((LANE, d)), _resident((d, d))],
        out_specs=row,
        compiler_params=_params("parallel"),
        name="attn_out",
    )(x2, *outs, *lses, expand, w_o)


def _router_kernel(x_ref, g_ref, w_ref, idx_ref, gate_ref):
    h = _rms(x_ref[...], g_ref[...])
    w = w_ref[...]
    h_hi = h.astype(BF16)
    h_lo = (h - h_hi.astype(F32)).astype(BF16)
    w_hi = w.astype(BF16)
    w_lo = (w - w_hi.astype(F32)).astype(BF16)
    logits = (jnp.dot(h_hi, w_hi, preferred_element_type=F32)
              + jnp.dot(h_lo, w_hi, preferred_element_type=F32)
              + jnp.dot(h_hi, w_lo, preferred_element_type=F32))
    ne = N_EXPERTS
    lane = lax.broadcasted_iota(jnp.int32, logits.shape, 1)
    logits = jnp.where(lane < ne, logits, -jnp.inf)
    m1 = jnp.max(logits, axis=-1, keepdims=True)
    i1 = jnp.min(jnp.where(logits == m1, lane, ne), axis=-1, keepdims=True)
    rest = jnp.where(lane == i1, -jnp.inf, logits)
    m2 = jnp.max(rest, axis=-1, keepdims=True)
    i2 = jnp.min(jnp.where(rest == m2, lane, ne), axis=-1, keepdims=True)
    e2 = jnp.exp(m2 - m1)
    g1 = 1.0 / (1.0 + e2)
    g2 = e2 / (1.0 + e2)
    col = lax.broadcasted_iota(jnp.int32, idx_ref.shape, 1)
    idx_ref[...] = jnp.where(col == 0, i1, i2)
    gate_ref[...] = jnp.where(col == 0, g1, g2)


def router(x2, g, w_router):
    t, d = x2.shape
    tm = ROW_TILE
    two = pl.BlockSpec((tm, 2), lambda i: (i, 0))
    return pl.pallas_call(
        _router_kernel,
        out_shape=(jax.ShapeDtypeStruct((t, 2), jnp.int32), jax.ShapeDtypeStruct((t, 2), F32)),
        grid=(t // tm,),
        in_specs=[pl.BlockSpec((tm, d), lambda i: (i, 0)), _resident((1, d)),
                  _resident((d, LANE))],
        out_specs=(two, two),
        compiler_params=_params("parallel"),
        name="moe_router",
    )(x2, g, _pad_cols(w_router, LANE))


def routing_plan(idx, row_tile):
    t = idx.shape[0]
    e = idx.reshape(-1)
    onehot = (e[:, None] == jnp.arange(N_EXPERTS, dtype=jnp.int32)[None, :]).astype(jnp.int32)
    csum = jnp.cumsum(onehot, axis=0)
    rank = jnp.take_along_axis(csum, e[:, None], axis=1)[:, 0] - 1
    counts = csum[-1]
    tiles = (counts + row_tile - 1) // row_tile
    tile_end = jnp.cumsum(tiles)
    start = (tile_end - tiles) * row_tile
    pos = (start[e] + rank).astype(jnp.int32)
    n_tiles = 2 * t // row_tile + N_EXPERTS
    tile_id = jnp.arange(n_tiles, dtype=jnp.int32)
    tile_expert = jnp.minimum(jnp.sum(tile_id[:, None] >= tile_end[None, :], axis=1),
                              N_EXPERTS - 1).astype(jnp.int32)
    tile_valid = (tile_id < tile_end[-1]).astype(jnp.int32)
    return pos, tile_expert, tile_valid, n_tiles


def _dispatch_kernel(pos_ref, x_ref, g_ref, xs_in_ref, xs_ref, h_ref, sem):
    del xs_in_ref
    tt = x_ref.shape[0]
    h_ref[...] = _rms(x_ref[...], g_ref[...])

    def row_copy(a):
        return pltpu.make_async_copy(h_ref.at[pl.ds(a // 2, 1)], xs_ref.at[pl.ds(pos_ref[0, 0, a], 1)], sem)

    def start(a, c):
        row_copy(a).start()
        return c

    def wait(a, c):
        row_copy(a).wait()
        return c

    lax.fori_loop(0, 2 * tt, start, 0, unroll=ROW_DMA_UNROLL)
    lax.fori_loop(0, 2 * tt, wait, 0, unroll=ROW_DMA_UNROLL)


def dispatch(x2, g, pos, n_rows):
    t, d = x2.shape
    tt = MOE_TOKEN_TILE
    pos3 = pos.reshape(t // tt, 1, 2 * tt)
    xs0 = jnp.zeros((n_rows, d), F32)
    return pl.pallas_call(
        _dispatch_kernel,
        out_shape=jax.ShapeDtypeStruct((n_rows, d), F32),
        grid=(t // tt,),
        in_specs=[pl.BlockSpec((1, 1, 2 * tt), lambda i: (i, 0, 0), memory_space=pltpu.SMEM),
                  pl.BlockSpec((tt, d), lambda i: (i, 0)), _resident((1, d)),
                  pl.BlockSpec(memory_space=pl.ANY)],
        out_specs=pl.BlockSpec(memory_space=pl.ANY),
        scratch_shapes=[pltpu.VMEM((tt, d), F32), pltpu.SemaphoreType.DMA(())],
        input_output_aliases={3: 0},
        compiler_params=_params("arbitrary", row_dma=True),
        name="moe_dispatch",
    )(pos3, x2, g, xs0)


def _experts_kernel(te_ref, tv_ref, x_ref, wg_ref, wu_ref, wd_ref, o_ref, xb_ref):
    i, j = pl.program_id(0), pl.program_id(1)

    @pl.when(j == 0)
    def _():
        xb_ref[...] = x_ref[...].astype(BF16)
        o_ref[...] = jnp.zeros(o_ref.shape, o_ref.dtype)

    @pl.when(tv_ref[i] > 0)
    def _():
        xb = xb_ref[...]
        a = jnp.dot(xb, wg_ref[...], preferred_element_type=F32)
        u = jnp.dot(xb, wu_ref[...], preferred_element_type=F32)
        act = (a * jax.nn.sigmoid(a) * u).astype(BF16)
        o_ref[...] += jnp.dot(act, wd_ref[...], preferred_element_type=F32)


def experts(xs, tile_expert, tile_valid, w_gate, w_up, w_down):
    n_rows, d = xs.shape
    f = w_gate.shape[-1]
    tm, tf = MOE_ROW_TILE, MOE_FF_TILE
    grid_spec = pltpu.PrefetchScalarGridSpec(
        num_scalar_prefetch=2,
        grid=(n_rows // tm, f // tf),
        in_specs=[pl.BlockSpec((tm, d), lambda i, j, te, tv: (i, 0)),
                  pl.BlockSpec((None, d, tf), lambda i, j, te, tv: (te[i], 0, j)),
                  pl.BlockSpec((None, d, tf), lambda i, j, te, tv: (te[i], 0, j)),
                  pl.BlockSpec((None, tf, d), lambda i, j, te, tv: (te[i], j, 0))],
        out_specs=pl.BlockSpec((tm, d), lambda i, j, te, tv: (i, 0)),
        scratch_shapes=[pltpu.VMEM((tm, d), BF16)],
    )
    return pl.pallas_call(
        _experts_kernel,
        out_shape=jax.ShapeDtypeStruct((n_rows, d), F32),
        grid_spec=grid_spec,
        compiler_params=_params("arbitrary", "arbitrary"),
        name="moe_experts",
    )(tile_expert, tile_valid, xs, w_gate, w_up, w_down)


def _combine_kernel(pos_ref, x_ref, gate_ref, ys_ref, o_ref, buf_ref, sem):
    tt = x_ref.shape[0]

    def row_copy(a):
        return pltpu.make_async_copy(ys_ref.at[pl.ds(pos_ref[0, 0, a], 1)],
                                     buf_ref.at[a % 2, pl.ds(a // 2, 1)], sem)

    def start(a, c):
        row_copy(a).start()
        return c

    def wait(a, c):
        row_copy(a).wait()
        return c

    lax.fori_loop(0, 2 * tt, start, 0, unroll=ROW_DMA_UNROLL)
    lax.fori_loop(0, 2 * tt, wait, 0, unroll=ROW_DMA_UNROLL)
    gates = gate_ref[...]
    o_ref[...] = x_ref[...] + gates[:, 0:1] * buf_ref[0] + gates[:, 1:2] * buf_ref[1]


def combine(x2, gates, pos, ys):
    t, d = x2.shape
    tt = MOE_TOKEN_TILE
    pos3 = pos.reshape(t // tt, 1, 2 * tt)
    row = pl.BlockSpec((tt, d), lambda i: (i, 0))
    return pl.pallas_call(
        _combine_kernel,
        out_shape=jax.ShapeDtypeStruct((t, d), F32),
        grid=(t // tt,),
        in_specs=[pl.BlockSpec((1, 1, 2 * tt), lambda i: (i, 0, 0), memory_space=pltpu.SMEM),
                  row, pl.BlockSpec((tt, 2), lambda i: (i, 0)),
                  pl.BlockSpec(memory_space=pl.ANY)],
        out_specs=row,
        scratch_shapes=[pltpu.VMEM((2, tt, d), F32), pltpu.SemaphoreType.DMA(())],
        compiler_params=_params("arbitrary", row_dma=True),
        name="moe_combine",
    )(pos3, x2, gates, ys)


def _pad_cols(w, mult):
    pad = (-w.shape[-1]) % mult
    return jnp.pad(w, ((0, 0),) * (w.ndim - 1) + ((0, pad),))


def kernel(x, rel_bias, conv_norm, conv_w_in, conv_b_in, conv_dw_w, conv_dw_b, conv_ln_g, conv_ln_b, conv_w_out, conv_b_out, kv_norm, w_kv, k_norm, attn_norm, w_q, q_norm, w_o, ffn_norm, ffn_w_gate, ffn_w_up, ffn_w_down, moe_router, moe_w_gate, moe_w_up, moe_w_down):
    b, s, d = x.shape
    t = b * s
    row = lambda v: v.reshape(1, -1)

    x2 = x.reshape(t, d)
    u = conv_in(x2, row(conv_norm[0]), conv_w_in[0].astype(BF16), row(conv_b_in[0]))
    x3 = conv_out(u.reshape(b, s, d), x, conv_dw_w[0], row(conv_dw_b[0]), row(conv_ln_g[0]),
                  row(conv_ln_b[0]), conv_w_out[0].astype(BF16), row(conv_b_out[0]))
    x2 = x3.reshape(t, d)
    wg = _pad_cols(ffn_w_gate[0], LANE).astype(BF16)
    wu = _pad_cols(ffn_w_up[0], LANE).astype(BF16)
    wd = jnp.pad(ffn_w_down[0], ((0, wg.shape[1] - ffn_w_down.shape[1]), (0, 0))).astype(BF16)
    x2 = ffn_dense(x2, row(ffn_norm[0]), wg, wu, wd)

    qn = row(jnp.tile(q_norm[0] * (HEAD_DIM ** -0.5), (1, N_HEADS)))
    kn = row(jnp.tile(k_norm, (1, N_HEADS)))
    q, k, v = qkv_proj(x2, row(attn_norm[0]), row(kv_norm), w_q[0].astype(BF16), w_kv.astype(BF16),
                       qn, kn)
    hw = N_HEADS * HEAD_DIM
    outs, lses = [], []
    for g, (_, dil) in enumerate(DILATED_GROUPS):
        sl = slice(g * hw, (g + 1) * hw)
        qr, kr, vr = (_to_residue(a[:, sl].reshape(b, s, hw), dil) for a in (q, k, v))
        o_g, lse_g = dilated_attention_group(qr, kr, vr, band_bias(rel_bias, g))
        outs.append(_from_residue(o_g, b, dil).reshape(t, hw))
        lses.append(_from_residue(lse_g, b, dil).reshape(t, LANE))
    x2 = attn_out(x2, outs, lses, w_o[0].astype(BF16))

    g_moe = row(ffn_norm[1])
    idx, gates = router(x2, g_moe, moe_router[0])
    pos, tile_expert, tile_valid, n_tiles = routing_plan(idx, MOE_ROW_TILE)
    xs = dispatch(x2, g_moe, pos, n_tiles * MOE_ROW_TILE)
    ys = experts(xs, tile_expert, tile_valid, moe_w_gate[0].astype(BF16), moe_w_up[0].astype(BF16),
                 moe_w_down[0].astype(BF16))
    return combine(x2, gates, pos, ys).reshape(b, s, d)
```

```python
import functools

import numpy as np
import jax
import jax.numpy as jnp
from jax import lax
from jax.experimental import pallas as pl
from jax.experimental.pallas import tpu as pltpu

F32 = jnp.float32
BF16 = jnp.bfloat16

EPS = 1e-6
NEG = -1e30
CONV_WIDTH = 31
CONV_HALO = 32
N_HEADS = 16
HEAD_DIM = 64
DILATED_GROUPS = ((128, 1), (512, 4), (2048, 16))
BLK = 128
N_BUCKETS = 32
MAX_DISTANCE = 2048
N_EXPERTS = 8
LANE = 128
SUBLANES = 8
VMEM_LIMIT_BYTES = 56 * 1024 * 1024

ROW_TILE = 512
CONV_CHUNK = 64
CONV_SPAN = CONV_CHUNK + SUBLANES * ((CONV_WIDTH - 1) // SUBLANES)
MOE_ROW_TILE = 1024
MOE_FF_TILE = 512
MOE_TOKEN_TILE = 512
MOE_SMALL_WINDOW = 256
META_FIELDS = 8


def _params(*sem):
    return pltpu.CompilerParams(dimension_semantics=sem, vmem_limit_bytes=VMEM_LIMIT_BYTES)


def _resident(shape):
    nd = len(shape)
    return pl.BlockSpec(shape, lambda *_: (0,) * nd, pipeline_mode=pl.Buffered(1))


def _rms(x, g):
    return x * lax.rsqrt(jnp.mean(x * x, axis=-1, keepdims=True) + EPS) * g


def _conv_in_kernel(x_ref, g_ref, w_ref, b_ref, u_ref):
    h = _rms(x_ref[...], g_ref[...])
    y = jnp.dot(h.astype(BF16), w_ref[...], preferred_element_type=F32) + b_ref[...]
    d = u_ref.shape[-1]
    u_ref[...] = (y[:, :d] * jax.nn.sigmoid(y[:, d:])).astype(u_ref.dtype)


def conv_in(x2, g, w_in, b_in):
    t, d = x2.shape
    return pl.pallas_call(
        _conv_in_kernel,
        out_shape=jax.ShapeDtypeStruct((t, d), BF16),
        grid=(t // ROW_TILE,),
        in_specs=[pl.BlockSpec((ROW_TILE, d), lambda i: (i, 0)),
                  _resident((1, d)), _resident((d, 2 * d)), _resident((1, 2 * d))],
        out_specs=pl.BlockSpec((ROW_TILE, d), lambda i: (i, 0)),
        compiler_params=_params("parallel"),
        name="conv_in",
    )(x2, g, w_in, b_in)


def _conv_out_kernel(u_ref, x_ref, dww_ref, dwb_ref, lng_ref, lnb_ref, w_ref, b_ref, o_ref,
                     buf_ref, conv_ref, shift_ref):
    ts, d = x_ref.shape
    j = pl.program_id(1)

    @pl.when(j == 0)
    def _():
        buf_ref[0:CONV_HALO, :] = jnp.zeros((CONV_HALO, d), F32)

    @pl.when(j > 0)
    def _():
        buf_ref[0:CONV_HALO, :] = buf_ref[ts:ts + CONV_HALO, :]

    buf_ref[CONV_HALO:CONV_HALO + ts, :] = u_ref[...].astype(F32)
    first_tap = CONV_HALO - (CONV_WIDTH - 1)

    def lane_chunk(c, carry):
        lanes = pl.ds(pl.multiple_of(c * LANE, LANE), LANE)
        taps = [dww_ref[k:k + 1, lanes] for k in range(CONV_WIDTH)]
        bias = dwb_ref[:, lanes]
        for r0 in range(0, ts, CONV_CHUNK):
            acc = jnp.broadcast_to(bias, (CONV_CHUNK, LANE))
            slot = (r0 // CONV_CHUNK) % 2
            for phase in range(SUBLANES):
                lo = r0 + first_tap + phase
                span = CONV_CHUNK + SUBLANES * (len(range(phase, CONV_WIDTH, SUBLANES)) - 1)
                shift_ref[slot, phase, 0:span, :] = buf_ref[lo:lo + span, lanes]
            for phase in range(SUBLANES):
                for a, k in enumerate(range(phase, CONV_WIDTH, SUBLANES)):
                    acc = acc + taps[k] * shift_ref[slot, phase, SUBLANES * a:SUBLANES * a + CONV_CHUNK, :]
            conv_ref[r0:r0 + CONV_CHUNK, lanes] = acc
        return carry

    lax.fori_loop(0, d // LANE, lane_chunk, 0)
    acc = conv_ref[...]
    mu = jnp.mean(acc, axis=-1, keepdims=True)
    cen = acc - mu
    var = jnp.mean(cen * cen, axis=-1, keepdims=True)
    y = cen * lax.rsqrt(var + EPS) * lng_ref[...] + lnb_ref[...]
    act = (y * jax.nn.sigmoid(y)).astype(BF16)
    o_ref[...] = x_ref[...] + b_ref[...] + jnp.dot(act, w_ref[...], preferred_element_type=F32)


def conv_out(u3, x3, dw_w, dw_b, ln_g, ln_b, w_out, b_out):
    b, s, d = x3.shape
    ts = ROW_TILE
    row = pl.BlockSpec((None, ts, d), lambda bi, j: (bi, j, 0))
    return pl.pallas_call(
        _conv_out_kernel,
        out_shape=jax.ShapeDtypeStruct((b, s, d), F32),
        grid=(b, s // ts),
        in_specs=[row, row, _resident((CONV_WIDTH, d)), _resident((1, d)), _resident((1, d)),
                  _resident((1, d)), _resident((d, d)), _resident((1, d))],
        out_specs=row,
        scratch_shapes=[pltpu.VMEM((ts + CONV_HALO, d), F32), pltpu.VMEM((ts, d), F32),
                        pltpu.VMEM((2, SUBLANES, CONV_SPAN, LANE), F32)],
        compiler_params=_params("parallel", "arbitrary"),
        name="conv_out",
    )(u3, x3, dw_w, dw_b, ln_g, ln_b, w_out, b_out)


def _ffn_kernel(x_ref, g_ref, wg_ref, wu_ref, wd_ref, o_ref):
    x = x_ref[...]
    h = _rms(x, g_ref[...]).astype(BF16)
    a = jnp.dot(h, wg_ref[...], preferred_element_type=F32)
    u = jnp.dot(h, wu_ref[...], preferred_element_type=F32)
    act = (a * jax.nn.sigmoid(a) * u).astype(BF16)
    o_ref[...] = x + jnp.dot(act, wd_ref[...], preferred_element_type=F32)


def ffn_dense(x2, g, w_gate, w_up, w_down):
    t, d = x2.shape
    f = w_gate.shape[1]
    row = pl.BlockSpec((ROW_TILE, d), lambda i: (i, 0))
    return pl.pallas_call(
        _ffn_kernel,
        out_shape=jax.ShapeDtypeStruct((t, d), F32),
        grid=(t // ROW_TILE,),
        in_specs=[row, _resident((1, d)), _resident((d, f)), _resident((d, f)), _resident((f, d))],
        out_specs=row,
        compiler_params=_params("parallel"),
        name="ffn_dense",
    )(x2, g, w_gate, w_up, w_down)


def _store_heads(y, out_ref, gain_ref=None, gain_off=0):
    dil, n, _ = out_ref.shape
    low = lax.broadcasted_iota(jnp.int32, (1, LANE), 1) < HEAD_DIM
    for c in range(y.shape[-1] // LANE):
        cols = slice(c * LANE, (c + 1) * LANE)
        blk = y[:, cols]
        if gain_ref is not None:
            sq = blk * blk
            ms_lo = jnp.sum(jnp.where(low, sq, 0.0), axis=-1, keepdims=True) * (1.0 / HEAD_DIM)
            ms_hi = jnp.sum(jnp.where(low, 0.0, sq), axis=-1, keepdims=True) * (1.0 / HEAD_DIM)
            inv = jnp.where(low, lax.rsqrt(ms_lo + EPS), lax.rsqrt(ms_hi + EPS))
            blk = blk * inv * gain_ref[:, gain_off + c * LANE:gain_off + (c + 1) * LANE]
        blk = blk.astype(out_ref.dtype)
        for r in range(dil):
            out_ref[r, :, cols] = blk[r * n:(r + 1) * n]


def _qkv_kernel(x_ref, gq_ref, gkv_ref, wq_ref, wkv_ref, qn_ref, kn_ref, *refs):
    out_refs, xn_ref = refs[:-1], refs[-1]
    x = x_ref[...]
    xn = x * lax.rsqrt(jnp.mean(x * x, axis=-1, keepdims=True) + EPS)
    n_cols = xn_ref.shape[0]
    for c in range(n_cols):
        xn_ref[c] = xn[:, c * LANE:(c + 1) * LANE]
    tm = x.shape[0]
    hw = N_HEADS * HEAD_DIM
    qw = len(DILATED_GROUPS) * hw
    for g, (_, dil) in enumerate(DILATED_GROUPS):
        n = tm // dil
        xg = xn if dil == 1 else jnp.concatenate(
            [jnp.concatenate([xn_ref[c, pl.ds(r, n, stride=dil), :] for r in range(dil)], axis=0)
             for c in range(n_cols)], axis=1)
        hq = (xg * gq_ref[...]).astype(BF16)
        hkv = (xg * gkv_ref[...]).astype(BF16)
        q_ref, k_ref, v_ref = out_refs[3 * g:3 * g + 3]
        cols = slice(g * hw, (g + 1) * hw)
        _store_heads(jnp.dot(hq, wq_ref[:, cols], preferred_element_type=F32), q_ref, qn_ref, g * hw)
        _store_heads(jnp.dot(hkv, wkv_ref[:, cols], preferred_element_type=F32), k_ref, kn_ref, g * hw)
        _store_heads(jnp.dot(hkv, wkv_ref[:, qw + g * hw:qw + (g + 1) * hw],
                             preferred_element_type=F32), v_ref)


def qkv_proj(x3, gq, gkv, w_q, w_kv, qn, kn):
    b, s, d = x3.shape
    qw = w_q.shape[1]
    hw = N_HEADS * HEAD_DIM
    tm = ROW_TILE
    out_shape, out_specs = [], []
    for _, dil in DILATED_GROUPS:
        for _ in range(3):
            out_shape.append(jax.ShapeDtypeStruct((b, dil, s // dil, hw), BF16))
            out_specs.append(pl.BlockSpec((None, dil, tm // dil, hw), lambda bi, j: (bi, 0, j, 0)))
    return pl.pallas_call(
        _qkv_kernel,
        out_shape=tuple(out_shape),
        grid=(b, s // tm),
        in_specs=[pl.BlockSpec((None, tm, d), lambda bi, j: (bi, j, 0)), _resident((1, d)),
                  _resident((1, d)), _resident((d, qw)), _resident((d, 2 * qw)), _resident((1, qw)),
                  _resident((1, qw))],
        out_specs=tuple(out_specs),
        scratch_shapes=[pltpu.VMEM((d // LANE, tm, LANE), F32)],
        compiler_params=_params("parallel", "parallel"),
        name="qkv_proj",
    )(x3, gq, gkv, w_q, w_kv, qn, kn)


def _t5_bucket(dist):
    n = np.asarray(dist)
    max_exact = N_BUCKETS // 2
    large = max_exact + (np.log(np.maximum(n, 1) / max_exact)
                         / np.log(MAX_DISTANCE / max_exact)
                         * (N_BUCKETS - max_exact)).astype(np.int32)
    large = np.minimum(large, N_BUCKETS - 1)
    return np.where(n < max_exact, n, large).astype(np.int32)


def band_bias(rel_bias, g):
    window, d = DILATED_GROUPS[g]
    span = window // d
    sub = BLK + np.arange(BLK)[:, None] - np.arange(2 * BLK)[None, :]
    in_band = (sub >= 0) & (sub <= span)
    bucket = _t5_bucket(np.clip(sub, 0, span) * d)
    table = rel_bias[:, g * N_HEADS:(g + 1) * N_HEADS]
    bias = jnp.transpose(table[bucket], (2, 0, 1)).astype(F32)
    general = jnp.where(jnp.asarray(in_band), bias, NEG)
    has_prev = np.arange(2 * BLK)[None, :] >= BLK
    first = jnp.where(jnp.asarray(in_band & has_prev), bias, NEG)
    return jnp.stack([first, general])


def _attn_kernel(q_ref, k_ref, v_ref, bias_ref, hm_ref, o_ref, lse_ref,
                 kprev_ref, vprev_ref, s_ref, p_ref, m_ref):
    i = pl.program_id(1)

    @pl.when(i == 0)
    def _():
        kprev_ref[...] = jnp.zeros(kprev_ref.shape, kprev_ref.dtype)
        vprev_ref[...] = jnp.zeros(vprev_ref.shape, vprev_ref.dtype)

    variant = jnp.minimum(i, 1)
    n_cols = q_ref.shape[-1] // LANE

    for c in range(n_cols):
        cols = slice(c * LANE, (c + 1) * LANE)
        q2 = q_ref[:, cols]
        kcat = jnp.concatenate([kprev_ref[:, cols], k_ref[:, cols]], axis=0)
        for half in range(2):
            h = 2 * c + half
            s = lax.dot_general(q2 * hm_ref[half:half + 1, :], kcat, (((1,), (1,)), ((), ())),
                                preferred_element_type=F32) + bias_ref[variant, h]
            s_ref[h] = s
            m_ref[h] = jnp.broadcast_to(jnp.max(s, axis=-1, keepdims=True), (BLK, LANE))

    for h in range(N_HEADS):
        m = m_ref[h]
        p_ref[h] = jnp.exp(s_ref[h] - jnp.concatenate([m, m], axis=-1)).astype(p_ref.dtype)

    lane = lax.broadcasted_iota(jnp.int32, (BLK, LANE), 1)
    low = lane < HEAD_DIM
    ones = jnp.ones((2 * BLK, LANE), BF16)
    m_tile = jnp.zeros((BLK, LANE), F32)
    l_tile = jnp.ones((BLK, LANE), F32)
    for c in range(n_cols):
        cols = slice(c * LANE, (c + 1) * LANE)
        vcat = jnp.concatenate([vprev_ref[:, cols], v_ref[:, cols]], axis=0)
        acc = []
        for half in range(2):
            rhs = jnp.concatenate([vcat * hm_ref[half:half + 1, :], ones], axis=-1)
            acc.append(jnp.dot(p_ref[2 * c + half], rhs, preferred_element_type=F32))
        den = jnp.where(low, acc[0][:, LANE:], acc[1][:, LANE:])
        o_ref[:, cols] = ((acc[0][:, :LANE] + acc[1][:, :LANE]) / den).astype(o_ref.dtype)
        for half in range(2):
            h = 2 * c + half
            m_tile = jnp.where(lane == h, m_ref[h], m_tile)
            l_tile = jnp.where(lane == h, acc[half][:, LANE:], l_tile)
    lse_ref[...] = m_tile + jnp.log(l_tile)
    kprev_ref[...] = k_ref[...]
    vprev_ref[...] = v_ref[...]


def dilated_attention_group(q, k, v, bias):
    r, l, w = q.shape
    blk = pl.BlockSpec((None, BLK, w), lambda a, i: (a, i, 0))
    lane = np.arange(LANE)[None, :] < HEAD_DIM
    head_mask = jnp.asarray(np.concatenate([lane, ~lane]).astype(np.float32), BF16)
    return pl.pallas_call(
        _attn_kernel,
        out_shape=(jax.ShapeDtypeStruct((r, l, w), BF16),
                   jax.ShapeDtypeStruct((r, l, LANE), F32)),
        grid=(r, l // BLK),
        in_specs=[blk, blk, blk, _resident(bias.shape), _resident((2, LANE))],
        out_specs=(blk, pl.BlockSpec((None, BLK, LANE), lambda a, i: (a, i, 0))),
        scratch_shapes=[pltpu.VMEM((BLK, w), BF16), pltpu.VMEM((BLK, w), BF16),
                        pltpu.VMEM((N_HEADS, BLK, 2 * BLK), F32),
                        pltpu.VMEM((N_HEADS, BLK, 2 * BLK), BF16),
                        pltpu.VMEM((N_HEADS, BLK, LANE), F32)],
        compiler_params=_params("parallel", "arbitrary"),
        name="dilated_attn",
    )(q, k, v, bias, head_mask)


def _attn_out_kernel(x_ref, o0_ref, o1_ref, o2_ref, l0_ref, l1_ref, l2_ref, e_ref, w_ref, o_ref,
                     on_ref, ln_ref):
    n_cols = on_ref.shape[1]
    for gi, (og_ref, lg_ref) in enumerate(((o1_ref, l1_ref), (o2_ref, l2_ref))):
        dil, n, _ = og_ref.shape
        for r in range(dil):
            ln_ref[gi, pl.ds(r, n, stride=dil), :] = lg_ref[r]
            for c in range(n_cols):
                on_ref[gi, c, pl.ds(r, n, stride=dil), :] = (
                    og_ref[r, :, c * LANE:(c + 1) * LANE].astype(F32))
    l0, l1, l2 = l0_ref[0], ln_ref[0], ln_ref[1]
    m = jnp.maximum(jnp.maximum(l0, l1), l2)
    e0, e1, e2 = jnp.exp(l0 - m), jnp.exp(l1 - m), jnp.exp(l2 - m)
    den = e0 + e1 + e2
    acc = None
    token_order = [jnp.concatenate([on_ref[gi, c] for c in range(n_cols)], axis=1) for gi in range(2)]
    for e, o in ((e0, o0_ref[0].astype(F32)), (e1, token_order[0]), (e2, token_order[1])):
        wgt = jnp.dot((e / den).astype(BF16), e_ref[...], preferred_element_type=F32)
        term = wgt * o
        acc = term if acc is None else acc + term
    o_ref[...] = x_ref[...] + jnp.dot(acc.astype(BF16), w_ref[...], preferred_element_type=F32)


def attn_out(x3, outs, lses, w_o):
    b, s, d = x3.shape
    tm = ROW_TILE
    expand = np.zeros((LANE, d), np.float32)
    expand[:N_HEADS] = np.repeat(np.eye(N_HEADS, dtype=np.float32), HEAD_DIM, axis=1)
    expand = jnp.asarray(expand, BF16)
    row = pl.BlockSpec((None, tm, d), lambda bi, j: (bi, j, 0))
    o_specs = [pl.BlockSpec((None, dil, tm // dil, d), lambda bi, j: (bi, 0, j, 0))
               for _, dil in DILATED_GROUPS]
    l_specs = [pl.BlockSpec((None, dil, tm // dil, LANE), lambda bi, j: (bi, 0, j, 0))
               for _, dil in DILATED_GROUPS]
    return pl.pallas_call(
        _attn_out_kernel,
        out_shape=jax.ShapeDtypeStruct((b, s, d), F32),
        grid=(b, s // tm),
        in_specs=[row, *o_specs, *l_specs, _resident((LANE, d)), _resident((d, d))],
        out_specs=row,
        scratch_shapes=[pltpu.VMEM((2, d // LANE, tm, LANE), F32), pltpu.VMEM((2, tm, LANE), F32)],
        compiler_params=_params("parallel", "parallel"),
        name="attn_out",
    )(x3, *outs, *lses, expand, w_o)


def _router_kernel(x_ref, g_ref, w_ref, tri_ref, col_ref, row_ref, cnt_ref):
    h = _rms(x_ref[...], g_ref[...])
    w = w_ref[...]
    h_hi = h.astype(BF16)
    h_lo = (h - h_hi.astype(F32)).astype(BF16)
    w_hi = w.astype(BF16)
    w_lo = (w - w_hi.astype(F32)).astype(BF16)
    logits = (jnp.dot(h_hi, w_hi, preferred_element_type=F32)
              + jnp.dot(h_lo, w_hi, preferred_element_type=F32)
              + jnp.dot(h_hi, w_lo, preferred_element_type=F32))
    ne = N_EXPERTS
    lane = lax.broadcasted_iota(jnp.int32, logits.shape, 1)
    logits = jnp.where(lane < ne, logits, -jnp.inf)
    m1 = jnp.max(logits, axis=-1, keepdims=True)
    i1 = jnp.min(jnp.where(logits == m1, lane, ne), axis=-1, keepdims=True)
    rest = jnp.where(lane == i1, -jnp.inf, logits)
    m2 = jnp.max(rest, axis=-1, keepdims=True)
    i2 = jnp.min(jnp.where(rest == m2, lane, ne), axis=-1, keepdims=True)
    e2 = jnp.exp(m2 - m1)
    g1 = 1.0 / (1.0 + e2)
    g2 = e2 / (1.0 + e2)
    both = (lane == i1) | (lane == i2)
    cnt = jnp.dot(tri_ref[...], jnp.where(both, 1.0, 0.0).astype(BF16), preferred_element_type=F32)
    r1 = jnp.sum(jnp.where(lane == i1, cnt, 0.0), axis=-1, keepdims=True)
    r2 = jnp.sum(jnp.where(lane == i2, cnt, 0.0), axis=-1, keepdims=True)
    cnt_ref[...] = jnp.sum(jnp.where(both, 1, 0), axis=0, keepdims=True)
    fields = (i1.astype(F32), i2.astype(F32), r1, r2, g1, g2)
    meta = jnp.zeros(logits.shape, F32)
    for f, val in enumerate(fields):
        meta = jnp.where(lane == f, val, meta)
    col_ref[...] = meta[:, :META_FIELDS]
    row_ref[...] = meta.T[:META_FIELDS, :]


def router(x2, g, w_router):
    t, d = x2.shape
    tt = MOE_TOKEN_TILE
    nt = t // tt
    tri = jnp.asarray(np.tril(np.ones((tt, tt), np.float32), -1), BF16)
    return pl.pallas_call(
        _router_kernel,
        out_shape=(jax.ShapeDtypeStruct((t, META_FIELDS), F32),
                   jax.ShapeDtypeStruct((nt, META_FIELDS, tt), F32),
                   jax.ShapeDtypeStruct((nt, 1, LANE), jnp.int32)),
        grid=(nt,),
        in_specs=[pl.BlockSpec((tt, d), lambda i: (i, 0)), _resident((1, d)),
                  _resident((d, LANE)), _resident((tt, tt))],
        out_specs=(pl.BlockSpec((tt, META_FIELDS), lambda i: (i, 0)),
                   pl.BlockSpec((None, META_FIELDS, tt), lambda i: (i, 0, 0)),
                   pl.BlockSpec((None, 1, LANE), lambda i: (i, 0, 0))),
        compiler_params=_params("parallel"),
        name="moe_router",
    )(x2, g, _pad_cols(w_router, LANE), tri)


def moe_plan(counts, n_tokens):
    tm = MOE_ROW_TILE
    counts = (counts + SUBLANES - 1) // SUBLANES * SUBLANES
    experts_ = jnp.arange(N_EXPERTS, dtype=jnp.int32)
    total = jnp.sum(counts, axis=0)
    owned = (total + MOE_TOKEN_TILE + tm - 1) // tm
    owned_end = jnp.cumsum(owned)
    region = (owned_end - owned) * tm
    base = region[None, :] + jnp.cumsum(counts, axis=0) - counts
    local = jnp.cumsum(counts, axis=1) - counts
    max_rows = 2 * n_tokens + (SUBLANES - 1) * N_EXPERTS * counts.shape[0] + N_EXPERTS * MOE_TOKEN_TILE
    n_tiles = -(-max_rows // tm) + N_EXPERTS + 1
    tile_id = jnp.arange(n_tiles, dtype=jnp.int32)
    tile_expert = jnp.minimum(jnp.sum(tile_id[:, None] >= owned_end[None, :], axis=1), N_EXPERTS - 1)
    pick = tile_expert[:, None] == experts_[None, :]
    rows_before = (tile_id - jnp.sum(jnp.where(pick, (owned_end - owned)[None, :], 0), axis=1)) * tm
    has_rows = (tile_id < owned_end[-1]) & (rows_before < jnp.sum(jnp.where(pick, total[None, :], 0), axis=1))
    return dict(n_tiles=n_tiles, counts=counts.reshape(-1), base=base.reshape(-1).astype(jnp.int32),
                local=local.reshape(-1).astype(jnp.int32), tail=(region + total).astype(jnp.int32),
                tile_expert=tile_expert.astype(jnp.int32), tile_valid=has_rows.astype(jnp.int32))


def _run_windows(n):
    return (((n > 0) & (n <= MOE_SMALL_WINDOW), MOE_SMALL_WINDOW), (n > MOE_SMALL_WINDOW, MOE_TOKEN_TILE))


def _dispatch_kernel(cnt_ref, local_ref, base_ref, tail_ref, valid_ref, x_ref, g_ref, meta_ref, xs_ref,
                     stage_ref, sem):
    i = pl.program_id(0)
    tt, d = x_ref.shape
    n_sorted = 2 * tt + SUBLANES * N_EXPERTS
    n_zero = stage_ref.shape[0] - n_sorted
    tm = MOE_ROW_TILE

    @pl.when(i == 0)
    def _():
        stage_ref[n_sorted:, :] = jnp.zeros((n_zero, stage_ref.shape[1]), F32)

        def zero_copy(row, rows):
            return pltpu.make_async_copy(stage_ref.at[pl.ds(n_sorted, rows)],
                                         xs_ref.at[pl.ds(pl.multiple_of(row, SUBLANES), rows)], sem)

        pieces = [n_zero >> b for b in range((n_zero // SUBLANES).bit_length())]
        for action in ("start", "wait"):
            def empty_tile(k, carry, action=action):
                @pl.when(valid_ref[k] == 0)
                def _():
                    for part in range(tm // n_zero):
                        getattr(zero_copy(k * tm + part * n_zero, n_zero), action)()
                return carry

            lax.fori_loop(0, valid_ref.shape[0], empty_tile, 0)
            for e in range(N_EXPERTS):
                row = tail_ref[e]
                left = (-row) & (tm - 1)
                for rows in pieces:
                    @pl.when((left & rows) != 0)
                    def _(row=row, rows=rows):
                        getattr(zero_copy(row, rows), action)()
                    row = row + (left & rows)

    h = _rms(x_ref[...], g_ref[...]).astype(BF16)
    meta = meta_ref[...]
    e1, e2, r1, r2, g1, g2 = (meta[f:f + 1, :] for f in range(6))
    off1 = jnp.zeros_like(r1)
    off2 = jnp.zeros_like(r2)
    for e in range(N_EXPERTS):
        lo = local_ref[i * N_EXPERTS + e].astype(F32)
        off1 = jnp.where(e1 == e, lo, off1)
        off2 = jnp.where(e2 == e, lo, off2)
    slot = lax.broadcasted_iota(jnp.int32, (n_sorted, tt), 0).astype(F32)
    hit1 = slot == r1 + off1
    hit2 = slot == r2 + off2
    perm = jnp.where(hit1 | hit2, 1.0, 0.0).astype(BF16)
    stage_ref[0:n_sorted, 0:d] = jnp.dot(perm, h, preferred_element_type=F32)
    gate = jnp.sum(jnp.where(hit1, g1, 0.0) + jnp.where(hit2, g2, 0.0), axis=-1, keepdims=True)
    stage_ref[0:n_sorted, d:] = jnp.broadcast_to(gate, (n_sorted, LANE))

    def run_copy(e, rows):
        k = i * N_EXPERTS + e
        return pltpu.make_async_copy(stage_ref.at[pl.ds(pl.multiple_of(local_ref[k], SUBLANES), rows)],
                                     xs_ref.at[pl.ds(pl.multiple_of(base_ref[k], SUBLANES), rows)], sem)

    for action in ("start", "wait"):
        for e in range(N_EXPERTS):
            for cond, rows in _run_windows(cnt_ref[i * N_EXPERTS + e]):
                @pl.when(cond)
                def _(e=e, rows=rows):
                    getattr(run_copy(e, rows), action)()


def dispatch(x2, g, meta_row, plan):
    t, d = x2.shape
    tt = MOE_TOKEN_TILE
    grid_spec = pltpu.PrefetchScalarGridSpec(
        num_scalar_prefetch=5,
        grid=(t // tt,),
        in_specs=[pl.BlockSpec((tt, d), lambda i, *_: (i, 0)),
                  pl.BlockSpec((1, d), lambda i, *_: (0, 0)),
                  pl.BlockSpec((None, META_FIELDS, tt), lambda i, *_: (i, 0, 0))],
        out_specs=pl.BlockSpec(memory_space=pl.ANY),
        scratch_shapes=[pltpu.VMEM((2 * tt + SUBLANES * N_EXPERTS + MOE_TOKEN_TILE, d + LANE), F32),
                        pltpu.SemaphoreType.DMA(())],
    )
    return pl.pallas_call(
        _dispatch_kernel,
        out_shape=jax.ShapeDtypeStruct((plan["n_tiles"] * MOE_ROW_TILE, d + LANE), F32),
        grid_spec=grid_spec,
        compiler_params=_params("arbitrary"),
        name="moe_dispatch",
    )(plan["counts"], plan["local"], plan["base"], plan["tail"], plan["tile_valid"], x2, g, meta_row)


def _experts_kernel(te_ref, tv_ref, x_ref, wg_ref, wu_ref, wd_ref, o_ref, xb_ref):
    i, j = pl.program_id(0), pl.program_id(1)
    d = o_ref.shape[-1]

    @pl.when(j == 0)
    def _():
        xb_ref[...] = x_ref[:, :d].astype(BF16)
        o_ref[...] = jnp.zeros(o_ref.shape, o_ref.dtype)

    @pl.when(tv_ref[i] > 0)
    def _():
        xb = xb_ref[...]
        a = jnp.dot(xb, wg_ref[...], preferred_element_type=F32)
        u = jnp.dot(xb, wu_ref[...], preferred_element_type=F32)
        act = (a * jax.nn.sigmoid(a) * u).astype(BF16)
        o_ref[...] += jnp.dot(act, wd_ref[...], preferred_element_type=F32)

        @pl.when(j == pl.num_programs(1) - 1)
        def _():
            o_ref[...] = o_ref[...] * x_ref[:, d:d + 1]


def experts(xs, plan, w_gate, w_up, w_down):
    n_rows, dw = xs.shape
    d = dw - LANE
    f = w_gate.shape[-1]
    tm, tf = MOE_ROW_TILE, MOE_FF_TILE
    grid_spec = pltpu.PrefetchScalarGridSpec(
        num_scalar_prefetch=2,
        grid=(n_rows // tm, f // tf),
        in_specs=[pl.BlockSpec((tm, dw), lambda i, j, te, tv: (i, 0)),
                  pl.BlockSpec((None, d, tf), lambda i, j, te, tv: (te[i], 0, j)),
                  pl.BlockSpec((None, d, tf), lambda i, j, te, tv: (te[i], 0, j)),
                  pl.BlockSpec((None, tf, d), lambda i, j, te, tv: (te[i], j, 0))],
        out_specs=pl.BlockSpec((tm, d), lambda i, j, te, tv: (i, 0)),
        scratch_shapes=[pltpu.VMEM((tm, d), BF16)],
    )
    return pl.pallas_call(
        _experts_kernel,
        out_shape=jax.ShapeDtypeStruct((n_rows, d), F32),
        grid_spec=grid_spec,
        compiler_params=_params("arbitrary", "arbitrary"),
        name="moe_experts",
    )(plan["tile_expert"], plan["tile_valid"], xs, w_gate, w_up, w_down)


def _combine_kernel(cnt_ref, base_ref, x_ref, meta_ref, ys_ref, o_ref, win_ref, sem):
    i = pl.program_id(0)
    tt = x_ref.shape[0]
    meta = meta_ref[...]
    e1, e2, r1, r2 = (meta[:, f:f + 1] for f in range(4))
    counts = [cnt_ref[i * N_EXPERTS + e] for e in range(N_EXPERTS)]
    widest = counts[0]
    for n in counts[1:]:
        widest = jnp.maximum(widest, n)

    def gather(rows):
        copies = [pltpu.make_async_copy(
            ys_ref.at[pl.ds(pl.multiple_of(base_ref[i * N_EXPERTS + e], SUBLANES), rows)],
            win_ref.at[e, pl.ds(0, rows)], sem) for e in range(N_EXPERTS)]
        for c in copies:
            c.start()
        for c in copies:
            c.wait()
        pos = lax.broadcasted_iota(jnp.int32, (tt, rows), 1).astype(F32)
        acc = x_ref[...]
        for e in range(N_EXPERTS):
            want1 = jnp.where(e1 == e, r1, -1.0)
            want2 = jnp.where(e2 == e, r2, -1.0)
            pick = jnp.where((pos == want1) | (pos == want2), 1.0, 0.0).astype(BF16)
            acc = acc + jnp.dot(pick, win_ref[e, 0:rows, :].astype(BF16), preferred_element_type=F32)
        o_ref[...] = acc

    @pl.when(widest <= MOE_SMALL_WINDOW)
    def _():
        gather(MOE_SMALL_WINDOW)

    @pl.when(widest > MOE_SMALL_WINDOW)
    def _():
        gather(tt)


def combine(x2, meta_col, ys, plan):
    t, d = x2.shape
    tt = MOE_TOKEN_TILE
    grid_spec = pltpu.PrefetchScalarGridSpec(
        num_scalar_prefetch=2,
        grid=(t // tt,),
        in_specs=[pl.BlockSpec((tt, d), lambda i, *_: (i, 0)),
                  pl.BlockSpec((tt, META_FIELDS), lambda i, *_: (i, 0)),
                  pl.BlockSpec(memory_space=pl.ANY)],
        out_specs=pl.BlockSpec((tt, d), lambda i, *_: (i, 0)),
        scratch_shapes=[pltpu.VMEM((N_EXPERTS, tt, d), F32), pltpu.SemaphoreType.DMA(())],
    )
    return pl.pallas_call(
        _combine_kernel,
        out_shape=jax.ShapeDtypeStruct((t, d), F32),
        grid_spec=grid_spec,
        compiler_params=_params("arbitrary"),
        name="moe_combine",
    )(plan["counts"], plan["base"], x2, meta_col, ys)


def _pad_cols(w, mult):
    pad = (-w.shape[-1]) % mult
    return jnp.pad(w, ((0, 0),) * (w.ndim - 1) + ((0, pad),))


def kernel(x, rel_bias, conv_norm, conv_w_in, conv_b_in, conv_dw_w, conv_dw_b, conv_ln_g, conv_ln_b, conv_w_out, conv_b_out, kv_norm, w_kv, k_norm, attn_norm, w_q, q_norm, w_o, ffn_norm, ffn_w_gate, ffn_w_up, ffn_w_down, moe_router, moe_w_gate, moe_w_up, moe_w_down):
    b, s, d = x.shape
    t = b * s
    row = lambda v: v.reshape(1, -1)

    x2 = x.reshape(t, d)
    u = conv_in(x2, row(conv_norm[0]), conv_w_in[0].astype(BF16), row(conv_b_in[0]))
    x3 = conv_out(u.reshape(b, s, d), x, conv_dw_w[0], row(conv_dw_b[0]), row(conv_ln_g[0]),
                  row(conv_ln_b[0]), conv_w_out[0].astype(BF16), row(conv_b_out[0]))
    x2 = x3.reshape(t, d)
    wg = _pad_cols(ffn_w_gate[0], LANE).astype(BF16)
    wu = _pad_cols(ffn_w_up[0], LANE).astype(BF16)
    wd = jnp.pad(ffn_w_down[0], ((0, wg.shape[1] - ffn_w_down.shape[1]), (0, 0))).astype(BF16)
    x2 = ffn_dense(x2, row(ffn_norm[0]), wg, wu, wd)

    qn = row(jnp.tile(q_norm[0] * (HEAD_DIM ** -0.5), (1, N_HEADS)))
    kn = row(jnp.tile(k_norm, (1, N_HEADS)))
    x3 = x2.reshape(b, s, d)
    qkv = qkv_proj(x3, row(attn_norm[0]), row(kv_norm), w_q[0].astype(BF16), w_kv.astype(BF16), qn, kn)
    outs, lses = [], []
    for g, (_, dil) in enumerate(DILATED_GROUPS):
        qr, kr, vr = (a.reshape(b * dil, s // dil, a.shape[-1]) for a in qkv[3 * g:3 * g + 3])
        o_g, lse_g = dilated_attention_group(qr, kr, vr, band_bias(rel_bias, g))
        outs.append(o_g.reshape(b, dil, s // dil, o_g.shape[-1]))
        lses.append(lse_g.reshape(b, dil, s // dil, LANE))
    x2 = attn_out(x3, outs, lses, w_o[0].astype(BF16)).reshape(t, d)

    g_moe = row(ffn_norm[1])
    meta_col, meta_row, counts = router(x2, g_moe, moe_router[0])
    plan = moe_plan(counts[:, 0, :N_EXPERTS], t)
    xs = dispatch(x2, g_moe, meta_row, plan)
    ys = experts(xs, plan, moe_w_gate[0].astype(BF16), moe_w_up[0].astype(BF16),
                 moe_w_down[0].astype(BF16))
    return combine(x2, meta_col, ys, plan).reshape(b, s, d)
```

```python
import functools

import numpy as np
import jax
import jax.numpy as jnp
from jax import lax
from jax.experimental import pallas as pl
from jax.experimental.pallas import tpu as pltpu

F32 = jnp.float32
BF16 = jnp.bfloat16

EPS = 1e-6
NEG = -1e30
CONV_WIDTH = 31
CONV_HALO = 32
N_HEADS = 16
HEAD_DIM = 64
DILATED_GROUPS = ((128, 1), (512, 4), (2048, 16))
BLK = 128
N_BUCKETS = 32
MAX_DISTANCE = 2048
N_EXPERTS = 8
LANE = 128
SUBLANES = 8
VMEM_LIMIT_BYTES = 56 * 1024 * 1024

ROW_TILE = 512
CONV_CHUNK = 64
CONV_SPAN = CONV_CHUNK + SUBLANES * ((CONV_WIDTH - 1) // SUBLANES)
MOE_ROW_TILE = 1024
MOE_FF_TILE = 896
MOE_TOKEN_TILE = 512
MOE_SMALL_WINDOW = 256
META_FIELDS = 8


def _params(*sem):
    return pltpu.CompilerParams(dimension_semantics=sem, vmem_limit_bytes=VMEM_LIMIT_BYTES)


def _resident(shape):
    nd = len(shape)
    return pl.BlockSpec(shape, lambda *_: (0,) * nd, pipeline_mode=pl.Buffered(1))


def _rms(x, g):
    return x * lax.rsqrt(jnp.mean(x * x, axis=-1, keepdims=True) + EPS) * g


def _conv_in_kernel(x_ref, g_ref, w_ref, b_ref, u_ref):
    h = _rms(x_ref[...], g_ref[...])
    y = jnp.dot(h.astype(BF16), w_ref[...], preferred_element_type=F32) + b_ref[...]
    d = u_ref.shape[-1]
    u_ref[...] = (y[:, :d] * jax.nn.sigmoid(y[:, d:])).astype(u_ref.dtype)


def conv_in(x2, g, w_in, b_in):
    t, d = x2.shape
    return pl.pallas_call(
        _conv_in_kernel,
        out_shape=jax.ShapeDtypeStruct((t, d), BF16),
        grid=(t // ROW_TILE,),
        in_specs=[pl.BlockSpec((ROW_TILE, d), lambda i: (i, 0)),
                  _resident((1, d)), _resident((d, 2 * d)), _resident((1, 2 * d))],
        out_specs=pl.BlockSpec((ROW_TILE, d), lambda i: (i, 0)),
        compiler_params=_params("parallel"),
        name="conv_in",
    )(x2, g, w_in, b_in)


def _conv_out_kernel(u_ref, x_ref, dww_ref, dwb_ref, lng_ref, lnb_ref, w_ref, b_ref, o_ref,
                     buf_ref, conv_ref, shift_ref):
    ts, d = x_ref.shape
    j = pl.program_id(1)

    @pl.when(j == 0)
    def _():
        buf_ref[0:CONV_HALO, :] = jnp.zeros((CONV_HALO, d), F32)

    @pl.when(j > 0)
    def _():
        buf_ref[0:CONV_HALO, :] = buf_ref[ts:ts + CONV_HALO, :]

    buf_ref[CONV_HALO:CONV_HALO + ts, :] = u_ref[...].astype(F32)
    first_tap = CONV_HALO - (CONV_WIDTH - 1)

    def lane_chunk(c, carry):
        lanes = pl.ds(pl.multiple_of(c * LANE, LANE), LANE)
        taps = [dww_ref[k:k + 1, lanes] for k in range(CONV_WIDTH)]
        bias = dwb_ref[:, lanes]
        for r0 in range(0, ts, CONV_CHUNK):
            acc = jnp.broadcast_to(bias, (CONV_CHUNK, LANE))
            slot = (r0 // CONV_CHUNK) % 2
            for phase in range(SUBLANES):
                lo = r0 + first_tap + phase
                span = CONV_CHUNK + SUBLANES * (len(range(phase, CONV_WIDTH, SUBLANES)) - 1)
                shift_ref[slot, phase, 0:span, :] = buf_ref[lo:lo + span, lanes]
            for phase in range(SUBLANES):
                for a, k in enumerate(range(phase, CONV_WIDTH, SUBLANES)):
                    acc = acc + taps[k] * shift_ref[slot, phase, SUBLANES * a:SUBLANES * a + CONV_CHUNK, :]
            conv_ref[r0:r0 + CONV_CHUNK, lanes] = acc
        return carry

    lax.fori_loop(0, d // LANE, lane_chunk, 0)
    acc = conv_ref[...]
    mu = jnp.mean(acc, axis=-1, keepdims=True)
    cen = acc - mu
    var = jnp.mean(cen * cen, axis=-1, keepdims=True)
    y = cen * lax.rsqrt(var + EPS) * lng_ref[...] + lnb_ref[...]
    act = (y * jax.nn.sigmoid(y)).astype(BF16)
    o_ref[...] = x_ref[...] + b_ref[...] + jnp.dot(act, w_ref[...], preferred_element_type=F32)


def conv_out(u3, x3, dw_w, dw_b, ln_g, ln_b, w_out, b_out):
    b, s, d = x3.shape
    ts = ROW_TILE
    row = pl.BlockSpec((None, ts, d), lambda bi, j: (bi, j, 0))
    return pl.pallas_call(
        _conv_out_kernel,
        out_shape=jax.ShapeDtypeStruct((b, s, d), F32),
        grid=(b, s // ts),
        in_specs=[row, row, _resident((CONV_WIDTH, d)), _resident((1, d)), _resident((1, d)),
                  _resident((1, d)), _resident((d, d)), _resident((1, d))],
        out_specs=row,
        scratch_shapes=[pltpu.VMEM((ts + CONV_HALO, d), F32), pltpu.VMEM((ts, d), F32),
                        pltpu.VMEM((2, SUBLANES, CONV_SPAN, LANE), F32)],
        compiler_params=_params("parallel", "arbitrary"),
        name="conv_out",
    )(u3, x3, dw_w, dw_b, ln_g, ln_b, w_out, b_out)


def _ffn_kernel(x_ref, g_ref, wg_ref, wu_ref, wd_ref, o_ref):
    x = x_ref[...]
    h = _rms(x, g_ref[...]).astype(BF16)
    a = jnp.dot(h, wg_ref[...], preferred_element_type=F32)
    u = jnp.dot(h, wu_ref[...], preferred_element_type=F32)
    act = (a * jax.nn.sigmoid(a) * u).astype(BF16)
    o_ref[...] = x + jnp.dot(act, wd_ref[...], preferred_element_type=F32)


def ffn_dense(x2, g, w_gate, w_up, w_down):
    t, d = x2.shape
    f = w_gate.shape[1]
    row = pl.BlockSpec((ROW_TILE, d), lambda i: (i, 0))
    return pl.pallas_call(
        _ffn_kernel,
        out_shape=jax.ShapeDtypeStruct((t, d), F32),
        grid=(t // ROW_TILE,),
        in_specs=[row, _resident((1, d)), _resident((d, f)), _resident((d, f)), _resident((f, d))],
        out_specs=row,
        compiler_params=_params("parallel"),
        name="ffn_dense",
    )(x2, g, w_gate, w_up, w_down)


def _store_heads(y, out_ref, gain_ref=None, gain_off=0):
    dil, n, _ = out_ref.shape
    low = lax.broadcasted_iota(jnp.int32, (1, LANE), 1) < HEAD_DIM
    for c in range(y.shape[-1] // LANE):
        cols = slice(c * LANE, (c + 1) * LANE)
        blk = y[:, cols]
        if gain_ref is not None:
            sq = blk * blk
            ms_lo = jnp.sum(jnp.where(low, sq, 0.0), axis=-1, keepdims=True) * (1.0 / HEAD_DIM)
            ms_hi = jnp.sum(jnp.where(low, 0.0, sq), axis=-1, keepdims=True) * (1.0 / HEAD_DIM)
            inv = jnp.where(low, lax.rsqrt(ms_lo + EPS), lax.rsqrt(ms_hi + EPS))
            blk = blk * inv * gain_ref[:, gain_off + c * LANE:gain_off + (c + 1) * LANE]
        blk = blk.astype(out_ref.dtype)
        for r in range(dil):
            out_ref[r, :, cols] = blk[r * n:(r + 1) * n]


def _qkv_kernel(x_ref, gq_ref, gkv_ref, wq_ref, wkv_ref, qn_ref, kn_ref, *refs):
    out_refs, xn_ref = refs[:-1], refs[-1]
    x = x_ref[...]
    xn = x * lax.rsqrt(jnp.mean(x * x, axis=-1, keepdims=True) + EPS)
    n_cols = xn_ref.shape[0]
    for c in range(n_cols):
        xn_ref[c] = xn[:, c * LANE:(c + 1) * LANE]
    tm = x.shape[0]
    hw = N_HEADS * HEAD_DIM
    qw = len(DILATED_GROUPS) * hw
    for g, (_, dil) in enumerate(DILATED_GROUPS):
        n = tm // dil
        xg = xn if dil == 1 else jnp.concatenate(
            [jnp.concatenate([xn_ref[c, pl.ds(r, n, stride=dil), :] for r in range(dil)], axis=0)
             for c in range(n_cols)], axis=1)
        hq = (xg * gq_ref[...]).astype(BF16)
        hkv = (xg * gkv_ref[...]).astype(BF16)
        q_ref, k_ref, v_ref = out_refs[3 * g:3 * g + 3]
        cols = slice(g * hw, (g + 1) * hw)
        _store_heads(jnp.dot(hq, wq_ref[:, cols], preferred_element_type=F32), q_ref, qn_ref, g * hw)
        _store_heads(jnp.dot(hkv, wkv_ref[:, cols], preferred_element_type=F32), k_ref, kn_ref, g * hw)
        _store_heads(jnp.dot(hkv, wkv_ref[:, qw + g * hw:qw + (g + 1) * hw],
                             preferred_element_type=F32), v_ref)


def qkv_proj(x3, gq, gkv, w_q, w_kv, qn, kn):
    b, s, d = x3.shape
    qw = w_q.shape[1]
    hw = N_HEADS * HEAD_DIM
    tm = ROW_TILE
    out_shape, out_specs = [], []
    for _, dil in DILATED_GROUPS:
        for _ in range(3):
            out_shape.append(jax.ShapeDtypeStruct((b, dil, s // dil, hw), BF16))
            out_specs.append(pl.BlockSpec((None, dil, tm // dil, hw), lambda bi, j: (bi, 0, j, 0)))
    return pl.pallas_call(
        _qkv_kernel,
        out_shape=tuple(out_shape),
        grid=(b, s // tm),
        in_specs=[pl.BlockSpec((None, tm, d), lambda bi, j: (bi, j, 0)), _resident((1, d)),
                  _resident((1, d)), _resident((d, qw)), _resident((d, 2 * qw)), _resident((1, qw)),
                  _resident((1, qw))],
        out_specs=tuple(out_specs),
        scratch_shapes=[pltpu.VMEM((d // LANE, tm, LANE), F32)],
        compiler_params=_params("parallel", "parallel"),
        name="qkv_proj",
    )(x3, gq, gkv, w_q, w_kv, qn, kn)


def _t5_bucket(dist):
    n = np.asarray(dist)
    max_exact = N_BUCKETS // 2
    large = max_exact + (np.log(np.maximum(n, 1) / max_exact)
                         / np.log(MAX_DISTANCE / max_exact)
                         * (N_BUCKETS - max_exact)).astype(np.int32)
    large = np.minimum(large, N_BUCKETS - 1)
    return np.where(n < max_exact, n, large).astype(np.int32)


def band_bias(rel_bias, g):
    window, d = DILATED_GROUPS[g]
    span = window // d
    period = 3 * BLK
    diff = np.arange(period)
    diff = np.where(diff >= 2 * BLK, diff - period, diff)
    sub = BLK - diff
    in_band = (sub >= 0) & (sub <= span)
    bucket = _t5_bucket(np.clip(sub, 0, span) * d)
    table = rel_bias[:, g * N_HEADS:(g + 1) * N_HEADS].astype(F32)
    onehot = jnp.asarray(bucket[:, None] == np.arange(N_BUCKETS)[None, :], F32)
    line = jnp.dot(onehot, table, precision=lax.Precision.HIGHEST).T
    line = jnp.where(jnp.asarray(in_band)[None, :], line, NEG)
    flat = jnp.tile(line, (1, BLK))[:, :BLK * (period - 1)]
    general = flat.reshape(N_HEADS, BLK, period - 1)[:, :, :2 * BLK]
    has_prev = jnp.asarray(np.arange(2 * BLK) >= BLK)[None, None, :]
    first = jnp.where(has_prev, general, NEG)
    return jnp.stack([first, general])


def _attn_kernel(q_ref, k_ref, v_ref, bias_ref, hm_ref, o_ref, lse_ref,
                 kprev_ref, vprev_ref, s_ref, p_ref, m_ref):
    i = pl.program_id(1)

    @pl.when(i == 0)
    def _():
        kprev_ref[...] = jnp.zeros(kprev_ref.shape, kprev_ref.dtype)
        vprev_ref[...] = jnp.zeros(vprev_ref.shape, vprev_ref.dtype)

    variant = jnp.minimum(i, 1)
    n_cols = q_ref.shape[-1] // LANE

    for c in range(n_cols):
        cols = slice(c * LANE, (c + 1) * LANE)
        q2 = q_ref[:, cols]
        kcat = jnp.concatenate([kprev_ref[:, cols], k_ref[:, cols]], axis=0)
        for half in range(2):
            h = 2 * c + half
            s = lax.dot_general(q2 * hm_ref[half:half + 1, :], kcat, (((1,), (1,)), ((), ())),
                                preferred_element_type=F32) + bias_ref[variant, h]
            s_ref[h] = s
            m_ref[h] = jnp.broadcast_to(jnp.max(s, axis=-1, keepdims=True), (BLK, LANE))

    for h in range(N_HEADS):
        m = m_ref[h]
        p_ref[h] = jnp.exp(s_ref[h] - jnp.concatenate([m, m], axis=-1)).astype(p_ref.dtype)

    lane = lax.broadcasted_iota(jnp.int32, (BLK, LANE), 1)
    low = lane < HEAD_DIM
    ones = jnp.ones((2 * BLK, LANE), BF16)
    m_tile = jnp.zeros((BLK, LANE), F32)
    l_tile = jnp.ones((BLK, LANE), F32)
    for c in range(n_cols):
        cols = slice(c * LANE, (c + 1) * LANE)
        vcat = jnp.concatenate([vprev_ref[:, cols], v_ref[:, cols]], axis=0)
        acc = []
        for half in range(2):
            rhs = jnp.concatenate([vcat * hm_ref[half:half + 1, :], ones], axis=-1)
            acc.append(jnp.dot(p_ref[2 * c + half], rhs, preferred_element_type=F32))
        den = jnp.where(low, acc[0][:, LANE:], acc[1][:, LANE:])
        o_ref[:, cols] = ((acc[0][:, :LANE] + acc[1][:, :LANE]) / den).astype(o_ref.dtype)
        for half in range(2):
            h = 2 * c + half
            m_tile = jnp.where(lane == h, m_ref[h], m_tile)
            l_tile = jnp.where(lane == h, acc[half][:, LANE:], l_tile)
    lse_ref[...] = m_tile + jnp.log(l_tile)
    kprev_ref[...] = k_ref[...]
    vprev_ref[...] = v_ref[...]


def dilated_attention_group(q, k, v, bias):
    r, l, w = q.shape
    blk = pl.BlockSpec((None, BLK, w), lambda a, i: (a, i, 0))
    lane = np.arange(LANE)[None, :] < HEAD_DIM
    head_mask = jnp.asarray(np.concatenate([lane, ~lane]).astype(np.float32), BF16)
    return pl.pallas_call(
        _attn_kernel,
        out_shape=(jax.ShapeDtypeStruct((r, l, w), BF16),
                   jax.ShapeDtypeStruct((r, l, LANE), F32)),
        grid=(r, l // BLK),
        in_specs=[blk, blk, blk, _resident(bias.shape), _resident((2, LANE))],
        out_specs=(blk, pl.BlockSpec((None, BLK, LANE), lambda a, i: (a, i, 0))),
        scratch_shapes=[pltpu.VMEM((BLK, w), BF16), pltpu.VMEM((BLK, w), BF16),
                        pltpu.VMEM((N_HEADS, BLK, 2 * BLK), F32),
                        pltpu.VMEM((N_HEADS, BLK, 2 * BLK), BF16),
                        pltpu.VMEM((N_HEADS, BLK, LANE), F32)],
        compiler_params=_params("parallel", "arbitrary"),
        name="dilated_attn",
    )(q, k, v, bias, head_mask)


def _attn_out_kernel(x_ref, o0_ref, o1_ref, o2_ref, l0_ref, l1_ref, l2_ref, e_ref, w_ref, o_ref,
                     on_ref, ln_ref):
    n_cols = on_ref.shape[1]
    for gi, (og_ref, lg_ref) in enumerate(((o1_ref, l1_ref), (o2_ref, l2_ref))):
        dil, n, _ = og_ref.shape
        for r in range(dil):
            ln_ref[gi, pl.ds(r, n, stride=dil), :] = lg_ref[r]
            for c in range(n_cols):
                on_ref[gi, c, pl.ds(r, n, stride=dil), :] = (
                    og_ref[r, :, c * LANE:(c + 1) * LANE].astype(F32))
    l0, l1, l2 = l0_ref[0], ln_ref[0], ln_ref[1]
    m = jnp.maximum(jnp.maximum(l0, l1), l2)
    e0, e1, e2 = jnp.exp(l0 - m), jnp.exp(l1 - m), jnp.exp(l2 - m)
    den = e0 + e1 + e2
    acc = None
    token_order = [jnp.concatenate([on_ref[gi, c] for c in range(n_cols)], axis=1) for gi in range(2)]
    for e, o in ((e0, o0_ref[0].astype(F32)), (e1, token_order[0]), (e2, token_order[1])):
        wgt = jnp.dot((e / den).astype(BF16), e_ref[...], preferred_element_type=F32)
        term = wgt * o
        acc = term if acc is None else acc + term
    o_ref[...] = x_ref[...] + jnp.dot(acc.astype(BF16), w_ref[...], preferred_element_type=F32)


def attn_out(x3, outs, lses, w_o):
    b, s, d = x3.shape
    tm = ROW_TILE
    expand = np.zeros((LANE, d), np.float32)
    expand[:N_HEADS] = np.repeat(np.eye(N_HEADS, dtype=np.float32), HEAD_DIM, axis=1)
    expand = jnp.asarray(expand, BF16)
    row = pl.BlockSpec((None, tm, d), lambda bi, j: (bi, j, 0))
    o_specs = [pl.BlockSpec((None, dil, tm // dil, d), lambda bi, j: (bi, 0, j, 0))
               for _, dil in DILATED_GROUPS]
    l_specs = [pl.BlockSpec((None, dil, tm // dil, LANE), lambda bi, j: (bi, 0, j, 0))
               for _, dil in DILATED_GROUPS]
    return pl.pallas_call(
        _attn_out_kernel,
        out_shape=jax.ShapeDtypeStruct((b, s, d), F32),
        grid=(b, s // tm),
        in_specs=[row, *o_specs, *l_specs, _resident((LANE, d)), _resident((d, d))],
        out_specs=row,
        scratch_shapes=[pltpu.VMEM((2, d // LANE, tm, LANE), F32), pltpu.VMEM((2, tm, LANE), F32)],
        compiler_params=_params("parallel", "parallel"),
        name="attn_out",
    )(x3, *outs, *lses, expand, w_o)


def _router_kernel(x_ref, g_ref, w_ref, tri_ref, col_ref, row_ref, cnt_ref):
    h = _rms(x_ref[...], g_ref[...])
    w = w_ref[...]
    h_hi = h.astype(BF16)
    h_lo = (h - h_hi.astype(F32)).astype(BF16)
    w_hi = w.astype(BF16)
    w_lo = (w - w_hi.astype(F32)).astype(BF16)
    logits = (jnp.dot(h_hi, w_hi, preferred_element_type=F32)
              + jnp.dot(h_lo, w_hi, preferred_element_type=F32)
              + jnp.dot(h_hi, w_lo, preferred_element_type=F32))
    ne = N_EXPERTS
    lane = lax.broadcasted_iota(jnp.int32, logits.shape, 1)
    logits = jnp.where(lane < ne, logits, -jnp.inf)
    m1 = jnp.max(logits, axis=-1, keepdims=True)
    i1 = jnp.min(jnp.where(logits == m1, lane, ne), axis=-1, keepdims=True)
    rest = jnp.where(lane == i1, -jnp.inf, logits)
    m2 = jnp.max(rest, axis=-1, keepdims=True)
    i2 = jnp.min(jnp.where(rest == m2, lane, ne), axis=-1, keepdims=True)
    e2 = jnp.exp(m2 - m1)
    g1 = 1.0 / (1.0 + e2)
    g2 = e2 / (1.0 + e2)
    both = (lane == i1) | (lane == i2)
    cnt = jnp.dot(tri_ref[...], jnp.where(both, 1.0, 0.0).astype(BF16), preferred_element_type=F32)
    r1 = jnp.sum(jnp.where(lane == i1, cnt, 0.0), axis=-1, keepdims=True)
    r2 = jnp.sum(jnp.where(lane == i2, cnt, 0.0), axis=-1, keepdims=True)
    cnt_ref[...] = jnp.sum(jnp.where(both, 1, 0), axis=0, keepdims=True)
    fields = (i1.astype(F32), i2.astype(F32), r1, r2, g1, g2)
    meta = jnp.zeros(logits.shape, F32)
    for f, val in enumerate(fields):
        meta = jnp.where(lane == f, val, meta)
    col_ref[...] = meta[:, :META_FIELDS]
    row_ref[...] = meta.T[:META_FIELDS, :]


def router(x2, g, w_router):
    t, d = x2.shape
    tt = MOE_TOKEN_TILE
    nt = t // tt
    tri = jnp.asarray(np.tril(np.ones((tt, tt), np.float32), -1), BF16)
    return pl.pallas_call(
        _router_kernel,
        out_shape=(jax.ShapeDtypeStruct((t, META_FIELDS), F32),
                   jax.ShapeDtypeStruct((nt, META_FIELDS, tt), F32),
                   jax.ShapeDtypeStruct((nt, 1, LANE), jnp.int32)),
        grid=(nt,),
        in_specs=[pl.BlockSpec((tt, d), lambda i: (i, 0)), _resident((1, d)),
                  _resident((d, LANE)), _resident((tt, tt))],
        out_specs=(pl.BlockSpec((tt, META_FIELDS), lambda i: (i, 0)),
                   pl.BlockSpec((None, META_FIELDS, tt), lambda i: (i, 0, 0)),
                   pl.BlockSpec((None, 1, LANE), lambda i: (i, 0, 0))),
        compiler_params=_params("parallel"),
        name="moe_router",
    )(x2, g, _pad_cols(w_router, LANE), tri)


def moe_plan(counts, n_tokens):
    tm = MOE_ROW_TILE
    counts = (counts + SUBLANES - 1) // SUBLANES * SUBLANES
    experts_ = jnp.arange(N_EXPERTS, dtype=jnp.int32)
    total = jnp.sum(counts, axis=0)
    owned = (total + MOE_TOKEN_TILE + tm - 1) // tm
    owned_end = jnp.cumsum(owned)
    region = (owned_end - owned) * tm
    base = region[None, :] + jnp.cumsum(counts, axis=0) - counts
    local = jnp.cumsum(counts, axis=1) - counts
    max_rows = 2 * n_tokens + (SUBLANES - 1) * N_EXPERTS * counts.shape[0] + N_EXPERTS * MOE_TOKEN_TILE
    n_tiles = -(-max_rows // tm) + N_EXPERTS + 1
    tile_id = jnp.arange(n_tiles, dtype=jnp.int32)
    tile_expert = jnp.minimum(jnp.sum(tile_id[:, None] >= owned_end[None, :], axis=1), N_EXPERTS - 1)
    pick = tile_expert[:, None] == experts_[None, :]
    rows_before = (tile_id - jnp.sum(jnp.where(pick, (owned_end - owned)[None, :], 0), axis=1)) * tm
    has_rows = (tile_id < owned_end[-1]) & (rows_before < jnp.sum(jnp.where(pick, total[None, :], 0), axis=1))
    return dict(n_tiles=n_tiles, counts=counts.reshape(-1), base=base.reshape(-1).astype(jnp.int32),
                local=local.reshape(-1).astype(jnp.int32), tail=(region + total).astype(jnp.int32),
                tile_expert=tile_expert.astype(jnp.int32), tile_valid=has_rows.astype(jnp.int32))


def _run_windows(n):
    return (((n > 0) & (n <= MOE_SMALL_WINDOW), MOE_SMALL_WINDOW), (n > MOE_SMALL_WINDOW, MOE_TOKEN_TILE))


def _dispatch_kernel(cnt_ref, local_ref, base_ref, tail_ref, valid_ref, x_ref, g_ref, meta_ref, xs_ref,
                     stage_ref, sem):
    i = pl.program_id(0)
    tt, d = x_ref.shape
    n_sorted = 2 * tt + SUBLANES * N_EXPERTS
    n_zero = stage_ref.shape[1] - n_sorted
    tm = MOE_ROW_TILE

    @pl.when(i == 0)
    def _():
        for side in range(2):
            stage_ref[side, n_sorted:, :] = jnp.zeros((n_zero, stage_ref.shape[2]), F32)

        def zero_copy(row, rows):
            return pltpu.make_async_copy(stage_ref.at[0, pl.ds(n_sorted, rows)],
                                         xs_ref.at[pl.ds(pl.multiple_of(row, SUBLANES), rows)], sem)

        pieces = [n_zero >> b for b in range((n_zero // SUBLANES).bit_length())]
        for action in ("start", "wait"):
            def empty_tile(k, carry, action=action):
                @pl.when(valid_ref[k] == 0)
                def _():
                    for part in range(tm // n_zero):
                        getattr(zero_copy(k * tm + part * n_zero, n_zero), action)()
                return carry

            lax.fori_loop(0, valid_ref.shape[0], empty_tile, 0)
            for e in range(N_EXPERTS):
                row = tail_ref[e]
                left = (-row) & (tm - 1)
                for rows in pieces:
                    @pl.when((left & rows) != 0)
                    def _(row=row, rows=rows):
                        getattr(zero_copy(row, rows), action)()
                    row = row + (left & rows)

    h = _rms(x_ref[...], g_ref[...]).astype(BF16)
    meta = meta_ref[...]
    e1, e2, r1, r2, g1, g2 = (meta[f:f + 1, :] for f in range(6))
    off1 = jnp.zeros_like(r1)
    off2 = jnp.zeros_like(r2)
    for e in range(N_EXPERTS):
        lo = local_ref[i * N_EXPERTS + e].astype(F32)
        off1 = jnp.where(e1 == e, lo, off1)
        off2 = jnp.where(e2 == e, lo, off2)
    slot = lax.broadcasted_iota(jnp.int32, (n_sorted, tt), 0).astype(F32)
    hit1 = slot == r1 + off1
    hit2 = slot == r2 + off2
    perm = jnp.where(hit1 | hit2, 1.0, 0.0).astype(BF16)
    half = i % 2
    stage_ref[half, 0:n_sorted, 0:d] = jnp.dot(perm, h, preferred_element_type=F32)
    gate = jnp.sum(jnp.where(hit1, g1, 0.0) + jnp.where(hit2, g2, 0.0), axis=-1, keepdims=True)
    stage_ref[half, 0:n_sorted, d:] = jnp.broadcast_to(gate, (n_sorted, LANE))

    def run_copies(step, action):
        for e in range(N_EXPERTS):
            k = step * N_EXPERTS + e
            for cond, rows in _run_windows(cnt_ref[k]):
                @pl.when(cond)
                def _(k=k, rows=rows):
                    copy = pltpu.make_async_copy(
                        stage_ref.at[step % 2, pl.ds(pl.multiple_of(local_ref[k], SUBLANES), rows)],
                        xs_ref.at[pl.ds(pl.multiple_of(base_ref[k], SUBLANES), rows)], sem)
                    getattr(copy, action)()

    @pl.when(i > 0)
    def _():
        run_copies(i - 1, "wait")

    run_copies(i, "start")

    @pl.when(i == pl.num_programs(0) - 1)
    def _():
        run_copies(i, "wait")


def dispatch(x2, g, meta_row, plan):
    t, d = x2.shape
    tt = MOE_TOKEN_TILE
    grid_spec = pltpu.PrefetchScalarGridSpec(
        num_scalar_prefetch=5,
        grid=(t // tt,),
        in_specs=[pl.BlockSpec((tt, d), lambda i, *_: (i, 0)),
                  pl.BlockSpec((1, d), lambda i, *_: (0, 0)),
                  pl.BlockSpec((None, META_FIELDS, tt), lambda i, *_: (i, 0, 0))],
        out_specs=pl.BlockSpec(memory_space=pl.ANY),
        scratch_shapes=[pltpu.VMEM((2, 2 * tt + SUBLANES * N_EXPERTS + MOE_TOKEN_TILE, d + LANE), F32),
                        pltpu.SemaphoreType.DMA(())],
    )
    return pl.pallas_call(
        _dispatch_kernel,
        out_shape=jax.ShapeDtypeStruct((plan["n_tiles"] * MOE_ROW_TILE, d + LANE), F32),
        grid_spec=grid_spec,
        compiler_params=_params("arbitrary"),
        name="moe_dispatch",
    )(plan["counts"], plan["local"], plan["base"], plan["tail"], plan["tile_valid"], x2, g, meta_row)


def _experts_kernel(te_ref, tv_ref, x_ref, wg_ref, wu_ref, wd_ref, o_ref, xb_ref):
    i, j = pl.program_id(0), pl.program_id(1)
    d = o_ref.shape[-1]

    @pl.when(j == 0)
    def _():
        xb_ref[...] = x_ref[:, :d].astype(BF16)
        o_ref[...] = jnp.zeros(o_ref.shape, o_ref.dtype)

    @pl.when(tv_ref[i] > 0)
    def _():
        xb = xb_ref[...]
        a = jnp.dot(xb, wg_ref[...], preferred_element_type=F32)
        u = jnp.dot(xb, wu_ref[...], preferred_element_type=F32)
        act = (a * jax.nn.sigmoid(a) * u).astype(BF16)
        o_ref[...] += jnp.dot(act, wd_ref[...], preferred_element_type=F32)

        @pl.when(j == pl.num_programs(1) - 1)
        def _():
            o_ref[...] = o_ref[...] * x_ref[:, d:d + 1]


def experts(xs, plan, w_gate, w_up, w_down):
    n_rows, dw = xs.shape
    d = dw - LANE
    f = w_gate.shape[-1]
    tm, tf = MOE_ROW_TILE, MOE_FF_TILE
    last = f // tf - 1
    grid_spec = pltpu.PrefetchScalarGridSpec(
        num_scalar_prefetch=2,
        grid=(n_rows // tm, f // tf),
        in_specs=[pl.BlockSpec((tm, dw), lambda i, j, te, tv: (i, 0)),
                  pl.BlockSpec((None, d, tf), lambda i, j, te, tv: (te[i], 0, jnp.where(tv[i] > 0, j, last))),
                  pl.BlockSpec((None, d, tf), lambda i, j, te, tv: (te[i], 0, jnp.where(tv[i] > 0, j, last))),
                  pl.BlockSpec((None, tf, d), lambda i, j, te, tv: (te[i], jnp.where(tv[i] > 0, j, last), 0))],
        out_specs=pl.BlockSpec((tm, d), lambda i, j, te, tv: (i, 0)),
        scratch_shapes=[pltpu.VMEM((tm, d), BF16)],
    )
    return pl.pallas_call(
        _experts_kernel,
        out_shape=jax.ShapeDtypeStruct((n_rows, d), F32),
        grid_spec=grid_spec,
        compiler_params=_params("arbitrary", "arbitrary"),
        name="moe_experts",
    )(plan["tile_expert"], plan["tile_valid"], xs, w_gate, w_up, w_down)


def _combine_kernel(cnt_ref, base_ref, x_ref, meta_ref, ys_ref, o_ref, small_ref, wide_ref, sems):
    i = pl.program_id(0)
    tt = x_ref.shape[0]
    meta = meta_ref[...]
    e1, e2, r1, r2 = (meta[:, f:f + 1] for f in range(4))
    widest = cnt_ref[i * N_EXPERTS]
    for e in range(1, N_EXPERTS):
        widest = jnp.maximum(widest, cnt_ref[i * N_EXPERTS + e])

    def small_copies(step):
        return [pltpu.make_async_copy(
            ys_ref.at[pl.ds(pl.multiple_of(base_ref[step * N_EXPERTS + e], SUBLANES), MOE_SMALL_WINDOW)],
            small_ref.at[step % 2, e], sems.at[step % 2]) for e in range(N_EXPERTS)]

    @pl.when(i == 0)
    def _():
        for c in small_copies(i):
            c.start()

    @pl.when(i + 1 < pl.num_programs(0))
    def _():
        for c in small_copies(i + 1):
            c.start()

    for c in small_copies(i):
        c.wait()

    def pick_rows(window_of):
        rows = window_of(0).shape[0]
        pos = lax.broadcasted_iota(jnp.int32, (tt, rows), 1).astype(F32)
        acc = x_ref[...]
        for e in range(N_EXPERTS):
            want1 = jnp.where(e1 == e, r1, -1.0)
            want2 = jnp.where(e2 == e, r2, -1.0)
            pick = jnp.where((pos == want1) | (pos == want2), 1.0, 0.0).astype(BF16)
            acc = acc + jnp.dot(pick, window_of(e).astype(BF16), preferred_element_type=F32)
        o_ref[...] = acc

    @pl.when(widest <= MOE_SMALL_WINDOW)
    def _():
        pick_rows(lambda e: small_ref[i % 2, e])

    @pl.when(widest > MOE_SMALL_WINDOW)
    def _():
        copies = [pltpu.make_async_copy(
            ys_ref.at[pl.ds(pl.multiple_of(base_ref[i * N_EXPERTS + e], SUBLANES), tt)],
            wide_ref.at[e], sems.at[2]) for e in range(N_EXPERTS)]
        for c in copies:
            c.start()
        for c in copies:
            c.wait()
        pick_rows(lambda e: wide_ref[e])


def combine(x2, meta_col, ys, plan):
    t, d = x2.shape
    tt = MOE_TOKEN_TILE
    grid_spec = pltpu.PrefetchScalarGridSpec(
        num_scalar_prefetch=2,
        grid=(t // tt,),
        in_specs=[pl.BlockSpec((tt, d), lambda i, *_: (i, 0)),
                  pl.BlockSpec((tt, META_FIELDS), lambda i, *_: (i, 0)),
                  pl.BlockSpec(memory_space=pl.ANY)],
        out_specs=pl.BlockSpec((tt, d), lambda i, *_: (i, 0)),
        scratch_shapes=[pltpu.VMEM((2, N_EXPERTS, MOE_SMALL_WINDOW, d), F32),
                        pltpu.VMEM((N_EXPERTS, tt, d), F32), pltpu.SemaphoreType.DMA((3,))],
    )
    return pl.pallas_call(
        _combine_kernel,
        out_shape=jax.ShapeDtypeStruct((t, d), F32),
        grid_spec=grid_spec,
        compiler_params=_params("arbitrary"),
        name="moe_combine",
    )(plan["counts"], plan["base"], x2, meta_col, ys)


def _pad_cols(w, mult):
    pad = (-w.shape[-1]) % mult
    return jnp.pad(w, ((0, 0),) * (w.ndim - 1) + ((0, pad),))


def kernel(x, rel_bias, conv_norm, conv_w_in, conv_b_in, conv_dw_w, conv_dw_b, conv_ln_g, conv_ln_b, conv_w_out, conv_b_out, kv_norm, w_kv, k_norm, attn_norm, w_q, q_norm, w_o, ffn_norm, ffn_w_gate, ffn_w_up, ffn_w_down, moe_router, moe_w_gate, moe_w_up, moe_w_down):
    b, s, d = x.shape
    t = b * s
    row = lambda v: v.reshape(1, -1)

    x2 = x.reshape(t, d)
    u = conv_in(x2, row(conv_norm[0]), conv_w_in[0].astype(BF16), row(conv_b_in[0]))
    x3 = conv_out(u.reshape(b, s, d), x, conv_dw_w[0], row(conv_dw_b[0]), row(conv_ln_g[0]),
                  row(conv_ln_b[0]), conv_w_out[0].astype(BF16), row(conv_b_out[0]))
    x2 = x3.reshape(t, d)
    wg = _pad_cols(ffn_w_gate[0], LANE).astype(BF16)
    wu = _pad_cols(ffn_w_up[0], LANE).astype(BF16)
    wd = jnp.pad(ffn_w_down[0], ((0, wg.shape[1] - ffn_w_down.shape[1]), (0, 0))).astype(BF16)
    x2 = ffn_dense(x2, row(ffn_norm[0]), wg, wu, wd)

    qn = row(jnp.tile(q_norm[0] * (HEAD_DIM ** -0.5), (1, N_HEADS)))
    kn = row(jnp.tile(k_norm, (1, N_HEADS)))
    x3 = x2.reshape(b, s, d)
    qkv = qkv_proj(x3, row(attn_norm[0]), row(kv_norm), w_q[0].astype(BF16), w_kv.astype(BF16), qn, kn)
    outs, lses = [], []
    for g, (_, dil) in enumerate(DILATED_GROUPS):
        qr, kr, vr = (a.reshape(b * dil, s // dil, a.shape[-1]) for a in qkv[3 * g:3 * g + 3])
        o_g, lse_g = dilated_attention_group(qr, kr, vr, band_bias(rel_bias, g))
        outs.append(o_g.reshape(b, dil, s // dil, o_g.shape[-1]))
        lses.append(lse_g.reshape(b, dil, s // dil, LANE))
    x2 = attn_out(x3, outs, lses, w_o[0].astype(BF16)).reshape(t, d)

    g_moe = row(ffn_norm[1])
    meta_col, meta_row, counts = router(x2, g_moe, moe_router[0])
    plan = moe_plan(counts[:, 0, :N_EXPERTS], t)
    xs = dispatch(x2, g_moe, meta_row, plan)
    ys = experts(xs, plan, moe_w_gate[0].astype(BF16), moe_w_up[0].astype(BF16),
                 moe_w_down[0].astype(BF16))
    return combine(x2, meta_col, ys, plan).reshape(b, s, d)
```

```python
import functools

import numpy as np
import jax
import jax.numpy as jnp
from jax import lax
from jax.experimental import pallas as pl
from jax.experimental.pallas import tpu as pltpu

F32 = jnp.float32
BF16 = jnp.bfloat16

EPS = 1e-6
NEG = -1e30
CONV_WIDTH = 31
CONV_HALO = 32
N_HEADS = 16
HEAD_DIM = 64
DILATED_GROUPS = ((128, 1), (512, 4), (2048, 16))
BLK = 128
ATTN_BLOCKS_PER_STEP = 2
N_BUCKETS = 32
MAX_DISTANCE = 2048
N_EXPERTS = 8
LANE = 128
SUBLANES = 8
VMEM_LIMIT_BYTES = 56 * 1024 * 1024

ROW_TILE = 512
CONV_CHUNK = 64
CONV_SPAN = CONV_CHUNK + SUBLANES * ((CONV_WIDTH - 1) // SUBLANES)
MOE_ROW_TILE = 1024
MOE_FF_CHUNK = 1024
MOE_TOKEN_TILE = 512
MOE_DISPATCH_WINDOW = 192
MOE_SMALL_WINDOW = 256
META_FIELDS = 8


def _params(*sem):
    return pltpu.CompilerParams(dimension_semantics=sem, vmem_limit_bytes=VMEM_LIMIT_BYTES)


def _resident(shape):
    nd = len(shape)
    return pl.BlockSpec(shape, lambda *_: (0,) * nd, pipeline_mode=pl.Buffered(1))


def _rms(x, g):
    return x * lax.rsqrt(jnp.mean(x * x, axis=-1, keepdims=True) + EPS) * g


def _conv_in_kernel(x_ref, g_ref, w_ref, b_ref, u_ref):
    h = _rms(x_ref[...], g_ref[...])
    y = jnp.dot(h.astype(BF16), w_ref[...], preferred_element_type=F32) + b_ref[...]
    d = u_ref.shape[-1]
    u_ref[...] = (y[:, :d] * jax.nn.sigmoid(y[:, d:])).astype(u_ref.dtype)


def conv_in(x2, g, w_in, b_in):
    t, d = x2.shape
    return pl.pallas_call(
        _conv_in_kernel,
        out_shape=jax.ShapeDtypeStruct((t, d), BF16),
        grid=(t // ROW_TILE,),
        in_specs=[pl.BlockSpec((ROW_TILE, d), lambda i: (i, 0)),
                  _resident((1, d)), _resident((d, 2 * d)), _resident((1, 2 * d))],
        out_specs=pl.BlockSpec((ROW_TILE, d), lambda i: (i, 0)),
        compiler_params=_params("parallel"),
        name="conv_in",
    )(x2, g, w_in, b_in)


def _conv_out_kernel(u_ref, x_ref, dww_ref, dwb_ref, lng_ref, lnb_ref, w_ref, b_ref, o_ref,
                     buf_ref, conv_ref, shift_ref):
    ts, d = x_ref.shape
    j = pl.program_id(1)

    @pl.when(j == 0)
    def _():
        buf_ref[0:CONV_HALO, :] = jnp.zeros((CONV_HALO, d), F32)

    @pl.when(j > 0)
    def _():
        buf_ref[0:CONV_HALO, :] = buf_ref[ts:ts + CONV_HALO, :]

    buf_ref[CONV_HALO:CONV_HALO + ts, :] = u_ref[...].astype(F32)
    first_tap = CONV_HALO - (CONV_WIDTH - 1)

    def lane_chunk(c, carry):
        lanes = pl.ds(pl.multiple_of(c * LANE, LANE), LANE)
        taps = [dww_ref[k:k + 1, lanes] for k in range(CONV_WIDTH)]
        bias = dwb_ref[:, lanes]
        for r0 in range(0, ts, CONV_CHUNK):
            acc = jnp.broadcast_to(bias, (CONV_CHUNK, LANE))
            slot = (r0 // CONV_CHUNK) % 2
            for phase in range(SUBLANES):
                lo = r0 + first_tap + phase
                span = CONV_CHUNK + SUBLANES * (len(range(phase, CONV_WIDTH, SUBLANES)) - 1)
                shift_ref[slot, phase, 0:span, :] = buf_ref[lo:lo + span, lanes]
            for phase in range(SUBLANES):
                for a, k in enumerate(range(phase, CONV_WIDTH, SUBLANES)):
                    acc = acc + taps[k] * shift_ref[slot, phase, SUBLANES * a:SUBLANES * a + CONV_CHUNK, :]
            conv_ref[r0:r0 + CONV_CHUNK, lanes] = acc
        return carry

    lax.fori_loop(0, d // LANE, lane_chunk, 0)
    acc = conv_ref[...]
    mu = jnp.mean(acc, axis=-1, keepdims=True)
    cen = acc - mu
    var = jnp.mean(cen * cen, axis=-1, keepdims=True)
    y = cen * lax.rsqrt(var + EPS) * lng_ref[...] + lnb_ref[...]
    act = (y * jax.nn.sigmoid(y)).astype(BF16)
    o_ref[...] = x_ref[...] + b_ref[...] + jnp.dot(act, w_ref[...], preferred_element_type=F32)


def conv_out(u3, x3, dw_w, dw_b, ln_g, ln_b, w_out, b_out):
    b, s, d = x3.shape
    ts = ROW_TILE
    row = pl.BlockSpec((None, ts, d), lambda bi, j: (bi, j, 0))
    return pl.pallas_call(
        _conv_out_kernel,
        out_shape=jax.ShapeDtypeStruct((b, s, d), F32),
        grid=(b, s // ts),
        in_specs=[row, row, _resident((CONV_WIDTH, d)), _resident((1, d)), _resident((1, d)),
                  _resident((1, d)), _resident((d, d)), _resident((1, d))],
        out_specs=row,
        scratch_shapes=[pltpu.VMEM((ts + CONV_HALO, d), F32), pltpu.VMEM((ts, d), F32),
                        pltpu.VMEM((2, SUBLANES, CONV_SPAN, LANE), F32)],
        compiler_params=_params("parallel", "arbitrary"),
        name="conv_out",
    )(u3, x3, dw_w, dw_b, ln_g, ln_b, w_out, b_out)


def _ffn_kernel(x_ref, g_ref, wg_ref, wu_ref, wd_ref, o_ref):
    x = x_ref[...]
    h = _rms(x, g_ref[...]).astype(BF16)
    a = jnp.dot(h, wg_ref[...], preferred_element_type=F32)
    u = jnp.dot(h, wu_ref[...], preferred_element_type=F32)
    act = (a * jax.nn.sigmoid(a) * u).astype(BF16)
    o_ref[...] = x + jnp.dot(act, wd_ref[...], preferred_element_type=F32)


def ffn_dense(x2, g, w_gate, w_up, w_down):
    t, d = x2.shape
    f = w_gate.shape[1]
    row = pl.BlockSpec((ROW_TILE, d), lambda i: (i, 0))
    return pl.pallas_call(
        _ffn_kernel,
        out_shape=jax.ShapeDtypeStruct((t, d), F32),
        grid=(t // ROW_TILE,),
        in_specs=[row, _resident((1, d)), _resident((d, f)), _resident((d, f)), _resident((f, d))],
        out_specs=row,
        compiler_params=_params("parallel"),
        name="ffn_dense",
    )(x2, g, w_gate, w_up, w_down)


def _store_heads(y, out_ref, gain_ref=None, gain_off=0):
    dil, n, _ = out_ref.shape
    low = lax.broadcasted_iota(jnp.int32, (1, LANE), 1) < HEAD_DIM
    for c in range(y.shape[-1] // LANE):
        cols = slice(c * LANE, (c + 1) * LANE)
        blk = y[:, cols]
        if gain_ref is not None:
            sq = blk * blk
            ms_lo = jnp.sum(jnp.where(low, sq, 0.0), axis=-1, keepdims=True) * (1.0 / HEAD_DIM)
            ms_hi = jnp.sum(jnp.where(low, 0.0, sq), axis=-1, keepdims=True) * (1.0 / HEAD_DIM)
            inv = jnp.where(low, lax.rsqrt(ms_lo + EPS), lax.rsqrt(ms_hi + EPS))
            blk = blk * inv * gain_ref[:, gain_off + c * LANE:gain_off + (c + 1) * LANE]
        blk = blk.astype(out_ref.dtype)
        for r in range(dil):
            out_ref[r, :, cols] = blk[r * n:(r + 1) * n]


def _qkv_kernel(x_ref, gq_ref, gkv_ref, wq_ref, wkv_ref, qn_ref, kn_ref, *refs):
    out_refs, xn_ref = refs[:-1], refs[-1]
    x = x_ref[...]
    xn = x * lax.rsqrt(jnp.mean(x * x, axis=-1, keepdims=True) + EPS)
    n_cols = xn_ref.shape[0]
    for c in range(n_cols):
        xn_ref[c] = xn[:, c * LANE:(c + 1) * LANE]
    tm = x.shape[0]
    hw = N_HEADS * HEAD_DIM
    qw = len(DILATED_GROUPS) * hw
    for g, (_, dil) in enumerate(DILATED_GROUPS):
        n = tm // dil
        xg = xn if dil == 1 else jnp.concatenate(
            [jnp.concatenate([xn_ref[c, pl.ds(r, n, stride=dil), :] for r in range(dil)], axis=0)
             for c in range(n_cols)], axis=1)
        hq = (xg * gq_ref[...]).astype(BF16)
        hkv = (xg * gkv_ref[...]).astype(BF16)
        q_ref, k_ref, v_ref = out_refs[3 * g:3 * g + 3]
        cols = slice(g * hw, (g + 1) * hw)
        _store_heads(jnp.dot(hq, wq_ref[:, cols], preferred_element_type=F32), q_ref, qn_ref, g * hw)
        _store_heads(jnp.dot(hkv, wkv_ref[:, cols], preferred_element_type=F32), k_ref, kn_ref, g * hw)
        _store_heads(jnp.dot(hkv, wkv_ref[:, qw + g * hw:qw + (g + 1) * hw],
                             preferred_element_type=F32), v_ref)


def qkv_proj(x3, gq, gkv, w_q, w_kv, qn, kn):
    b, s, d = x3.shape
    qw = w_q.shape[1]
    hw = N_HEADS * HEAD_DIM
    tm = ROW_TILE
    out_shape, out_specs = [], []
    for _, dil in DILATED_GROUPS:
        for _ in range(3):
            out_shape.append(jax.ShapeDtypeStruct((b, dil, s // dil, hw), BF16))
            out_specs.append(pl.BlockSpec((None, dil, tm // dil, hw), lambda bi, j: (bi, 0, j, 0)))
    return pl.pallas_call(
        _qkv_kernel,
        out_shape=tuple(out_shape),
        grid=(b, s // tm),
        in_specs=[pl.BlockSpec((None, tm, d), lambda bi, j: (bi, j, 0)), _resident((1, d)),
                  _resident((1, d)), _resident((d, qw)), _resident((d, 2 * qw)), _resident((1, qw)),
                  _resident((1, qw))],
        out_specs=tuple(out_specs),
        scratch_shapes=[pltpu.VMEM((d // LANE, tm, LANE), F32)],
        compiler_params=_params("parallel", "parallel"),
        name="qkv_proj",
    )(x3, gq, gkv, w_q, w_kv, qn, kn)


def _t5_bucket(dist):
    n = np.asarray(dist)
    max_exact = N_BUCKETS // 2
    large = max_exact + (np.log(np.maximum(n, 1) / max_exact)
                         / np.log(MAX_DISTANCE / max_exact)
                         * (N_BUCKETS - max_exact)).astype(np.int32)
    large = np.minimum(large, N_BUCKETS - 1)
    return np.where(n < max_exact, n, large).astype(np.int32)


def band_bias(rel_bias, g):
    window, d = DILATED_GROUPS[g]
    span = window // d
    period = 3 * BLK
    diff = np.arange(period)
    diff = np.where(diff >= 2 * BLK, diff - period, diff)
    sub = BLK - diff
    in_band = (sub >= 0) & (sub <= span)
    bucket = _t5_bucket(np.clip(sub, 0, span) * d)
    table = rel_bias[:, g * N_HEADS:(g + 1) * N_HEADS].astype(F32)
    onehot = jnp.asarray(bucket[:, None] == np.arange(N_BUCKETS)[None, :], F32)
    line = jnp.dot(onehot, table, precision=lax.Precision.HIGHEST).T
    line = jnp.where(jnp.asarray(in_band)[None, :], line, NEG)
    flat = jnp.tile(line, (1, BLK))[:, :BLK * (period - 1)]
    general = flat.reshape(N_HEADS, BLK, period - 1)[:, :, :2 * BLK]
    has_prev = jnp.asarray(np.arange(2 * BLK) >= BLK)[None, None, :]
    first = jnp.where(has_prev, general, NEG)
    return jnp.stack([first, general])


def _attn_kernel(q_ref, k_ref, v_ref, bias_ref, hm_ref, o_ref, lse_ref,
                 kprev_ref, vprev_ref, s_ref, p_ref, m_ref):
    i = pl.program_id(1)

    @pl.when(i == 0)
    def _():
        kprev_ref[...] = jnp.zeros(kprev_ref.shape, kprev_ref.dtype)
        vprev_ref[...] = jnp.zeros(vprev_ref.shape, vprev_ref.dtype)

    n_cols = q_ref.shape[-1] // LANE
    lane = lax.broadcasted_iota(jnp.int32, (BLK, LANE), 1)
    low = lane < HEAD_DIM
    ones = jnp.ones((2 * BLK, LANE), BF16)

    for b in range(q_ref.shape[0] // BLK):
        rows = slice(b * BLK, (b + 1) * BLK)
        before = slice((b - 1) * BLK, b * BLK)
        variant = jnp.minimum(i, 1) if b == 0 else 1

        for c in range(n_cols):
            cols = slice(c * LANE, (c + 1) * LANE)
            q2 = q_ref[rows, cols]
            kcat = jnp.concatenate([kprev_ref[:, cols] if b == 0 else k_ref[before, cols],
                                    k_ref[rows, cols]], axis=0)
            for half in range(2):
                h = 2 * c + half
                s = lax.dot_general(q2 * hm_ref[half:half + 1, :], kcat, (((1,), (1,)), ((), ())),
                                    preferred_element_type=F32) + bias_ref[variant, h]
                s_ref[b, h] = s
                m_ref[b, h] = jnp.broadcast_to(jnp.max(s, axis=-1, keepdims=True), (BLK, LANE))

        for h in range(N_HEADS):
            m = m_ref[b, h]
            p_ref[b, h] = jnp.exp(s_ref[b, h] - jnp.concatenate([m, m], axis=-1)).astype(p_ref.dtype)

        m_tile = jnp.zeros((BLK, LANE), F32)
        l_tile = jnp.ones((BLK, LANE), F32)
        for c in range(n_cols):
            cols = slice(c * LANE, (c + 1) * LANE)
            vcat = jnp.concatenate([vprev_ref[:, cols] if b == 0 else v_ref[before, cols],
                                    v_ref[rows, cols]], axis=0)
            acc = []
            for half in range(2):
                rhs = jnp.concatenate([vcat * hm_ref[half:half + 1, :], ones], axis=-1)
                acc.append(jnp.dot(p_ref[b, 2 * c + half], rhs, preferred_element_type=F32))
            den = jnp.where(low, acc[0][:, LANE:], acc[1][:, LANE:])
            o_ref[rows, cols] = ((acc[0][:, :LANE] + acc[1][:, :LANE]) / den).astype(o_ref.dtype)
            for half in range(2):
                h = 2 * c + half
                m_tile = jnp.where(lane == h, m_ref[b, h], m_tile)
                l_tile = jnp.where(lane == h, acc[half][:, LANE:], l_tile)
        lse_ref[rows, :] = m_tile + jnp.log(l_tile)

    last = slice(q_ref.shape[0] - BLK, q_ref.shape[0])
    kprev_ref[...] = k_ref[last, :]
    vprev_ref[...] = v_ref[last, :]


def dilated_attention_group(q, k, v, bias):
    r, l, w = q.shape
    nb = ATTN_BLOCKS_PER_STEP
    rows = nb * BLK
    assert l % rows == 0, (l, rows)
    blk = pl.BlockSpec((None, rows, w), lambda a, i: (a, i, 0))
    lane = np.arange(LANE)[None, :] < HEAD_DIM
    head_mask = jnp.asarray(np.concatenate([lane, ~lane]).astype(np.float32), BF16)
    return pl.pallas_call(
        _attn_kernel,
        out_shape=(jax.ShapeDtypeStruct((r, l, w), BF16),
                   jax.ShapeDtypeStruct((r, l, LANE), F32)),
        grid=(r, l // rows),
        in_specs=[blk, blk, blk, _resident(bias.shape), _resident((2, LANE))],
        out_specs=(blk, pl.BlockSpec((None, rows, LANE), lambda a, i: (a, i, 0))),
        scratch_shapes=[pltpu.VMEM((BLK, w), BF16), pltpu.VMEM((BLK, w), BF16),
                        pltpu.VMEM((nb, N_HEADS, BLK, 2 * BLK), F32),
                        pltpu.VMEM((nb, N_HEADS, BLK, 2 * BLK), BF16),
                        pltpu.VMEM((nb, N_HEADS, BLK, LANE), F32)],
        compiler_params=_params("parallel", "arbitrary"),
        name="dilated_attn",
    )(q, k, v, bias, head_mask)


def _attn_out_kernel(x_ref, o0_ref, o1_ref, o2_ref, l0_ref, l1_ref, l2_ref, e_ref, w_ref, o_ref,
                     on_ref, ln_ref):
    n_cols = on_ref.shape[1]
    for gi, (og_ref, lg_ref) in enumerate(((o1_ref, l1_ref), (o2_ref, l2_ref))):
        dil, n, _ = og_ref.shape
        for r in range(dil):
            ln_ref[gi, pl.ds(r, n, stride=dil), :] = lg_ref[r]
            for c in range(n_cols):
                on_ref[gi, c, pl.ds(r, n, stride=dil), :] = (
                    og_ref[r, :, c * LANE:(c + 1) * LANE].astype(F32))
    l0, l1, l2 = l0_ref[0], ln_ref[0], ln_ref[1]
    m = jnp.maximum(jnp.maximum(l0, l1), l2)
    e0, e1, e2 = jnp.exp(l0 - m), jnp.exp(l1 - m), jnp.exp(l2 - m)
    den = e0 + e1 + e2
    acc = None
    token_order = [jnp.concatenate([on_ref[gi, c] for c in range(n_cols)], axis=1) for gi in range(2)]
    for e, o in ((e0, o0_ref[0].astype(F32)), (e1, token_order[0]), (e2, token_order[1])):
        wgt = jnp.dot((e / den).astype(BF16), e_ref[...], preferred_element_type=F32)
        term = wgt * o
        acc = term if acc is None else acc + term
    o_ref[...] = x_ref[...] + jnp.dot(acc.astype(BF16), w_ref[...], preferred_element_type=F32)


def attn_out(x3, outs, lses, w_o):
    b, s, d = x3.shape
    tm = ROW_TILE
    expand = np.zeros((LANE, d), np.float32)
    expand[:N_HEADS] = np.repeat(np.eye(N_HEADS, dtype=np.float32), HEAD_DIM, axis=1)
    expand = jnp.asarray(expand, BF16)
    row = pl.BlockSpec((None, tm, d), lambda bi, j: (bi, j, 0))
    o_specs = [pl.BlockSpec((None, dil, tm // dil, d), lambda bi, j: (bi, 0, j, 0))
               for _, dil in DILATED_GROUPS]
    l_specs = [pl.BlockSpec((None, dil, tm // dil, LANE), lambda bi, j: (bi, 0, j, 0))
               for _, dil in DILATED_GROUPS]
    return pl.pallas_call(
        _attn_out_kernel,
        out_shape=jax.ShapeDtypeStruct((b, s, d), F32),
        grid=(b, s // tm),
        in_specs=[row, *o_specs, *l_specs, _resident((LANE, d)), _resident((d, d))],
        out_specs=row,
        scratch_shapes=[pltpu.VMEM((2, d // LANE, tm, LANE), F32), pltpu.VMEM((2, tm, LANE), F32)],
        compiler_params=_params("parallel", "parallel"),
        name="attn_out",
    )(x3, *outs, *lses, expand, w_o)


def _router_kernel(x_ref, g_ref, w_ref, tri_ref, col_ref, row_ref, cnt_ref):
    h = _rms(x_ref[...], g_ref[...])
    w = w_ref[...]
    h_hi = h.astype(BF16)
    h_lo = (h - h_hi.astype(F32)).astype(BF16)
    w_hi = w.astype(BF16)
    w_lo = (w - w_hi.astype(F32)).astype(BF16)
    logits = (jnp.dot(h_hi, w_hi, preferred_element_type=F32)
              + jnp.dot(h_lo, w_hi, preferred_element_type=F32)
              + jnp.dot(h_hi, w_lo, preferred_element_type=F32))
    ne = N_EXPERTS
    lane = lax.broadcasted_iota(jnp.int32, logits.shape, 1)
    logits = jnp.where(lane < ne, logits, -jnp.inf)
    m1 = jnp.max(logits, axis=-1, keepdims=True)
    i1 = jnp.min(jnp.where(logits == m1, lane, ne), axis=-1, keepdims=True)
    rest = jnp.where(lane == i1, -jnp.inf, logits)
    m2 = jnp.max(rest, axis=-1, keepdims=True)
    i2 = jnp.min(jnp.where(rest == m2, lane, ne), axis=-1, keepdims=True)
    e2 = jnp.exp(m2 - m1)
    g1 = 1.0 / (1.0 + e2)
    g2 = e2 / (1.0 + e2)
    both = (lane == i1) | (lane == i2)
    cnt = jnp.dot(tri_ref[...], jnp.where(both, 1.0, 0.0).astype(BF16), preferred_element_type=F32)
    r1 = jnp.sum(jnp.where(lane == i1, cnt, 0.0), axis=-1, keepdims=True)
    r2 = jnp.sum(jnp.where(lane == i2, cnt, 0.0), axis=-1, keepdims=True)
    cnt_ref[...] = jnp.sum(jnp.where(both, 1, 0), axis=0, keepdims=True)
    fields = (i1.astype(F32), i2.astype(F32), r1, r2, g1, g2)
    meta = jnp.zeros(logits.shape, F32)
    for f, val in enumerate(fields):
        meta = jnp.where(lane == f, val, meta)
    col_ref[...] = meta[:, :META_FIELDS]
    row_ref[...] = meta.T[:META_FIELDS, :]


def router(x2, g, w_router):
    t, d = x2.shape
    tt = MOE_TOKEN_TILE
    nt = t // tt
    tri = jnp.asarray(np.tril(np.ones((tt, tt), np.float32), -1), BF16)
    return pl.pallas_call(
        _router_kernel,
        out_shape=(jax.ShapeDtypeStruct((t, META_FIELDS), F32),
                   jax.ShapeDtypeStruct((nt, META_FIELDS, tt), F32),
                   jax.ShapeDtypeStruct((nt, 1, LANE), jnp.int32)),
        grid=(nt,),
        in_specs=[pl.BlockSpec((tt, d), lambda i: (i, 0)), _resident((1, d)),
                  _resident((d, LANE)), _resident((tt, tt))],
        out_specs=(pl.BlockSpec((tt, META_FIELDS), lambda i: (i, 0)),
                   pl.BlockSpec((None, META_FIELDS, tt), lambda i: (i, 0, 0)),
                   pl.BlockSpec((None, 1, LANE), lambda i: (i, 0, 0))),
        compiler_params=_params("parallel"),
        name="moe_router",
    )(x2, g, _pad_cols(w_router, LANE), tri)


def moe_plan(counts, n_tokens):
    tm = MOE_ROW_TILE
    counts = (counts + SUBLANES - 1) // SUBLANES * SUBLANES
    experts_ = jnp.arange(N_EXPERTS, dtype=jnp.int32)
    total = jnp.sum(counts, axis=0)
    owned = (total + MOE_TOKEN_TILE + tm - 1) // tm
    owned_end = jnp.cumsum(owned)
    region = (owned_end - owned) * tm
    base = region[None, :] + jnp.cumsum(counts, axis=0) - counts
    local = jnp.cumsum(counts, axis=1) - counts
    max_rows = 2 * n_tokens + (SUBLANES - 1) * N_EXPERTS * counts.shape[0] + N_EXPERTS * MOE_TOKEN_TILE
    n_tiles = -(-max_rows // tm) + N_EXPERTS + 1
    tile_id = jnp.arange(n_tiles, dtype=jnp.int32)
    tile_expert = jnp.minimum(jnp.sum(tile_id[:, None] >= owned_end[None, :], axis=1), N_EXPERTS - 1)
    pick = tile_expert[:, None] == experts_[None, :]
    rows_before = (tile_id - jnp.sum(jnp.where(pick, (owned_end - owned)[None, :], 0), axis=1)) * tm
    has_rows = (tile_id < owned_end[-1]) & (rows_before < jnp.sum(jnp.where(pick, total[None, :], 0), axis=1))
    return dict(n_tiles=n_tiles, counts=counts.reshape(-1), base=base.reshape(-1).astype(jnp.int32),
                local=local.reshape(-1).astype(jnp.int32), tail=(region + total).astype(jnp.int32),
                tile_expert=tile_expert.astype(jnp.int32), tile_valid=has_rows.astype(jnp.int32))


def _run_windows(n):
    return (((n > 0) & (n <= MOE_DISPATCH_WINDOW), MOE_DISPATCH_WINDOW),
            (n > MOE_DISPATCH_WINDOW, MOE_TOKEN_TILE))


def _dispatch_kernel(cnt_ref, local_ref, base_ref, tail_ref, valid_ref, x_ref, g_ref, meta_ref, xs_ref,
                     stage_ref, sem):
    i = pl.program_id(0)
    tt, d = x_ref.shape
    n_sorted = 2 * tt + SUBLANES * N_EXPERTS
    n_zero = stage_ref.shape[1] - n_sorted
    tm = MOE_ROW_TILE

    @pl.when(i == 0)
    def _():
        for side in range(2):
            stage_ref[side, n_sorted:, :] = jnp.zeros((n_zero, stage_ref.shape[2]), F32)

        def zero_copy(row, rows):
            return pltpu.make_async_copy(stage_ref.at[0, pl.ds(n_sorted, rows)],
                                         xs_ref.at[pl.ds(pl.multiple_of(row, SUBLANES), rows)], sem)

        pieces = [n_zero >> b for b in range((n_zero // SUBLANES).bit_length())]
        for action in ("start", "wait"):
            def empty_tile(k, carry, action=action):
                @pl.when(valid_ref[k] == 0)
                def _():
                    for part in range(tm // n_zero):
                        getattr(zero_copy(k * tm + part * n_zero, n_zero), action)()
                return carry

            lax.fori_loop(0, valid_ref.shape[0], empty_tile, 0)
            for e in range(N_EXPERTS):
                row = tail_ref[e]
                left = (-row) & (tm - 1)
                for rows in pieces:
                    @pl.when((left & rows) != 0)
                    def _(row=row, rows=rows):
                        getattr(zero_copy(row, rows), action)()
                    row = row + (left & rows)

    h = _rms(x_ref[...], g_ref[...]).astype(BF16)
    meta = meta_ref[...]
    e1, e2, r1, r2, g1, g2 = (meta[f:f + 1, :] for f in range(6))
    off1 = jnp.zeros_like(r1)
    off2 = jnp.zeros_like(r2)
    for e in range(N_EXPERTS):
        lo = local_ref[i * N_EXPERTS + e].astype(F32)
        off1 = jnp.where(e1 == e, lo, off1)
        off2 = jnp.where(e2 == e, lo, off2)
    slot = lax.broadcasted_iota(jnp.int32, (n_sorted, tt), 0).astype(F32)
    hit1 = slot == r1 + off1
    hit2 = slot == r2 + off2
    perm = jnp.where(hit1 | hit2, 1.0, 0.0).astype(BF16)
    half = i % 2
    stage_ref[half, 0:n_sorted, 0:d] = jnp.dot(perm, h, preferred_element_type=F32)
    gate = jnp.sum(jnp.where(hit1, g1, 0.0) + jnp.where(hit2, g2, 0.0), axis=-1, keepdims=True)
    stage_ref[half, 0:n_sorted, d:] = jnp.broadcast_to(gate, (n_sorted, LANE))

    def run_copies(step, action):
        for e in range(N_EXPERTS):
            k = step * N_EXPERTS + e
            for cond, rows in _run_windows(cnt_ref[k]):
                @pl.when(cond)
                def _(k=k, rows=rows):
                    copy = pltpu.make_async_copy(
                        stage_ref.at[step % 2, pl.ds(pl.multiple_of(local_ref[k], SUBLANES), rows)],
                        xs_ref.at[pl.ds(pl.multiple_of(base_ref[k], SUBLANES), rows)], sem)
                    getattr(copy, action)()

    @pl.when(i > 0)
    def _():
        run_copies(i - 1, "wait")

    run_copies(i, "start")

    @pl.when(i == pl.num_programs(0) - 1)
    def _():
        run_copies(i, "wait")


def dispatch(x2, g, meta_row, plan):
    t, d = x2.shape
    tt = MOE_TOKEN_TILE
    grid_spec = pltpu.PrefetchScalarGridSpec(
        num_scalar_prefetch=5,
        grid=(t // tt,),
        in_specs=[pl.BlockSpec((tt, d), lambda i, *_: (i, 0)),
                  pl.BlockSpec((1, d), lambda i, *_: (0, 0)),
                  pl.BlockSpec((None, META_FIELDS, tt), lambda i, *_: (i, 0, 0))],
        out_specs=pl.BlockSpec(memory_space=pl.ANY),
        scratch_shapes=[pltpu.VMEM((2, 2 * tt + SUBLANES * N_EXPERTS + MOE_TOKEN_TILE, d + LANE), F32),
                        pltpu.SemaphoreType.DMA(())],
    )
    return pl.pallas_call(
        _dispatch_kernel,
        out_shape=jax.ShapeDtypeStruct((plan["n_tiles"] * MOE_ROW_TILE, d + LANE), F32),
        grid_spec=grid_spec,
        compiler_params=_params("arbitrary"),
        name="moe_dispatch",
    )(plan["counts"], plan["local"], plan["base"], plan["tail"], plan["tile_valid"], x2, g, meta_row)


def _experts_kernel(te_ref, tv_ref, x_ref, wg_ref, wu_ref, wd_ref, o_ref):
    i = pl.program_id(0)
    d = o_ref.shape[-1]
    f = wg_ref.shape[-1]

    @pl.when(tv_ref[i] == 0)
    def _():
        o_ref[...] = jnp.zeros(o_ref.shape, o_ref.dtype)

    @pl.when(tv_ref[i] > 0)
    def _():
        xb = x_ref[:, :d].astype(BF16)
        acc = None
        for c0 in range(0, f, MOE_FF_CHUNK):
            cols = slice(c0, min(c0 + MOE_FF_CHUNK, f))
            a = jnp.dot(xb, wg_ref[:, cols], preferred_element_type=F32)
            u = jnp.dot(xb, wu_ref[:, cols], preferred_element_type=F32)
            act = (a * jax.nn.sigmoid(a) * u).astype(BF16)
            part = jnp.dot(act, wd_ref[cols, :], preferred_element_type=F32)
            acc = part if acc is None else acc + part
        o_ref[...] = acc * x_ref[:, d:d + 1]


def experts(xs, plan, w_gate, w_up, w_down):
    n_rows, dw = xs.shape
    d = dw - LANE
    f = w_gate.shape[-1]
    tm = MOE_ROW_TILE
    weights = lambda shape: pl.BlockSpec((None,) + shape, lambda i, te, tv: (te[i], 0, 0),
                                         pipeline_mode=pl.Buffered(1))
    grid_spec = pltpu.PrefetchScalarGridSpec(
        num_scalar_prefetch=2,
        grid=(n_rows // tm,),
        in_specs=[pl.BlockSpec((tm, dw), lambda i, te, tv: (i, 0)),
                  weights((d, f)), weights((d, f)), weights((f, d))],
        out_specs=pl.BlockSpec((tm, d), lambda i, te, tv: (i, 0)),
    )
    return pl.pallas_call(
        _experts_kernel,
        out_shape=jax.ShapeDtypeStruct((n_rows, d), F32),
        grid_spec=grid_spec,
        compiler_params=_params("arbitrary"),
        name="moe_experts",
    )(plan["tile_expert"], plan["tile_valid"], xs, w_gate, w_up, w_down)


def _combine_kernel(cnt_ref, base_ref, x_ref, meta_ref, ys_ref, o_ref, small_ref, wide_ref, sems):
    i = pl.program_id(0)
    tt = x_ref.shape[0]
    meta = meta_ref[...]
    e1, e2, r1, r2 = (meta[:, f:f + 1] for f in range(4))
    widest = cnt_ref[i * N_EXPERTS]
    for e in range(1, N_EXPERTS):
        widest = jnp.maximum(widest, cnt_ref[i * N_EXPERTS + e])

    def small_copies(step):
        return [pltpu.make_async_copy(
            ys_ref.at[pl.ds(pl.multiple_of(base_ref[step * N_EXPERTS + e], SUBLANES), MOE_SMALL_WINDOW)],
            small_ref.at[step % 2, e], sems.at[step % 2]) for e in range(N_EXPERTS)]

    @pl.when(i == 0)
    def _():
        for c in small_copies(i):
            c.start()

    @pl.when(i + 1 < pl.num_programs(0))
    def _():
        for c in small_copies(i + 1):
            c.start()

    for c in small_copies(i):
        c.wait()

    def pick_rows(window_of):
        rows = window_of(0).shape[0]
        pos = lax.broadcasted_iota(jnp.int32, (tt, rows), 1).astype(F32)
        acc = x_ref[...]
        for e in range(N_EXPERTS):
            want1 = jnp.where(e1 == e, r1, -1.0)
            want2 = jnp.where(e2 == e, r2, -1.0)
            pick = jnp.where((pos == want1) | (pos == want2), 1.0, 0.0).astype(BF16)
            acc = acc + jnp.dot(pick, window_of(e).astype(BF16), preferred_element_type=F32)
        o_ref[...] = acc

    @pl.when(widest <= MOE_SMALL_WINDOW)
    def _():
        pick_rows(lambda e: small_ref[i % 2, e])

    @pl.when(widest > MOE_SMALL_WINDOW)
    def _():
        copies = [pltpu.make_async_copy(
            ys_ref.at[pl.ds(pl.multiple_of(base_ref[i * N_EXPERTS + e], SUBLANES), tt)],
            wide_ref.at[e], sems.at[2]) for e in range(N_EXPERTS)]
        for c in copies:
            c.start()
        for c in copies:
            c.wait()
        pick_rows(lambda e: wide_ref[e])


def combine(x2, meta_col, ys, plan):
    t, d = x2.shape
    tt = MOE_TOKEN_TILE
    grid_spec = pltpu.PrefetchScalarGridSpec(
        num_scalar_prefetch=2,
        grid=(t // tt,),
        in_specs=[pl.BlockSpec((tt, d), lambda i, *_: (i, 0)),
                  pl.BlockSpec((tt, META_FIELDS), lambda i, *_: (i, 0)),
                  pl.BlockSpec(memory_space=pl.ANY)],
        out_specs=pl.BlockSpec((tt, d), lambda i, *_: (i, 0)),
        scratch_shapes=[pltpu.VMEM((2, N_EXPERTS, MOE_SMALL_WINDOW, d), F32),
                        pltpu.VMEM((N_EXPERTS, tt, d), F32), pltpu.SemaphoreType.DMA((3,))],
    )
    return pl.pallas_call(
        _combine_kernel,
        out_shape=jax.ShapeDtypeStruct((t, d), F32),
        grid_spec=grid_spec,
        compiler_params=_params("arbitrary"),
        name="moe_combine",
    )(plan["counts"], plan["base"], x2, meta_col, ys)


def _pad_cols(w, mult):
    pad = (-w.shape[-1]) % mult
    return jnp.pad(w, ((0, 0),) * (w.ndim - 1) + ((0, pad),))


def kernel(x, rel_bias, conv_norm, conv_w_in, conv_b_in, conv_dw_w, conv_dw_b, conv_ln_g, conv_ln_b, conv_w_out, conv_b_out, kv_norm, w_kv, k_norm, attn_norm, w_q, q_norm, w_o, ffn_norm, ffn_w_gate, ffn_w_up, ffn_w_down, moe_router, moe_w_gate, moe_w_up, moe_w_down):
    b, s, d = x.shape
    t = b * s
    row = lambda v: v.reshape(1, -1)

    x2 = x.reshape(t, d)
    u = conv_in(x2, row(conv_norm[0]), conv_w_in[0].astype(BF16), row(conv_b_in[0]))
    x3 = conv_out(u.reshape(b, s, d), x, conv_dw_w[0], row(conv_dw_b[0]), row(conv_ln_g[0]),
                  row(conv_ln_b[0]), conv_w_out[0].astype(BF16), row(conv_b_out[0]))
    x2 = x3.reshape(t, d)
    wg = _pad_cols(ffn_w_gate[0], LANE).astype(BF16)
    wu = _pad_cols(ffn_w_up[0], LANE).astype(BF16)
    wd = jnp.pad(ffn_w_down[0], ((0, wg.shape[1] - ffn_w_down.shape[1]), (0, 0))).astype(BF16)
    x2 = ffn_dense(x2, row(ffn_norm[0]), wg, wu, wd)

    qn = row(jnp.tile(q_norm[0] * (HEAD_DIM ** -0.5), (1, N_HEADS)))
    kn = row(jnp.tile(k_norm, (1, N_HEADS)))
    x3 = x2.reshape(b, s, d)
    qkv = qkv_proj(x3, row(attn_norm[0]), row(kv_norm), w_q[0].astype(BF16), w_kv.astype(BF16), qn, kn)
    outs, lses = [], []
    for g, (_, dil) in enumerate(DILATED_GROUPS):
        qr, kr, vr = (a.reshape(b * dil, s // dil, a.shape[-1]) for a in qkv[3 * g:3 * g + 3])
        o_g, lse_g = dilated_attention_group(qr, kr, vr, band_bias(rel_bias, g))
        outs.append(o_g.reshape(b, dil, s // dil, o_g.shape[-1]))
        lses.append(lse_g.reshape(b, dil, s // dil, LANE))
    x2 = attn_out(x3, outs, lses, w_o[0].astype(BF16)).reshape(t, d)

    g_moe = row(ffn_norm[1])
    meta_col, meta_row, counts = router(x2, g_moe, moe_router[0])
    plan = moe_plan(counts[:, 0, :N_EXPERTS], t)
    xs = dispatch(x2, g_moe, meta_row, plan)
    ys = experts(xs, plan, moe_w_gate[0].astype(BF16), moe_w_up[0].astype(BF16),
                 moe_w_down[0].astype(BF16))
    return combine(x2, meta_col, ys, plan).reshape(b, s, d)
```

```python
import functools

import numpy as np
import jax
import jax.numpy as jnp
from jax import lax
from jax.experimental import pallas as pl
from jax.experimental.pallas import tpu as pltpu

F32 = jnp.float32
BF16 = jnp.bfloat16

EPS = 1e-6
NEG = -1e30
CONV_WIDTH = 31
CONV_HALO = 32
N_HEADS = 16
HEAD_DIM = 64
DILATED_GROUPS = ((128, 1), (512, 4), (2048, 16))
BLK = 128
ATTN_BLOCKS_PER_STEP = 4
N_BUCKETS = 32
MAX_DISTANCE = 2048
N_EXPERTS = 8
LANE = 128
SUBLANES = 8
VMEM_LIMIT_BYTES = 56 * 1024 * 1024

ROW_TILE = 512
CONV_CHUNK = 64
CONV_SPAN = CONV_CHUNK + SUBLANES * ((CONV_WIDTH - 1) // SUBLANES)
CONV_WINDOW = 128
MOE_ROW_TILE = 1024
MOE_FF_CHUNK = 1024
MOE_TOKEN_TILE = 512
MOE_DISPATCH_WINDOW = 192
MOE_SMALL_WINDOW = 256
META_FIELDS = 8


def _params(*sem):
    return pltpu.CompilerParams(dimension_semantics=sem, vmem_limit_bytes=VMEM_LIMIT_BYTES)


def _resident(shape):
    nd = len(shape)
    return pl.BlockSpec(shape, lambda *_: (0,) * nd, pipeline_mode=pl.Buffered(1))


def _rms(x, g):
    return x * lax.rsqrt(jnp.mean(x * x, axis=-1, keepdims=True) + EPS) * g


def _conv_in_kernel(x_ref, g_ref, w_ref, b_ref, u_ref):
    h = _rms(x_ref[...], g_ref[...])
    y = jnp.dot(h.astype(BF16), w_ref[...], preferred_element_type=F32) + b_ref[...]
    d = u_ref.shape[-1]
    u_ref[...] = (y[:, :d] * jax.nn.sigmoid(y[:, d:])).astype(u_ref.dtype)


def conv_in(x2, g, w_in, b_in):
    t, d = x2.shape
    return pl.pallas_call(
        _conv_in_kernel,
        out_shape=jax.ShapeDtypeStruct((t, d), BF16),
        grid=(t // ROW_TILE,),
        in_specs=[pl.BlockSpec((ROW_TILE, d), lambda i: (i, 0)),
                  _resident((1, d)), _resident((d, 2 * d)), _resident((1, 2 * d))],
        out_specs=pl.BlockSpec((ROW_TILE, d), lambda i: (i, 0)),
        compiler_params=_params("parallel"),
        name="conv_in",
    )(x2, g, w_in, b_in)


def _conv_out_kernel(u_ref, x_ref, dww_ref, dwb_ref, lng_ref, lnb_ref, w_ref, b_ref, shift_ref, o_ref,
                     buf_ref, conv_ref):
    ts, d = x_ref.shape
    j = pl.program_id(1)

    @pl.when(j == 0)
    def _():
        buf_ref[0:CONV_HALO, :] = jnp.zeros((CONV_HALO, d), buf_ref.dtype)
        buf_ref[CONV_HALO + ts:, :] = jnp.zeros((buf_ref.shape[0] - CONV_HALO - ts, d), buf_ref.dtype)

    @pl.when(j > 0)
    def _():
        buf_ref[0:CONV_HALO, :] = buf_ref[ts:ts + CONV_HALO, :]

    buf_ref[CONV_HALO:CONV_HALO + ts, :] = u_ref[...]

    def lane_chunk(c, carry):
        lanes = pl.ds(pl.multiple_of(c * LANE, LANE), LANE)
        taps = [dww_ref[k:k + 1, lanes] for k in range(CONV_WIDTH)]
        bias = dwb_ref[:, lanes]
        for r0 in range(0, ts, CONV_CHUNK):
            shifted = jnp.dot(shift_ref[...], buf_ref[r0:r0 + CONV_WINDOW, lanes],
                              preferred_element_type=F32)
            acc = jnp.broadcast_to(bias, (CONV_CHUNK, LANE))
            for phase in range(SUBLANES):
                for a, k in enumerate(range(phase, CONV_WIDTH, SUBLANES)):
                    lo = phase * CONV_SPAN + SUBLANES * a
                    acc = acc + taps[k] * shifted[lo:lo + CONV_CHUNK]
            conv_ref[r0:r0 + CONV_CHUNK, lanes] = acc
        return carry

    lax.fori_loop(0, d // LANE, lane_chunk, 0)
    acc = conv_ref[...]
    mu = jnp.mean(acc, axis=-1, keepdims=True)
    cen = acc - mu
    var = jnp.mean(cen * cen, axis=-1, keepdims=True)
    y = cen * lax.rsqrt(var + EPS) * lng_ref[...] + lnb_ref[...]
    act = (y * jax.nn.sigmoid(y)).astype(BF16)
    o_ref[...] = x_ref[...] + b_ref[...] + jnp.dot(act, w_ref[...], preferred_element_type=F32)


def conv_out(u3, x3, dw_w, dw_b, ln_g, ln_b, w_out, b_out):
    b, s, d = x3.shape
    ts = ROW_TILE
    row = pl.BlockSpec((None, ts, d), lambda bi, j: (bi, j, 0))
    first_tap = CONV_HALO - (CONV_WIDTH - 1)
    shift = np.zeros((SUBLANES * CONV_SPAN, CONV_WINDOW), np.float32)
    for p in range(SUBLANES):
        i = np.arange(CONV_SPAN)
        shift[p * CONV_SPAN + i, first_tap + p + i] = 1.0
    return pl.pallas_call(
        _conv_out_kernel,
        out_shape=jax.ShapeDtypeStruct((b, s, d), F32),
        grid=(b, s // ts),
        in_specs=[row, row, _resident((CONV_WIDTH, d)), _resident((1, d)), _resident((1, d)),
                  _resident((1, d)), _resident((d, d)), _resident((1, d)), _resident(shift.shape)],
        out_specs=row,
        scratch_shapes=[pltpu.VMEM((ts - CONV_CHUNK + CONV_WINDOW, d), BF16), pltpu.VMEM((ts, d), F32)],
        compiler_params=_params("parallel", "arbitrary"),
        name="conv_out",
    )(u3, x3, dw_w, dw_b, ln_g, ln_b, w_out, b_out, jnp.asarray(shift, BF16))


def _ffn_kernel(x_ref, g_ref, wg_ref, wu_ref, wd_ref, o_ref):
    x = x_ref[...]
    h = _rms(x, g_ref[...]).astype(BF16)
    a = jnp.dot(h, wg_ref[...], preferred_element_type=F32)
    u = jnp.dot(h, wu_ref[...], preferred_element_type=F32)
    act = (a * jax.nn.sigmoid(a) * u).astype(BF16)
    o_ref[...] = x + jnp.dot(act, wd_ref[...], preferred_element_type=F32)


def ffn_dense(x2, g, w_gate, w_up, w_down):
    t, d = x2.shape
    f = w_gate.shape[1]
    row = pl.BlockSpec((ROW_TILE, d), lambda i: (i, 0))
    return pl.pallas_call(
        _ffn_kernel,
        out_shape=jax.ShapeDtypeStruct((t, d), F32),
        grid=(t // ROW_TILE,),
        in_specs=[row, _resident((1, d)), _resident((d, f)), _resident((d, f)), _resident((f, d))],
        out_specs=row,
        compiler_params=_params("parallel"),
        name="ffn_dense",
    )(x2, g, w_gate, w_up, w_down)


def _store_heads(y, out_ref, gain_ref=None, gain_off=0):
    dil, n, _ = out_ref.shape
    low = lax.broadcasted_iota(jnp.int32, (1, LANE), 1) < HEAD_DIM
    for c in range(y.shape[-1] // LANE):
        cols = slice(c * LANE, (c + 1) * LANE)
        blk = y[:, cols]
        if gain_ref is not None:
            sq = blk * blk
            ms_lo = jnp.sum(jnp.where(low, sq, 0.0), axis=-1, keepdims=True) * (1.0 / HEAD_DIM)
            ms_hi = jnp.sum(jnp.where(low, 0.0, sq), axis=-1, keepdims=True) * (1.0 / HEAD_DIM)
            inv = jnp.where(low, lax.rsqrt(ms_lo + EPS), lax.rsqrt(ms_hi + EPS))
            blk = blk * inv * gain_ref[:, gain_off + c * LANE:gain_off + (c + 1) * LANE]
        blk = blk.astype(out_ref.dtype)
        for r in range(dil):
            out_ref[r, :, cols] = blk[r * n:(r + 1) * n]


def _qkv_kernel(x_ref, gq_ref, gkv_ref, wq_ref, wkv_ref, qn_ref, kn_ref, *refs):
    out_refs, xn_ref = refs[:-1], refs[-1]
    x = x_ref[...]
    xn = x * lax.rsqrt(jnp.mean(x * x, axis=-1, keepdims=True) + EPS)
    n_cols = xn_ref.shape[0]
    for c in range(n_cols):
        xn_ref[c] = xn[:, c * LANE:(c + 1) * LANE]
    tm = x.shape[0]
    hw = N_HEADS * HEAD_DIM
    qw = len(DILATED_GROUPS) * hw
    for g, (_, dil) in enumerate(DILATED_GROUPS):
        n = tm // dil
        xg = xn if dil == 1 else jnp.concatenate(
            [jnp.concatenate([xn_ref[c, pl.ds(r, n, stride=dil), :] for r in range(dil)], axis=0)
             for c in range(n_cols)], axis=1)
        hq = (xg * gq_ref[...]).astype(BF16)
        hkv = (xg * gkv_ref[...]).astype(BF16)
        q_ref, k_ref, v_ref = out_refs[3 * g:3 * g + 3]
        cols = slice(g * hw, (g + 1) * hw)
        _store_heads(jnp.dot(hq, wq_ref[:, cols], preferred_element_type=F32), q_ref, qn_ref, g * hw)
        _store_heads(jnp.dot(hkv, wkv_ref[:, cols], preferred_element_type=F32), k_ref, kn_ref, g * hw)
        _store_heads(jnp.dot(hkv, wkv_ref[:, qw + g * hw:qw + (g + 1) * hw],
                             preferred_element_type=F32), v_ref)


def qkv_proj(x3, gq, gkv, w_q, w_kv, qn, kn):
    b, s, d = x3.shape
    qw = w_q.shape[1]
    hw = N_HEADS * HEAD_DIM
    tm = ROW_TILE
    out_shape, out_specs = [], []
    for _, dil in DILATED_GROUPS:
        for _ in range(3):
            out_shape.append(jax.ShapeDtypeStruct((b, dil, s // dil, hw), BF16))
            out_specs.append(pl.BlockSpec((None, dil, tm // dil, hw), lambda bi, j: (bi, 0, j, 0)))
    return pl.pallas_call(
        _qkv_kernel,
        out_shape=tuple(out_shape),
        grid=(b, s // tm),
        in_specs=[pl.BlockSpec((None, tm, d), lambda bi, j: (bi, j, 0)), _resident((1, d)),
                  _resident((1, d)), _resident((d, qw)), _resident((d, 2 * qw)), _resident((1, qw)),
                  _resident((1, qw))],
        out_specs=tuple(out_specs),
        scratch_shapes=[pltpu.VMEM((d // LANE, tm, LANE), F32)],
        compiler_params=_params("parallel", "parallel"),
        name="qkv_proj",
    )(x3, gq, gkv, w_q, w_kv, qn, kn)


def _t5_bucket(dist):
    n = np.asarray(dist)
    max_exact = N_BUCKETS // 2
    large = max_exact + (np.log(np.maximum(n, 1) / max_exact)
                         / np.log(MAX_DISTANCE / max_exact)
                         * (N_BUCKETS - max_exact)).astype(np.int32)
    large = np.minimum(large, N_BUCKETS - 1)
    return np.where(n < max_exact, n, large).astype(np.int32)


def band_bias(rel_bias, g):
    window, d = DILATED_GROUPS[g]
    span = window // d
    period = 3 * BLK
    diff = np.arange(period)
    diff = np.where(diff >= 2 * BLK, diff - period, diff)
    sub = BLK - diff
    in_band = (sub >= 0) & (sub <= span)
    bucket = _t5_bucket(np.clip(sub, 0, span) * d)
    table = rel_bias[:, g * N_HEADS:(g + 1) * N_HEADS].astype(F32)
    onehot = jnp.asarray(bucket[:, None] == np.arange(N_BUCKETS)[None, :], F32)
    line = jnp.dot(onehot, table, precision=lax.Precision.HIGHEST).T
    line = jnp.where(jnp.asarray(in_band)[None, :], line, NEG)
    flat = jnp.tile(line, (1, BLK))[:, :BLK * (period - 1)]
    general = flat.reshape(N_HEADS, BLK, period - 1)[:, :, :2 * BLK]
    has_prev = jnp.asarray(np.arange(2 * BLK) >= BLK)[None, None, :]
    first = jnp.where(has_prev, general, NEG)
    return jnp.stack([first, general])


def _attn_kernel(q_ref, k_ref, v_ref, bias_ref, hm_ref, o_ref, lse_ref,
                 kprev_ref, vprev_ref, s_ref, p_ref, m_ref):
    i = pl.program_id(1)

    @pl.when(i == 0)
    def _():
        kprev_ref[...] = jnp.zeros(kprev_ref.shape, kprev_ref.dtype)
        vprev_ref[...] = jnp.zeros(vprev_ref.shape, vprev_ref.dtype)

    n_cols = q_ref.shape[-1] // LANE
    lane = lax.broadcasted_iota(jnp.int32, (BLK, LANE), 1)
    low = lane < HEAD_DIM
    ones = jnp.ones((2 * BLK, LANE), BF16)

    for b in range(q_ref.shape[0] // BLK):
        rows = slice(b * BLK, (b + 1) * BLK)
        before = slice((b - 1) * BLK, b * BLK)
        variant = jnp.minimum(i, 1) if b == 0 else 1

        for c in range(n_cols):
            cols = slice(c * LANE, (c + 1) * LANE)
            q2 = q_ref[rows, cols]
            kcat = jnp.concatenate([kprev_ref[:, cols] if b == 0 else k_ref[before, cols],
                                    k_ref[rows, cols]], axis=0)
            for half in range(2):
                h = 2 * c + half
                s = lax.dot_general(q2 * hm_ref[half:half + 1, :], kcat, (((1,), (1,)), ((), ())),
                                    preferred_element_type=F32) + bias_ref[variant, h]
                s_ref[b, h] = s
                m_ref[b, h] = jnp.broadcast_to(jnp.max(s, axis=-1, keepdims=True), (BLK, LANE))

        for h in range(N_HEADS):
            m = m_ref[b, h]
            p_ref[b, h] = jnp.exp(s_ref[b, h] - jnp.concatenate([m, m], axis=-1)).astype(p_ref.dtype)

        m_tile = jnp.zeros((BLK, LANE), F32)
        l_tile = jnp.ones((BLK, LANE), F32)
        for c in range(n_cols):
            cols = slice(c * LANE, (c + 1) * LANE)
            vcat = jnp.concatenate([vprev_ref[:, cols] if b == 0 else v_ref[before, cols],
                                    v_ref[rows, cols]], axis=0)
            acc = []
            for half in range(2):
                rhs = jnp.concatenate([vcat * hm_ref[half:half + 1, :], ones], axis=-1)
                acc.append(jnp.dot(p_ref[b, 2 * c + half], rhs, preferred_element_type=F32))
            den = jnp.where(low, acc[0][:, LANE:], acc[1][:, LANE:])
            o_ref[rows, cols] = ((acc[0][:, :LANE] + acc[1][:, :LANE]) / den).astype(o_ref.dtype)
            for half in range(2):
                h = 2 * c + half
                m_tile = jnp.where(lane == h, m_ref[b, h], m_tile)
                l_tile = jnp.where(lane == h, acc[half][:, LANE:], l_tile)
        lse_ref[rows, :] = m_tile + jnp.log(l_tile)

    last = slice(q_ref.shape[0] - BLK, q_ref.shape[0])
    kprev_ref[...] = k_ref[last, :]
    vprev_ref[...] = v_ref[last, :]


def dilated_attention_group(q, k, v, bias):
    r, l, w = q.shape
    nb = min(ATTN_BLOCKS_PER_STEP, l // BLK)
    rows = nb * BLK
    assert l % rows == 0, (l, rows)
    blk = pl.BlockSpec((None, rows, w), lambda a, i: (a, i, 0))
    lane = np.arange(LANE)[None, :] < HEAD_DIM
    head_mask = jnp.asarray(np.concatenate([lane, ~lane]).astype(np.float32), BF16)
    return pl.pallas_call(
        _attn_kernel,
        out_shape=(jax.ShapeDtypeStruct((r, l, w), BF16),
                   jax.ShapeDtypeStruct((r, l, LANE), F32)),
        grid=(r, l // rows),
        in_specs=[blk, blk, blk, _resident(bias.shape), _resident((2, LANE))],
        out_specs=(blk, pl.BlockSpec((None, rows, LANE), lambda a, i: (a, i, 0))),
        scratch_shapes=[pltpu.VMEM((BLK, w), BF16), pltpu.VMEM((BLK, w), BF16),
                        pltpu.VMEM((nb, N_HEADS, BLK, 2 * BLK), F32),
                        pltpu.VMEM((nb, N_HEADS, BLK, 2 * BLK), BF16),
                        pltpu.VMEM((nb, N_HEADS, BLK, LANE), F32)],
        compiler_params=_params("parallel", "arbitrary"),
        name="dilated_attn",
    )(q, k, v, bias, head_mask)


def _attn_out_kernel(x_ref, o0_ref, o1_ref, o2_ref, l0_ref, l1_ref, l2_ref, e_ref, w_ref, o_ref,
                     on_ref, ln_ref):
    n_cols = on_ref.shape[1]
    for gi, (og_ref, lg_ref) in enumerate(((o1_ref, l1_ref), (o2_ref, l2_ref))):
        dil, n, _ = og_ref.shape
        for r in range(dil):
            ln_ref[gi, pl.ds(r, n, stride=dil), :] = lg_ref[r]
            for c in range(n_cols):
                on_ref[gi, c, pl.ds(r, n, stride=dil), :] = (
                    og_ref[r, :, c * LANE:(c + 1) * LANE].astype(F32))
    l0, l1, l2 = l0_ref[0], ln_ref[0], ln_ref[1]
    m = jnp.maximum(jnp.maximum(l0, l1), l2)
    e0, e1, e2 = jnp.exp(l0 - m), jnp.exp(l1 - m), jnp.exp(l2 - m)
    den = e0 + e1 + e2
    acc = None
    token_order = [jnp.concatenate([on_ref[gi, c] for c in range(n_cols)], axis=1) for gi in range(2)]
    for e, o in ((e0, o0_ref[0].astype(F32)), (e1, token_order[0]), (e2, token_order[1])):
        wgt = jnp.dot((e / den).astype(BF16), e_ref[...], preferred_element_type=F32)
        term = wgt * o
        acc = term if acc is None else acc + term
    o_ref[...] = x_ref[...] + jnp.dot(acc.astype(BF16), w_ref[...], preferred_element_type=F32)


def attn_out(x3, outs, lses, w_o):
    b, s, d = x3.shape
    tm = ROW_TILE
    expand = np.zeros((LANE, d), np.float32)
    expand[:N_HEADS] = np.repeat(np.eye(N_HEADS, dtype=np.float32), HEAD_DIM, axis=1)
    expand = jnp.asarray(expand, BF16)
    row = pl.BlockSpec((None, tm, d), lambda bi, j: (bi, j, 0))
    o_specs = [pl.BlockSpec((None, dil, tm // dil, d), lambda bi, j: (bi, 0, j, 0))
               for _, dil in DILATED_GROUPS]
    l_specs = [pl.BlockSpec((None, dil, tm // dil, LANE), lambda bi, j: (bi, 0, j, 0))
               for _, dil in DILATED_GROUPS]
    return pl.pallas_call(
        _attn_out_kernel,
        out_shape=jax.ShapeDtypeStruct((b, s, d), F32),
        grid=(b, s // tm),
        in_specs=[row, *o_specs, *l_specs, _resident((LANE, d)), _resident((d, d))],
        out_specs=row,
        scratch_shapes=[pltpu.VMEM((2, d // LANE, tm, LANE), F32), pltpu.VMEM((2, tm, LANE), F32)],
        compiler_params=_params("parallel", "parallel"),
        name="attn_out",
    )(x3, *outs, *lses, expand, w_o)


def _router_kernel(x_ref, g_ref, w_ref, tri_ref, col_ref, row_ref, cnt_ref):
    h = _rms(x_ref[...], g_ref[...])
    w = w_ref[...]
    h_hi = h.astype(BF16)
    h_lo = (h - h_hi.astype(F32)).astype(BF16)
    w_hi = w.astype(BF16)
    w_lo = (w - w_hi.astype(F32)).astype(BF16)
    logits = (jnp.dot(h_hi, w_hi, preferred_element_type=F32)
              + jnp.dot(h_lo, w_hi, preferred_element_type=F32)
              + jnp.dot(h_hi, w_lo, preferred_element_type=F32))
    ne = N_EXPERTS
    lane = lax.broadcasted_iota(jnp.int32, logits.shape, 1)
    logits = jnp.where(lane < ne, logits, -jnp.inf)
    m1 = jnp.max(logits, axis=-1, keepdims=True)
    i1 = jnp.min(jnp.where(logits == m1, lane, ne), axis=-1, keepdims=True)
    rest = jnp.where(lane == i1, -jnp.inf, logits)
    m2 = jnp.max(rest, axis=-1, keepdims=True)
    i2 = jnp.min(jnp.where(rest == m2, lane, ne), axis=-1, keepdims=True)
    e2 = jnp.exp(m2 - m1)
    g1 = 1.0 / (1.0 + e2)
    g2 = e2 / (1.0 + e2)
    both = (lane == i1) | (lane == i2)
    cnt = jnp.dot(tri_ref[...], jnp.where(both, 1.0, 0.0).astype(BF16), preferred_element_type=F32)
    r1 = jnp.sum(jnp.where(lane == i1, cnt, 0.0), axis=-1, keepdims=True)
    r2 = jnp.sum(jnp.where(lane == i2, cnt, 0.0), axis=-1, keepdims=True)
    cnt_ref[...] = jnp.sum(jnp.where(both, 1, 0), axis=0, keepdims=True)
    fields = (i1.astype(F32), i2.astype(F32), r1, r2, g1, g2)
    meta = jnp.zeros(logits.shape, F32)
    for f, val in enumerate(fields):
        meta = jnp.where(lane == f, val, meta)
    col_ref[...] = meta[:, :META_FIELDS]
    row_ref[...] = meta.T[:META_FIELDS, :]


def router(x2, g, w_router):
    t, d = x2.shape
    tt = MOE_TOKEN_TILE
    nt = t // tt
    tri = jnp.asarray(np.tril(np.ones((tt, tt), np.float32), -1), BF16)
    return pl.pallas_call(
        _router_kernel,
        out_shape=(jax.ShapeDtypeStruct((t, META_FIELDS), F32),
                   jax.ShapeDtypeStruct((nt, META_FIELDS, tt), F32),
                   jax.ShapeDtypeStruct((nt, 1, LANE), jnp.int32)),
        grid=(nt,),
        in_specs=[pl.BlockSpec((tt, d), lambda i: (i, 0)), _resident((1, d)),
                  _resident((d, LANE)), _resident((tt, tt))],
        out_specs=(pl.BlockSpec((tt, META_FIELDS), lambda i: (i, 0)),
                   pl.BlockSpec((None, META_FIELDS, tt), lambda i: (i, 0, 0)),
                   pl.BlockSpec((None, 1, LANE), lambda i: (i, 0, 0))),
        compiler_params=_params("parallel"),
        name="moe_router",
    )(x2, g, _pad_cols(w_router, LANE), tri)


def moe_plan(counts, n_tokens):
    tm = MOE_ROW_TILE
    counts = (counts + SUBLANES - 1) // SUBLANES * SUBLANES
    experts_ = jnp.arange(N_EXPERTS, dtype=jnp.int32)
    total = jnp.sum(counts, axis=0)
    owned = (total + MOE_TOKEN_TILE + tm - 1) // tm
    owned_end = jnp.cumsum(owned)
    region = (owned_end - owned) * tm
    base = region[None, :] + jnp.cumsum(counts, axis=0) - counts
    local = jnp.cumsum(counts, axis=1) - counts
    max_rows = 2 * n_tokens + (SUBLANES - 1) * N_EXPERTS * counts.shape[0] + N_EXPERTS * MOE_TOKEN_TILE
    n_tiles = -(-max_rows // tm) + N_EXPERTS + 1
    tile_id = jnp.arange(n_tiles, dtype=jnp.int32)
    tile_expert = jnp.minimum(jnp.sum(tile_id[:, None] >= owned_end[None, :], axis=1), N_EXPERTS - 1)
    pick = tile_expert[:, None] == experts_[None, :]
    rows_before = (tile_id - jnp.sum(jnp.where(pick, (owned_end - owned)[None, :], 0), axis=1)) * tm
    has_rows = (tile_id < owned_end[-1]) & (rows_before < jnp.sum(jnp.where(pick, total[None, :], 0), axis=1))
    return dict(n_tiles=n_tiles, counts=counts.reshape(-1), base=base.reshape(-1).astype(jnp.int32),
                local=local.reshape(-1).astype(jnp.int32), tail=(region + total).astype(jnp.int32),
                tile_expert=tile_expert.astype(jnp.int32), tile_valid=has_rows.astype(jnp.int32))


def _run_windows(n):
    return (((n > 0) & (n <= MOE_DISPATCH_WINDOW), MOE_DISPATCH_WINDOW),
            (n > MOE_DISPATCH_WINDOW, MOE_TOKEN_TILE))


def _dispatch_kernel(cnt_ref, local_ref, base_ref, tail_ref, valid_ref, x_ref, g_ref, meta_ref, xs_ref,
                     stage_ref, sem):
    i = pl.program_id(0)
    tt, d = x_ref.shape
    n_sorted = 2 * tt + SUBLANES * N_EXPERTS
    n_zero = stage_ref.shape[1] - n_sorted
    tm = MOE_ROW_TILE

    @pl.when(i == 0)
    def _():
        for side in range(2):
            stage_ref[side, n_sorted:, :] = jnp.zeros((n_zero, stage_ref.shape[2]), F32)

        def zero_copy(row, rows):
            return pltpu.make_async_copy(stage_ref.at[0, pl.ds(n_sorted, rows)],
                                         xs_ref.at[pl.ds(pl.multiple_of(row, SUBLANES), rows)], sem)

        pieces = [n_zero >> b for b in range((n_zero // SUBLANES).bit_length())]
        for action in ("start", "wait"):
            def empty_tile(k, carry, action=action):
                @pl.when(valid_ref[k] == 0)
                def _():
                    for part in range(tm // n_zero):
                        getattr(zero_copy(k * tm + part * n_zero, n_zero), action)()
                return carry

            lax.fori_loop(0, valid_ref.shape[0], empty_tile, 0)
            for e in range(N_EXPERTS):
                row = tail_ref[e]
                left = (-row) & (tm - 1)
                for rows in pieces:
                    @pl.when((left & rows) != 0)
                    def _(row=row, rows=rows):
                        getattr(zero_copy(row, rows), action)()
                    row = row + (left & rows)

    h = _rms(x_ref[...], g_ref[...]).astype(BF16)
    meta = meta_ref[...]
    e1, e2, r1, r2, g1, g2 = (meta[f:f + 1, :] for f in range(6))
    off1 = jnp.zeros_like(r1)
    off2 = jnp.zeros_like(r2)
    for e in range(N_EXPERTS):
        lo = local_ref[i * N_EXPERTS + e].astype(F32)
        off1 = jnp.where(e1 == e, lo, off1)
        off2 = jnp.where(e2 == e, lo, off2)
    slot = lax.broadcasted_iota(jnp.int32, (n_sorted, tt), 0).astype(F32)
    hit1 = slot == r1 + off1
    hit2 = slot == r2 + off2
    perm = jnp.where(hit1 | hit2, 1.0, 0.0).astype(BF16)
    half = i % 2
    stage_ref[half, 0:n_sorted, 0:d] = jnp.dot(perm, h, preferred_element_type=F32)
    gate = jnp.sum(jnp.where(hit1, g1, 0.0) + jnp.where(hit2, g2, 0.0), axis=-1, keepdims=True)
    stage_ref[half, 0:n_sorted, d:] = jnp.broadcast_to(gate, (n_sorted, LANE))

    def run_copies(step, action):
        for e in range(N_EXPERTS):
            k = step * N_EXPERTS + e
            for cond, rows in _run_windows(cnt_ref[k]):
                @pl.when(cond)
                def _(k=k, rows=rows):
                    copy = pltpu.make_async_copy(
                        stage_ref.at[step % 2, pl.ds(pl.multiple_of(local_ref[k], SUBLANES), rows)],
                        xs_ref.at[pl.ds(pl.multiple_of(base_ref[k], SUBLANES), rows)], sem)
                    getattr(copy, action)()

    @pl.when(i > 0)
    def _():
        run_copies(i - 1, "wait")

    run_copies(i, "start")

    @pl.when(i == pl.num_programs(0) - 1)
    def _():
        run_copies(i, "wait")


def dispatch(x2, g, meta_row, plan):
    t, d = x2.shape
    tt = MOE_TOKEN_TILE
    grid_spec = pltpu.PrefetchScalarGridSpec(
        num_scalar_prefetch=5,
        grid=(t // tt,),
        in_specs=[pl.BlockSpec((tt, d), lambda i, *_: (i, 0)),
                  pl.BlockSpec((1, d), lambda i, *_: (0, 0)),
                  pl.BlockSpec((None, META_FIELDS, tt), lambda i, *_: (i, 0, 0))],
        out_specs=pl.BlockSpec(memory_space=pl.ANY),
        scratch_shapes=[pltpu.VMEM((2, 2 * tt + SUBLANES * N_EXPERTS + MOE_TOKEN_TILE, d + LANE), F32),
                        pltpu.SemaphoreType.DMA(())],
    )
    return pl.pallas_call(
        _dispatch_kernel,
        out_shape=jax.ShapeDtypeStruct((plan["n_tiles"] * MOE_ROW_TILE, d + LANE), F32),
        grid_spec=grid_spec,
        compiler_params=_params("arbitrary"),
        name="moe_dispatch",
    )(plan["counts"], plan["local"], plan["base"], plan["tail"], plan["tile_valid"], x2, g, meta_row)


def _experts_kernel(te_ref, tv_ref, x_ref, wg_ref, wu_ref, wd_ref, o_ref):
    i = pl.program_id(0)
    d = o_ref.shape[-1]
    f = wg_ref.shape[-1]

    @pl.when(tv_ref[i] == 0)
    def _():
        o_ref[...] = jnp.zeros(o_ref.shape, o_ref.dtype)

    @pl.when(tv_ref[i] > 0)
    def _():
        xb = x_ref[:, :d].astype(BF16)
        acc = None
        for c0 in range(0, f, MOE_FF_CHUNK):
            cols = slice(c0, min(c0 + MOE_FF_CHUNK, f))
            a = jnp.dot(xb, wg_ref[:, cols], preferred_element_type=F32)
            u = jnp.dot(xb, wu_ref[:, cols], preferred_element_type=F32)
            act = (a * jax.nn.sigmoid(a) * u).astype(BF16)
            part = jnp.dot(act, wd_ref[cols, :], preferred_element_type=F32)
            acc = part if acc is None else acc + part
        o_ref[...] = acc * x_ref[:, d:d + 1]


def experts(xs, plan, w_gate, w_up, w_down):
    n_rows, dw = xs.shape
    d = dw - LANE
    f = w_gate.shape[-1]
    tm = MOE_ROW_TILE
    weights = lambda shape: pl.BlockSpec((None,) + shape, lambda i, te, tv: (te[i], 0, 0),
                                         pipeline_mode=pl.Buffered(1))
    grid_spec = pltpu.PrefetchScalarGridSpec(
        num_scalar_prefetch=2,
        grid=(n_rows // tm,),
        in_specs=[pl.BlockSpec((tm, dw), lambda i, te, tv: (i, 0)),
                  weights((d, f)), weights((d, f)), weights((f, d))],
        out_specs=pl.BlockSpec((tm, d), lambda i, te, tv: (i, 0)),
    )
    return pl.pallas_call(
        _experts_kernel,
        out_shape=jax.ShapeDtypeStruct((n_rows, d), F32),
        grid_spec=grid_spec,
        compiler_params=_params("arbitrary"),
        name="moe_experts",
    )(plan["tile_expert"], plan["tile_valid"], xs, w_gate, w_up, w_down)


def _combine_kernel(cnt_ref, base_ref, x_ref, meta_ref, ys_ref, o_ref, small_ref, wide_ref, sems):
    i = pl.program_id(0)
    tt = x_ref.shape[0]
    meta = meta_ref[...]
    e1, e2, r1, r2 = (meta[:, f:f + 1] for f in range(4))
    widest = cnt_ref[i * N_EXPERTS]
    for e in range(1, N_EXPERTS):
        widest = jnp.maximum(widest, cnt_ref[i * N_EXPERTS + e])

    def small_copies(step):
        return [pltpu.make_async_copy(
            ys_ref.at[pl.ds(pl.multiple_of(base_ref[step * N_EXPERTS + e], SUBLANES), MOE_SMALL_WINDOW)],
            small_ref.at[step % 2, e], sems.at[step % 2]) for e in range(N_EXPERTS)]

    @pl.when(i == 0)
    def _():
        for c in small_copies(i):
            c.start()

    @pl.when(i + 1 < pl.num_programs(0))
    def _():
        for c in small_copies(i + 1):
            c.start()

    for c in small_copies(i):
        c.wait()

    def pick_rows(window_of):
        rows = window_of(0).shape[0]
        pos = lax.broadcasted_iota(jnp.int32, (tt, rows), 1).astype(F32)
        acc = x_ref[...]
        for e in range(N_EXPERTS):
            want1 = jnp.where(e1 == e, r1, -1.0)
            want2 = jnp.where(e2 == e, r2, -1.0)
            pick = jnp.where((pos == want1) | (pos == want2), 1.0, 0.0).astype(BF16)
            acc = acc + jnp.dot(pick, window_of(e).astype(BF16), preferred_element_type=F32)
        o_ref[...] = acc

    @pl.when(widest <= MOE_SMALL_WINDOW)
    def _():
        pick_rows(lambda e: small_ref[i % 2, e])

    @pl.when(widest > MOE_SMALL_WINDOW)
    def _():
        copies = [pltpu.make_async_copy(
            ys_ref.at[pl.ds(pl.multiple_of(base_ref[i * N_EXPERTS + e], SUBLANES), tt)],
            wide_ref.at[e], sems.at[2]) for e in range(N_EXPERTS)]
        for c in copies:
            c.start()
        for c in copies:
            c.wait()
        pick_rows(lambda e: wide_ref[e])


def combine(x2, meta_col, ys, plan):
    t, d = x2.shape
    tt = MOE_TOKEN_TILE
    grid_spec = pltpu.PrefetchScalarGridSpec(
        num_scalar_prefetch=2,
        grid=(t // tt,),
        in_specs=[pl.BlockSpec((tt, d), lambda i, *_: (i, 0)),
                  pl.BlockSpec((tt, META_FIELDS), lambda i, *_: (i, 0)),
                  pl.BlockSpec(memory_space=pl.ANY)],
        out_specs=pl.BlockSpec((tt, d), lambda i, *_: (i, 0)),
        scratch_shapes=[pltpu.VMEM((2, N_EXPERTS, MOE_SMALL_WINDOW, d), F32),
                        pltpu.VMEM((N_EXPERTS, tt, d), F32), pltpu.SemaphoreType.DMA((3,))],
    )
    return pl.pallas_call(
        _combine_kernel,
        out_shape=jax.ShapeDtypeStruct((t, d), F32),
        grid_spec=grid_spec,
        compiler_params=_params("arbitrary"),
        name="moe_combine",
    )(plan["counts"], plan["base"], x2, meta_col, ys)


def _pad_cols(w, mult):
    pad = (-w.shape[-1]) % mult
    return jnp.pad(w, ((0, 0),) * (w.ndim - 1) + ((0, pad),))


def kernel(x, rel_bias, conv_norm, conv_w_in, conv_b_in, conv_dw_w, conv_dw_b, conv_ln_g, conv_ln_b, conv_w_out, conv_b_out, kv_norm, w_kv, k_norm, attn_norm, w_q, q_norm, w_o, ffn_norm, ffn_w_gate, ffn_w_up, ffn_w_down, moe_router, moe_w_gate, moe_w_up, moe_w_down):
    b, s, d = x.shape
    t = b * s
    row = lambda v: v.reshape(1, -1)

    x2 = x.reshape(t, d)
    u = conv_in(x2, row(conv_norm[0]), conv_w_in[0].astype(BF16), row(conv_b_in[0]))
    x3 = conv_out(u.reshape(b, s, d), x, conv_dw_w[0], row(conv_dw_b[0]), row(conv_ln_g[0]),
                  row(conv_ln_b[0]), conv_w_out[0].astype(BF16), row(conv_b_out[0]))
    x2 = x3.reshape(t, d)
    wg = _pad_cols(ffn_w_gate[0], LANE).astype(BF16)
    wu = _pad_cols(ffn_w_up[0], LANE).astype(BF16)
    wd = jnp.pad(ffn_w_down[0], ((0, wg.shape[1] - ffn_w_down.shape[1]), (0, 0))).astype(BF16)
    x2 = ffn_dense(x2, row(ffn_norm[0]), wg, wu, wd)

    qn = row(jnp.tile(q_norm[0] * (HEAD_DIM ** -0.5), (1, N_HEADS)))
    kn = row(jnp.tile(k_norm, (1, N_HEADS)))
    x3 = x2.reshape(b, s, d)
    qkv = qkv_proj(x3, row(attn_norm[0]), row(kv_norm), w_q[0].astype(BF16), w_kv.astype(BF16), qn, kn)
    outs, lses = [], []
    for g, (_, dil) in enumerate(DILATED_GROUPS):
        qr, kr, vr = (a.reshape(b * dil, s // dil, a.shape[-1]) for a in qkv[3 * g:3 * g + 3])
        o_g, lse_g = dilated_attention_group(qr, kr, vr, band_bias(rel_bias, g))
        outs.append(o_g.reshape(b, dil, s // dil, o_g.shape[-1]))
        lses.append(lse_g.reshape(b, dil, s // dil, LANE))
    x2 = attn_out(x3, outs, lses, w_o[0].astype(BF16)).reshape(t, d)

    g_moe = row(ffn_norm[1])
    meta_col, meta_row, counts = router(x2, g_moe, moe_router[0])
    plan = moe_plan(counts[:, 0, :N_EXPERTS], t)
    xs = dispatch(x2, g_moe, meta_row, plan)
    ys = experts(xs, plan, moe_w_gate[0].astype(BF16), moe_w_up[0].astype(BF16),
                 moe_w_down[0].astype(BF16))
    return combine(x2, meta_col, ys, plan).reshape(b, s, d)
```

```python
import functools

import numpy as np
import jax
import jax.numpy as jnp
from jax import lax
from jax.experimental import pallas as pl
from jax.experimental.pallas import tpu as pltpu

F32 = jnp.float32
BF16 = jnp.bfloat16

EPS = 1e-6
NEG = -1e30
CONV_WIDTH = 31
CONV_HALO = 32
N_HEADS = 16
HEAD_DIM = 64
DILATED_GROUPS = ((128, 1), (512, 4), (2048, 16))
BLK = 128
ATTN_BLOCKS_PER_STEP = 4
N_BUCKETS = 32
MAX_DISTANCE = 2048
N_EXPERTS = 8
LANE = 128
SUBLANES = 8
VMEM_LIMIT_BYTES = 56 * 1024 * 1024

ROW_TILE = 512
CONV_CHUNK = 64
CONV_SPAN = CONV_CHUNK + SUBLANES * ((CONV_WIDTH - 1) // SUBLANES)
CONV_WINDOW = 128
MOE_ROW_TILE = 1024
MOE_FF_CHUNK = 1024
MOE_TOKEN_TILE = 512
MOE_DISPATCH_WINDOW = 192
MOE_SMALL_WINDOW = 256
META_FIELDS = 8


def _params(*sem):
    return pltpu.CompilerParams(dimension_semantics=sem, vmem_limit_bytes=VMEM_LIMIT_BYTES)


def _resident(shape):
    nd = len(shape)
    return pl.BlockSpec(shape, lambda *_: (0,) * nd, pipeline_mode=pl.Buffered(1))


def _rms(x, g):
    return x * lax.rsqrt(jnp.mean(x * x, axis=-1, keepdims=True) + EPS) * g


class _WeightCast:
    def __init__(self, weights, pads, steps, index):
        self.shapes = [w.shape for w in weights]
        self.pads = pads
        self.inputs, self.in_specs, self.out_shape, self.out_specs = [], [], [], []
        for w, pad in zip(weights, pads):
            rows, cols = w.shape
            assert rows % steps == 0, (rows, steps)
            block = lambda *grid: (index(*grid), 0, 0)
            self.inputs.append(w.reshape(steps, rows // steps, cols))
            self.in_specs.append(pl.BlockSpec((None, rows // steps, cols), block))
            self.out_shape.append(jax.ShapeDtypeStruct((steps, rows // steps, cols + pad), BF16))
            self.out_specs.append(pl.BlockSpec((None, rows // steps, cols + pad), block))

    @staticmethod
    def run(src_refs, dst_refs):
        for src, dst in zip(src_refs, dst_refs):
            cols = src.shape[-1]
            dst[:, :cols] = src[...].astype(dst.dtype)
            if dst.shape[-1] > cols:
                dst[:, cols:] = jnp.zeros((dst.shape[0], dst.shape[-1] - cols), dst.dtype)

    def results(self, outs):
        return [o.reshape(rows, cols + pad) for o, (rows, cols), pad in zip(outs, self.shapes, self.pads)]


def _conv_in_kernel(x_ref, g_ref, w_ref, b_ref, *refs):
    n_cast = (len(refs) - 1) // 2
    u_ref = refs[n_cast]
    h = _rms(x_ref[...], g_ref[...])
    y = jnp.dot(h.astype(BF16), w_ref[...], preferred_element_type=F32) + b_ref[...]
    d = u_ref.shape[-1]
    u_ref[...] = (y[:, :d] * jax.nn.sigmoid(y[:, d:])).astype(u_ref.dtype)
    _WeightCast.run(refs[:n_cast], refs[n_cast + 1:])


def conv_in(x2, g, w_in, b_in, later_weights, pads):
    t, d = x2.shape
    steps = t // ROW_TILE
    cast = _WeightCast(later_weights, pads, steps, lambda i: i)
    outs = pl.pallas_call(
        _conv_in_kernel,
        out_shape=(jax.ShapeDtypeStruct((t, d), BF16), *cast.out_shape),
        grid=(steps,),
        in_specs=[pl.BlockSpec((ROW_TILE, d), lambda i: (i, 0)),
                  _resident((1, d)), _resident((d, 2 * d)), _resident((1, 2 * d)), *cast.in_specs],
        out_specs=(pl.BlockSpec((ROW_TILE, d), lambda i: (i, 0)), *cast.out_specs),
        compiler_params=_params("parallel"),
        name="conv_in",
    )(x2, g, w_in, b_in, *cast.inputs)
    return outs[0], cast.results(outs[1:])


def _conv_out_kernel(u_ref, x_ref, dww_ref, dwb_ref, lng_ref, lnb_ref, w_ref, b_ref, shift_ref, *refs):
    n_cast = (len(refs) - 3) // 2
    o_ref = refs[n_cast]
    buf_ref, conv_ref = refs[-2:]
    _WeightCast.run(refs[:n_cast], refs[n_cast + 1:-2])
    ts, d = x_ref.shape
    j = pl.program_id(1)

    @pl.when(j == 0)
    def _():
        buf_ref[0:CONV_HALO, :] = jnp.zeros((CONV_HALO, d), buf_ref.dtype)
        buf_ref[CONV_HALO + ts:, :] = jnp.zeros((buf_ref.shape[0] - CONV_HALO - ts, d), buf_ref.dtype)

    @pl.when(j > 0)
    def _():
        buf_ref[0:CONV_HALO, :] = buf_ref[ts:ts + CONV_HALO, :]

    buf_ref[CONV_HALO:CONV_HALO + ts, :] = u_ref[...]

    def lane_chunk(c, carry):
        lanes = pl.ds(pl.multiple_of(c * LANE, LANE), LANE)
        taps = [dww_ref[k:k + 1, lanes] for k in range(CONV_WIDTH)]
        bias = dwb_ref[:, lanes]
        for r0 in range(0, ts, CONV_CHUNK):
            shifted = jnp.dot(shift_ref[...], buf_ref[r0:r0 + CONV_WINDOW, lanes],
                              preferred_element_type=F32)
            acc = jnp.broadcast_to(bias, (CONV_CHUNK, LANE))
            for phase in range(SUBLANES):
                for a, k in enumerate(range(phase, CONV_WIDTH, SUBLANES)):
                    lo = phase * CONV_SPAN + SUBLANES * a
                    acc = acc + taps[k] * shifted[lo:lo + CONV_CHUNK]
            conv_ref[r0:r0 + CONV_CHUNK, lanes] = acc
        return carry

    lax.fori_loop(0, d // LANE, lane_chunk, 0)
    acc = conv_ref[...]
    mu = jnp.mean(acc, axis=-1, keepdims=True)
    cen = acc - mu
    var = jnp.mean(cen * cen, axis=-1, keepdims=True)
    y = cen * lax.rsqrt(var + EPS) * lng_ref[...] + lnb_ref[...]
    act = (y * jax.nn.sigmoid(y)).astype(BF16)
    o_ref[...] = x_ref[...] + b_ref[...] + jnp.dot(act, w_ref[...], preferred_element_type=F32)


def conv_out(u3, x3, dw_w, dw_b, ln_g, ln_b, w_out, b_out, later_weights):
    b, s, d = x3.shape
    ts = ROW_TILE
    nj = s // ts
    cast = _WeightCast(later_weights, [0] * len(later_weights), b * nj, lambda bi, j: bi * nj + j)
    row = pl.BlockSpec((None, ts, d), lambda bi, j: (bi, j, 0))
    first_tap = CONV_HALO - (CONV_WIDTH - 1)
    shift = np.zeros((SUBLANES * CONV_SPAN, CONV_WINDOW), np.float32)
    for p in range(SUBLANES):
        i = np.arange(CONV_SPAN)
        shift[p * CONV_SPAN + i, first_tap + p + i] = 1.0
    outs = pl.pallas_call(
        _conv_out_kernel,
        out_shape=(jax.ShapeDtypeStruct((b, s, d), F32), *cast.out_shape),
        grid=(b, nj),
        in_specs=[row, row, _resident((CONV_WIDTH, d)), _resident((1, d)), _resident((1, d)),
                  _resident((1, d)), _resident((d, d)), _resident((1, d)), _resident(shift.shape),
                  *cast.in_specs],
        out_specs=(row, *cast.out_specs),
        scratch_shapes=[pltpu.VMEM((ts - CONV_CHUNK + CONV_WINDOW, d), BF16), pltpu.VMEM((ts, d), F32)],
        compiler_params=_params("parallel", "arbitrary"),
        name="conv_out",
    )(u3, x3, dw_w, dw_b, ln_g, ln_b, w_out, b_out, jnp.asarray(shift, BF16), *cast.inputs)
    return outs[0], cast.results(outs[1:])


def _ffn_kernel(x_ref, g_ref, wg_ref, wu_ref, wd_ref, o_ref):
    x = x_ref[...]
    h = _rms(x, g_ref[...]).astype(BF16)
    a = jnp.dot(h, wg_ref[...], preferred_element_type=F32)
    u = jnp.dot(h, wu_ref[...], preferred_element_type=F32)
    act = (a * jax.nn.sigmoid(a) * u).astype(BF16)
    o_ref[...] = x + jnp.dot(act, wd_ref[...], preferred_element_type=F32)


def ffn_dense(x2, g, w_gate, w_up, w_down):
    t, d = x2.shape
    f = w_gate.shape[1]
    row = pl.BlockSpec((ROW_TILE, d), lambda i: (i, 0))
    return pl.pallas_call(
        _ffn_kernel,
        out_shape=jax.ShapeDtypeStruct((t, d), F32),
        grid=(t // ROW_TILE,),
        in_specs=[row, _resident((1, d)), _resident((d, f)), _resident((d, f)), _resident((f, d))],
        out_specs=row,
        compiler_params=_params("parallel"),
        name="ffn_dense",
    )(x2, g, w_gate, w_up, w_down)


def _store_heads(y, out_ref, gain_ref=None, gain_off=0):
    dil, n, _ = out_ref.shape
    low = lax.broadcasted_iota(jnp.int32, (1, LANE), 1) < HEAD_DIM
    for c in range(y.shape[-1] // LANE):
        cols = slice(c * LANE, (c + 1) * LANE)
        blk = y[:, cols]
        if gain_ref is not None:
            sq = blk * blk
            ms_lo = jnp.sum(jnp.where(low, sq, 0.0), axis=-1, keepdims=True) * (1.0 / HEAD_DIM)
            ms_hi = jnp.sum(jnp.where(low, 0.0, sq), axis=-1, keepdims=True) * (1.0 / HEAD_DIM)
            inv = jnp.where(low, lax.rsqrt(ms_lo + EPS), lax.rsqrt(ms_hi + EPS))
            blk = blk * inv * gain_ref[:, gain_off + c * LANE:gain_off + (c + 1) * LANE]
        blk = blk.astype(out_ref.dtype)
        for r in range(dil):
            out_ref[r, :, cols] = blk[r * n:(r + 1) * n]


def _qkv_kernel(x_ref, gq_ref, gkv_ref, wq_ref, wkv_ref, qn_ref, kn_ref, *refs):
    out_refs, xn_ref = refs[:-1], refs[-1]
    x = x_ref[...]
    xn = x * lax.rsqrt(jnp.mean(x * x, axis=-1, keepdims=True) + EPS)
    n_cols = xn_ref.shape[0]
    for c in range(n_cols):
        xn_ref[c] = xn[:, c * LANE:(c + 1) * LANE]
    tm = x.shape[0]
    hw = N_HEADS * HEAD_DIM
    qw = len(DILATED_GROUPS) * hw
    for g, (_, dil) in enumerate(DILATED_GROUPS):
        n = tm // dil
        xg = xn if dil == 1 else jnp.concatenate(
            [jnp.concatenate([xn_ref[c, pl.ds(r, n, stride=dil), :] for r in range(dil)], axis=0)
             for c in range(n_cols)], axis=1)
        hq = (xg * gq_ref[...]).astype(BF16)
        hkv = (xg * gkv_ref[...]).astype(BF16)
        q_ref, k_ref, v_ref = out_refs[3 * g:3 * g + 3]
        cols = slice(g * hw, (g + 1) * hw)
        _store_heads(jnp.dot(hq, wq_ref[:, cols], preferred_element_type=F32), q_ref, qn_ref, g * hw)
        _store_heads(jnp.dot(hkv, wkv_ref[:, cols], preferred_element_type=F32), k_ref, kn_ref, g * hw)
        _store_heads(jnp.dot(hkv, wkv_ref[:, qw + g * hw:qw + (g + 1) * hw],
                             preferred_element_type=F32), v_ref)


def qkv_proj(x3, gq, gkv, w_q, w_kv, qn, kn):
    b, s, d = x3.shape
    qw = w_q.shape[1]
    hw = N_HEADS * HEAD_DIM
    tm = ROW_TILE
    out_shape, out_specs = [], []
    for _, dil in DILATED_GROUPS:
        for _ in range(3):
            out_shape.append(jax.ShapeDtypeStruct((b, dil, s // dil, hw), BF16))
            out_specs.append(pl.BlockSpec((None, dil, tm // dil, hw), lambda bi, j: (bi, 0, j, 0)))
    return pl.pallas_call(
        _qkv_kernel,
        out_shape=tuple(out_shape),
        grid=(b, s // tm),
        in_specs=[pl.BlockSpec((None, tm, d), lambda bi, j: (bi, j, 0)), _resident((1, d)),
                  _resident((1, d)), _resident((d, qw)), _resident((d, 2 * qw)), _resident((1, qw)),
                  _resident((1, qw))],
        out_specs=tuple(out_specs),
        scratch_shapes=[pltpu.VMEM((d // LANE, tm, LANE), F32)],
        compiler_params=_params("parallel", "parallel"),
        name="qkv_proj",
    )(x3, gq, gkv, w_q, w_kv, qn, kn)


def _t5_bucket(dist):
    n = np.asarray(dist)
    max_exact = N_BUCKETS // 2
    large = max_exact + (np.log(np.maximum(n, 1) / max_exact)
                         / np.log(MAX_DISTANCE / max_exact)
                         * (N_BUCKETS - max_exact)).astype(np.int32)
    large = np.minimum(large, N_BUCKETS - 1)
    return np.where(n < max_exact, n, large).astype(np.int32)


def band_bias(rel_bias, g):
    window, d = DILATED_GROUPS[g]
    span = window // d
    period = 3 * BLK
    diff = np.arange(period)
    diff = np.where(diff >= 2 * BLK, diff - period, diff)
    sub = BLK - diff
    in_band = (sub >= 0) & (sub <= span)
    bucket = _t5_bucket(np.clip(sub, 0, span) * d)
    table = rel_bias[:, g * N_HEADS:(g + 1) * N_HEADS].astype(F32)
    onehot = jnp.asarray(bucket[:, None] == np.arange(N_BUCKETS)[None, :], F32)
    line = jnp.dot(onehot, table, precision=lax.Precision.HIGHEST).T
    line = jnp.where(jnp.asarray(in_band)[None, :], line, NEG)
    flat = jnp.tile(line, (1, BLK))[:, :BLK * (period - 1)]
    general = flat.reshape(N_HEADS, BLK, period - 1)[:, :, :2 * BLK]
    has_prev = jnp.asarray(np.arange(2 * BLK) >= BLK)[None, None, :]
    first = jnp.where(has_prev, general, NEG)
    return jnp.stack([first, general])


def _attn_kernel(q_ref, k_ref, v_ref, bias_ref, hm_ref, o_ref, lse_ref,
                 kprev_ref, vprev_ref, s_ref, p_ref, m_ref):
    i = pl.program_id(1)

    @pl.when(i == 0)
    def _():
        kprev_ref[...] = jnp.zeros(kprev_ref.shape, kprev_ref.dtype)
        vprev_ref[...] = jnp.zeros(vprev_ref.shape, vprev_ref.dtype)

    n_cols = q_ref.shape[-1] // LANE
    lane = lax.broadcasted_iota(jnp.int32, (BLK, LANE), 1)
    low = lane < HEAD_DIM
    ones = jnp.ones((2 * BLK, LANE), BF16)

    for b in range(q_ref.shape[0] // BLK):
        rows = slice(b * BLK, (b + 1) * BLK)
        before = slice((b - 1) * BLK, b * BLK)
        variant = jnp.minimum(i, 1) if b == 0 else 1

        for c in range(n_cols):
            cols = slice(c * LANE, (c + 1) * LANE)
            q2 = q_ref[rows, cols]
            kcat = jnp.concatenate([kprev_ref[:, cols] if b == 0 else k_ref[before, cols],
                                    k_ref[rows, cols]], axis=0)
            for half in range(2):
                h = 2 * c + half
                s = lax.dot_general(q2 * hm_ref[half:half + 1, :], kcat, (((1,), (1,)), ((), ())),
                                    preferred_element_type=F32) + bias_ref[variant, h]
                s_ref[b, h] = s
                m_ref[b, h] = jnp.broadcast_to(jnp.max(s, axis=-1, keepdims=True), (BLK, LANE))

        for h in range(N_HEADS):
            m = m_ref[b, h]
            p_ref[b, h] = jnp.exp(s_ref[b, h] - jnp.concatenate([m, m], axis=-1)).astype(p_ref.dtype)

        m_tile = jnp.zeros((BLK, LANE), F32)
        l_tile = jnp.ones((BLK, LANE), F32)
        for c in range(n_cols):
            cols = slice(c * LANE, (c + 1) * LANE)
            vcat = jnp.concatenate([vprev_ref[:, cols] if b == 0 else v_ref[before, cols],
                                    v_ref[rows, cols]], axis=0)
            acc = []
            for half in range(2):
                rhs = jnp.concatenate([vcat * hm_ref[half:half + 1, :], ones], axis=-1)
                acc.append(jnp.dot(p_ref[b, 2 * c + half], rhs, preferred_element_type=F32))
            den = jnp.where(low, acc[0][:, LANE:], acc[1][:, LANE:])
            o_ref[rows, cols] = ((acc[0][:, :LANE] + acc[1][:, :LANE]) / den).astype(o_ref.dtype)
            for half in range(2):
                h = 2 * c + half
                m_tile = jnp.where(lane == h, m_ref[b, h], m_tile)
                l_tile = jnp.where(lane == h, acc[half][:, LANE:], l_tile)
        lse_ref[rows, :] = m_tile + jnp.log(l_tile)

    last = slice(q_ref.shape[0] - BLK, q_ref.shape[0])
    kprev_ref[...] = k_ref[last, :]
    vprev_ref[...] = v_ref[last, :]


def dilated_attention_group(q, k, v, bias):
    r, l, w = q.shape
    nb = min(ATTN_BLOCKS_PER_STEP, l // BLK)
    rows = nb * BLK
    assert l % rows == 0, (l, rows)
    blk = pl.BlockSpec((None, rows, w), lambda a, i: (a, i, 0))
    lane = np.arange(LANE)[None, :] < HEAD_DIM
    head_mask = jnp.asarray(np.concatenate([lane, ~lane]).astype(np.float32), BF16)
    return pl.pallas_call(
        _attn_kernel,
        out_shape=(jax.ShapeDtypeStruct((r, l, w), BF16),
                   jax.ShapeDtypeStruct((r, l, LANE), F32)),
        grid=(r, l // rows),
        in_specs=[blk, blk, blk, _resident(bias.shape), _resident((2, LANE))],
        out_specs=(blk, pl.BlockSpec((None, rows, LANE), lambda a, i: (a, i, 0))),
        scratch_shapes=[pltpu.VMEM((BLK, w), BF16), pltpu.VMEM((BLK, w), BF16),
                        pltpu.VMEM((nb, N_HEADS, BLK, 2 * BLK), F32),
                        pltpu.VMEM((nb, N_HEADS, BLK, 2 * BLK), BF16),
                        pltpu.VMEM((nb, N_HEADS, BLK, LANE), F32)],
        compiler_params=_params("parallel", "arbitrary"),
        name="dilated_attn",
    )(q, k, v, bias, head_mask)


def _attn_out_kernel(x_ref, o0_ref, o1_ref, o2_ref, l0_ref, l1_ref, l2_ref, e_ref, w_ref, o_ref,
                     on_ref, ln_ref):
    n_cols = on_ref.shape[1]
    for gi, (og_ref, lg_ref) in enumerate(((o1_ref, l1_ref), (o2_ref, l2_ref))):
        dil, n, _ = og_ref.shape
        for r in range(dil):
            ln_ref[gi, pl.ds(r, n, stride=dil), :] = lg_ref[r]
            for c in range(n_cols):
                on_ref[gi, c, pl.ds(r, n, stride=dil), :] = (
                    og_ref[r, :, c * LANE:(c + 1) * LANE].astype(F32))
    l0, l1, l2 = l0_ref[0], ln_ref[0], ln_ref[1]
    m = jnp.maximum(jnp.maximum(l0, l1), l2)
    e0, e1, e2 = jnp.exp(l0 - m), jnp.exp(l1 - m), jnp.exp(l2 - m)
    den = e0 + e1 + e2
    acc = None
    token_order = [jnp.concatenate([on_ref[gi, c] for c in range(n_cols)], axis=1) for gi in range(2)]
    for e, o in ((e0, o0_ref[0].astype(F32)), (e1, token_order[0]), (e2, token_order[1])):
        wgt = jnp.dot((e / den).astype(BF16), e_ref[...], preferred_element_type=F32)
        term = wgt * o
        acc = term if acc is None else acc + term
    o_ref[...] = x_ref[...] + jnp.dot(acc.astype(BF16), w_ref[...], preferred_element_type=F32)


def attn_out(x3, outs, lses, w_o):
    b, s, d = x3.shape
    tm = ROW_TILE
    expand = np.zeros((LANE, d), np.float32)
    expand[:N_HEADS] = np.repeat(np.eye(N_HEADS, dtype=np.float32), HEAD_DIM, axis=1)
    expand = jnp.asarray(expand, BF16)
    row = pl.BlockSpec((None, tm, d), lambda bi, j: (bi, j, 0))
    o_specs = [pl.BlockSpec((None, dil, tm // dil, d), lambda bi, j: (bi, 0, j, 0))
               for _, dil in DILATED_GROUPS]
    l_specs = [pl.BlockSpec((None, dil, tm // dil, LANE), lambda bi, j: (bi, 0, j, 0))
               for _, dil in DILATED_GROUPS]
    return pl.pallas_call(
        _attn_out_kernel,
        out_shape=jax.ShapeDtypeStruct((b, s, d), F32),
        grid=(b, s // tm),
        in_specs=[row, *o_specs, *l_specs, _resident((LANE, d)), _resident((d, d))],
        out_specs=row,
        scratch_shapes=[pltpu.VMEM((2, d // LANE, tm, LANE), F32), pltpu.VMEM((2, tm, LANE), F32)],
        compiler_params=_params("parallel", "parallel"),
        name="attn_out",
    )(x3, *outs, *lses, expand, w_o)


def _router_kernel(x_ref, g_ref, w_ref, tri_ref, col_ref, row_ref, cnt_ref):
    h = _rms(x_ref[...], g_ref[...])
    w = w_ref[...]
    h_hi = h.astype(BF16)
    h_lo = (h - h_hi.astype(F32)).astype(BF16)
    w_hi = w.astype(BF16)
    w_lo = (w - w_hi.astype(F32)).astype(BF16)
    logits = (jnp.dot(h_hi, w_hi, preferred_element_type=F32)
              + jnp.dot(h_lo, w_hi, preferred_element_type=F32)
              + jnp.dot(h_hi, w_lo, preferred_element_type=F32))
    ne = N_EXPERTS
    lane = lax.broadcasted_iota(jnp.int32, logits.shape, 1)
    logits = jnp.where(lane < ne, logits, -jnp.inf)
    m1 = jnp.max(logits, axis=-1, keepdims=True)
    i1 = jnp.min(jnp.where(logits == m1, lane, ne), axis=-1, keepdims=True)
    rest = jnp.where(lane == i1, -jnp.inf, logits)
    m2 = jnp.max(rest, axis=-1, keepdims=True)
    i2 = jnp.min(jnp.where(rest == m2, lane, ne), axis=-1, keepdims=True)
    e2 = jnp.exp(m2 - m1)
    g1 = 1.0 / (1.0 + e2)
    g2 = e2 / (1.0 + e2)
    both = (lane == i1) | (lane == i2)
    cnt = jnp.dot(tri_ref[...], jnp.where(both, 1.0, 0.0).astype(BF16), preferred_element_type=F32)
    r1 = jnp.sum(jnp.where(lane == i1, cnt, 0.0), axis=-1, keepdims=True)
    r2 = jnp.sum(jnp.where(lane == i2, cnt, 0.0), axis=-1, keepdims=True)
    cnt_ref[...] = jnp.sum(jnp.where(both, 1, 0), axis=0, keepdims=True)
    fields = (i1.astype(F32), i2.astype(F32), r1, r2, g1, g2)
    meta = jnp.zeros(logits.shape, F32)
    for f, val in enumerate(fields):
        meta = jnp.where(lane == f, val, meta)
    col_ref[...] = meta[:, :META_FIELDS]
    row_ref[...] = meta.T[:META_FIELDS, :]


def router(x2, g, w_router):
    t, d = x2.shape
    tt = MOE_TOKEN_TILE
    nt = t // tt
    tri = jnp.asarray(np.tril(np.ones((tt, tt), np.float32), -1), BF16)
    return pl.pallas_call(
        _router_kernel,
        out_shape=(jax.ShapeDtypeStruct((t, META_FIELDS), F32),
                   jax.ShapeDtypeStruct((nt, META_FIELDS, tt), F32),
                   jax.ShapeDtypeStruct((nt, 1, LANE), jnp.int32)),
        grid=(nt,),
        in_specs=[pl.BlockSpec((tt, d), lambda i: (i, 0)), _resident((1, d)),
                  _resident((d, LANE)), _resident((tt, tt))],
        out_specs=(pl.BlockSpec((tt, META_FIELDS), lambda i: (i, 0)),
                   pl.BlockSpec((None, META_FIELDS, tt), lambda i: (i, 0, 0)),
                   pl.BlockSpec((None, 1, LANE), lambda i: (i, 0, 0))),
        compiler_params=_params("parallel"),
        name="moe_router",
    )(x2, g, _pad_cols(w_router, LANE), tri)


def moe_plan(counts, n_tokens):
    tm = MOE_ROW_TILE
    counts = (counts + SUBLANES - 1) // SUBLANES * SUBLANES
    experts_ = jnp.arange(N_EXPERTS, dtype=jnp.int32)
    total = jnp.sum(counts, axis=0)
    owned = (total + MOE_TOKEN_TILE + tm - 1) // tm
    owned_end = jnp.cumsum(owned)
    region = (owned_end - owned) * tm
    base = region[None, :] + jnp.cumsum(counts, axis=0) - counts
    local = jnp.cumsum(counts, axis=1) - counts
    max_rows = 2 * n_tokens + (SUBLANES - 1) * N_EXPERTS * counts.shape[0] + N_EXPERTS * MOE_TOKEN_TILE
    n_tiles = -(-max_rows // tm) + N_EXPERTS + 1
    tile_id = jnp.arange(n_tiles, dtype=jnp.int32)
    tile_expert = jnp.minimum(jnp.sum(tile_id[:, None] >= owned_end[None, :], axis=1), N_EXPERTS - 1)
    pick = tile_expert[:, None] == experts_[None, :]
    rows_before = (tile_id - jnp.sum(jnp.where(pick, (owned_end - owned)[None, :], 0), axis=1)) * tm
    has_rows = (tile_id < owned_end[-1]) & (rows_before < jnp.sum(jnp.where(pick, total[None, :], 0), axis=1))
    return dict(n_tiles=n_tiles, counts=counts.reshape(-1), base=base.reshape(-1).astype(jnp.int32),
                local=local.reshape(-1).astype(jnp.int32), tail=(region + total).astype(jnp.int32),
                tile_expert=tile_expert.astype(jnp.int32), tile_valid=has_rows.astype(jnp.int32))


def _run_windows(n):
    return (((n > 0) & (n <= MOE_DISPATCH_WINDOW), MOE_DISPATCH_WINDOW),
            (n > MOE_DISPATCH_WINDOW, MOE_TOKEN_TILE))


def _dispatch_kernel(cnt_ref, local_ref, base_ref, tail_ref, valid_ref, x_ref, g_ref, meta_ref, xs_ref,
                     stage_ref, sem):
    i = pl.program_id(0)
    tt, d = x_ref.shape
    n_sorted = 2 * tt + SUBLANES * N_EXPERTS
    n_zero = stage_ref.shape[1] - n_sorted
    tm = MOE_ROW_TILE

    @pl.when(i == 0)
    def _():
        for side in range(2):
            stage_ref[side, n_sorted:, :] = jnp.zeros((n_zero, stage_ref.shape[2]), F32)

        def zero_copy(row, rows):
            return pltpu.make_async_copy(stage_ref.at[0, pl.ds(n_sorted, rows)],
                                         xs_ref.at[pl.ds(pl.multiple_of(row, SUBLANES), rows)], sem)

        pieces = [n_zero >> b for b in range((n_zero // SUBLANES).bit_length())]
        for action in ("start", "wait"):
            def empty_tile(k, carry, action=action):
                @pl.when(valid_ref[k] == 0)
                def _():
                    for part in range(tm // n_zero):
                        getattr(zero_copy(k * tm + part * n_zero, n_zero), action)()
                return carry

            lax.fori_loop(0, valid_ref.shape[0], empty_tile, 0)
            for e in range(N_EXPERTS):
                row = tail_ref[e]
                left = (-row) & (tm - 1)
                for rows in pieces:
                    @pl.when((left & rows) != 0)
                    def _(row=row, rows=rows):
                        getattr(zero_copy(row, rows), action)()
                    row = row + (left & rows)

    h = _rms(x_ref[...], g_ref[...]).astype(BF16)
    meta = meta_ref[...]
    e1, e2, r1, r2, g1, g2 = (meta[f:f + 1, :] for f in range(6))
    off1 = jnp.zeros_like(r1)
    off2 = jnp.zeros_like(r2)
    for e in range(N_EXPERTS):
        lo = local_ref[i * N_EXPERTS + e].astype(F32)
        off1 = jnp.where(e1 == e, lo, off1)
        off2 = jnp.where(e2 == e, lo, off2)
    slot = lax.broadcasted_iota(jnp.int32, (n_sorted, tt), 0).astype(F32)
    hit1 = slot == r1 + off1
    hit2 = slot == r2 + off2
    perm = jnp.where(hit1 | hit2, 1.0, 0.0).astype(BF16)
    half = i % 2
    stage_ref[half, 0:n_sorted, 0:d] = jnp.dot(perm, h, preferred_element_type=F32)
    gate = jnp.sum(jnp.where(hit1, g1, 0.0) + jnp.where(hit2, g2, 0.0), axis=-1, keepdims=True)
    stage_ref[half, 0:n_sorted, d:] = jnp.broadcast_to(gate, (n_sorted, LANE))

    def run_copies(step, action):
        for e in range(N_EXPERTS):
            k = step * N_EXPERTS + e
            for cond, rows in _run_windows(cnt_ref[k]):
                @pl.when(cond)
                def _(k=k, rows=rows):
                    copy = pltpu.make_async_copy(
                        stage_ref.at[step % 2, pl.ds(pl.multiple_of(local_ref[k], SUBLANES), rows)],
                        xs_ref.at[pl.ds(pl.multiple_of(base_ref[k], SUBLANES), rows)], sem)
                    getattr(copy, action)()

    @pl.when(i > 0)
    def _():
        run_copies(i - 1, "wait")

    run_copies(i, "start")

    @pl.when(i == pl.num_programs(0) - 1)
    def _():
        run_copies(i, "wait")


def dispatch(x2, g, meta_row, plan):
    t, d = x2.shape
    tt = MOE_TOKEN_TILE
    grid_spec = pltpu.PrefetchScalarGridSpec(
        num_scalar_prefetch=5,
        grid=(t // tt,),
        in_specs=[pl.BlockSpec((tt, d), lambda i, *_: (i, 0)),
                  pl.BlockSpec((1, d), lambda i, *_: (0, 0)),
                  pl.BlockSpec((None, META_FIELDS, tt), lambda i, *_: (i, 0, 0))],
        out_specs=pl.BlockSpec(memory_space=pl.ANY),
        scratch_shapes=[pltpu.VMEM((2, 2 * tt + SUBLANES * N_EXPERTS + MOE_TOKEN_TILE, d + LANE), F32),
                        pltpu.SemaphoreType.DMA(())],
    )
    return pl.pallas_call(
        _dispatch_kernel,
        out_shape=jax.ShapeDtypeStruct((plan["n_tiles"] * MOE_ROW_TILE, d + LANE), F32),
        grid_spec=grid_spec,
        compiler_params=_params("arbitrary"),
        name="moe_dispatch",
    )(plan["counts"], plan["local"], plan["base"], plan["tail"], plan["tile_valid"], x2, g, meta_row)


def _experts_kernel(te_ref, tv_ref, x_ref, wg_ref, wu_ref, wd_ref, o_ref):
    i = pl.program_id(0)
    d = o_ref.shape[-1]
    f = wg_ref.shape[-1]

    @pl.when(tv_ref[i] == 0)
    def _():
        o_ref[...] = jnp.zeros(o_ref.shape, o_ref.dtype)

    @pl.when(tv_ref[i] > 0)
    def _():
        xb = x_ref[:, :d].astype(BF16)
        acc = None
        for c0 in range(0, f, MOE_FF_CHUNK):
            cols = slice(c0, min(c0 + MOE_FF_CHUNK, f))
            a = jnp.dot(xb, wg_ref[:, cols], preferred_element_type=F32)
            u = jnp.dot(xb, wu_ref[:, cols], preferred_element_type=F32)
            act = (a * jax.nn.sigmoid(a) * u).astype(BF16)
            part = jnp.dot(act, wd_ref[cols, :], preferred_element_type=F32)
            acc = part if acc is None else acc + part
        o_ref[...] = acc * x_ref[:, d:d + 1]


def experts(xs, plan, w_gate, w_up, w_down):
    n_rows, dw = xs.shape
    d = dw - LANE
    f = w_gate.shape[-1]
    tm = MOE_ROW_TILE
    weights = lambda shape: pl.BlockSpec((None,) + shape, lambda i, te, tv: (te[i], 0, 0),
                                         pipeline_mode=pl.Buffered(1))
    grid_spec = pltpu.PrefetchScalarGridSpec(
        num_scalar_prefetch=2,
        grid=(n_rows // tm,),
        in_specs=[pl.BlockSpec((tm, dw), lambda i, te, tv: (i, 0)),
                  weights((d, f)), weights((d, f)), weights((f, d))],
        out_specs=pl.BlockSpec((tm, d), lambda i, te, tv: (i, 0)),
    )
    return pl.pallas_call(
        _experts_kernel,
        out_shape=jax.ShapeDtypeStruct((n_rows, d), F32),
        grid_spec=grid_spec,
        compiler_params=_params("arbitrary"),
        name="moe_experts",
    )(plan["tile_expert"], plan["tile_valid"], xs, w_gate, w_up, w_down)


def _combine_kernel(cnt_ref, base_ref, x_ref, meta_ref, ys_ref, o_ref, small_ref, wide_ref, sems):
    i = pl.program_id(0)
    tt = x_ref.shape[0]
    meta = meta_ref[...]
    e1, e2, r1, r2 = (meta[:, f:f + 1] for f in range(4))
    widest = cnt_ref[i * N_EXPERTS]
    for e in range(1, N_EXPERTS):
        widest = jnp.maximum(widest, cnt_ref[i * N_EXPERTS + e])

    def small_copies(step):
        return [pltpu.make_async_copy(
            ys_ref.at[pl.ds(pl.multiple_of(base_ref[step * N_EXPERTS + e], SUBLANES), MOE_SMALL_WINDOW)],
            small_ref.at[step % 2, e], sems.at[step % 2]) for e in range(N_EXPERTS)]

    @pl.when(i == 0)
    def _():
        for c in small_copies(i):
            c.start()

    @pl.when(i + 1 < pl.num_programs(0))
    def _():
        for c in small_copies(i + 1):
            c.start()

    for c in small_copies(i):
        c.wait()

    def pick_rows(window_of):
        rows = window_of(0).shape[0]
        pos = lax.broadcasted_iota(jnp.int32, (tt, rows), 1).astype(F32)
        acc = x_ref[...]
        for e in range(N_EXPERTS):
            want1 = jnp.where(e1 == e, r1, -1.0)
            want2 = jnp.where(e2 == e, r2, -1.0)
            pick = jnp.where((pos == want1) | (pos == want2), 1.0, 0.0).astype(BF16)
            acc = acc + jnp.dot(pick, window_of(e).astype(BF16), preferred_element_type=F32)
        o_ref[...] = acc

    @pl.when(widest <= MOE_SMALL_WINDOW)
    def _():
        pick_rows(lambda e: small_ref[i % 2, e])

    @pl.when(widest > MOE_SMALL_WINDOW)
    def _():
        copies = [pltpu.make_async_copy(
            ys_ref.at[pl.ds(pl.multiple_of(base_ref[i * N_EXPERTS + e], SUBLANES), tt)],
            wide_ref.at[e], sems.at[2]) for e in range(N_EXPERTS)]
        for c in copies:
            c.start()
        for c in copies:
            c.wait()
        pick_rows(lambda e: wide_ref[e])


def combine(x2, meta_col, ys, plan):
    t, d = x2.shape
    tt = MOE_TOKEN_TILE
    grid_spec = pltpu.PrefetchScalarGridSpec(
        num_scalar_prefetch=2,
        grid=(t // tt,),
        in_specs=[pl.BlockSpec((tt, d), lambda i, *_: (i, 0)),
                  pl.BlockSpec((tt, META_FIELDS), lambda i, *_: (i, 0)),
                  pl.BlockSpec(memory_space=pl.ANY)],
        out_specs=pl.BlockSpec((tt, d), lambda i, *_: (i, 0)),
        scratch_shapes=[pltpu.VMEM((2, N_EXPERTS, MOE_SMALL_WINDOW, d), F32),
                        pltpu.VMEM((N_EXPERTS, tt, d), F32), pltpu.SemaphoreType.DMA((3,))],
    )
    return pl.pallas_call(
        _combine_kernel,
        out_shape=jax.ShapeDtypeStruct((t, d), F32),
        grid_spec=grid_spec,
        compiler_params=_params("arbitrary"),
        name="moe_combine",
    )(plan["counts"], plan["base"], x2, meta_col, ys)


def _pad_cols(w, mult):
    pad = (-w.shape[-1]) % mult
    return jnp.pad(w, ((0, 0),) * (w.ndim - 1) + ((0, pad),))


def kernel(x, rel_bias, conv_norm, conv_w_in, conv_b_in, conv_dw_w, conv_dw_b, conv_ln_g, conv_ln_b, conv_w_out, conv_b_out, kv_norm, w_kv, k_norm, attn_norm, w_q, q_norm, w_o, ffn_norm, ffn_w_gate, ffn_w_up, ffn_w_down, moe_router, moe_w_gate, moe_w_up, moe_w_down):
    b, s, d = x.shape
    t = b * s
    row = lambda v: v.reshape(1, -1)

    x2 = x.reshape(t, d)
    ff_pad = (-ffn_w_gate.shape[-1]) % LANE
    u, (w_conv_out, wg, wu, wq, wkv, wo) = conv_in(
        x2, row(conv_norm[0]), conv_w_in[0].astype(BF16), row(conv_b_in[0]),
        [conv_w_out[0], ffn_w_gate[0], ffn_w_up[0], w_q[0], w_kv, w_o[0]], [0, ff_pad, ff_pad, 0, 0, 0])
    n_e, _, f_e = moe_w_gate[0].shape
    x3, (we_gate, we_up, we_down) = conv_out(
        u.reshape(b, s, d), x, conv_dw_w[0], row(conv_dw_b[0]), row(conv_ln_g[0]), row(conv_ln_b[0]),
        w_conv_out, row(conv_b_out[0]),
        [moe_w_gate[0].reshape(n_e * d, f_e), moe_w_up[0].reshape(n_e * d, f_e),
         moe_w_down[0].reshape(n_e * f_e, d)])
    x2 = x3.reshape(t, d)
    wd = jnp.pad(ffn_w_down[0], ((0, ff_pad), (0, 0))).astype(BF16)
    x2 = ffn_dense(x2, row(ffn_norm[0]), wg, wu, wd)

    qn = row(jnp.tile(q_norm[0] * (HEAD_DIM ** -0.5), (1, N_HEADS)))
    kn = row(jnp.tile(k_norm, (1, N_HEADS)))
    x3 = x2.reshape(b, s, d)
    qkv = qkv_proj(x3, row(attn_norm[0]), row(kv_norm), wq, wkv, qn, kn)
    outs, lses = [], []
    for g, (_, dil) in enumerate(DILATED_GROUPS):
        qr, kr, vr = (a.reshape(b * dil, s // dil, a.shape[-1]) for a in qkv[3 * g:3 * g + 3])
        o_g, lse_g = dilated_attention_group(qr, kr, vr, band_bias(rel_bias, g))
        outs.append(o_g.reshape(b, dil, s // dil, o_g.shape[-1]))
        lses.append(lse_g.reshape(b, dil, s // dil, LANE))
    x2 = attn_out(x3, outs, lses, wo).reshape(t, d)

    g_moe = row(ffn_norm[1])
    meta_col, meta_row, counts = router(x2, g_moe, moe_router[0])
    plan = moe_plan(counts[:, 0, :N_EXPERTS], t)
    xs = dispatch(x2, g_moe, meta_row, plan)
    ys = experts(xs, plan, we_gate.reshape(n_e, d, f_e), we_up.reshape(n_e, d, f_e),
                 we_down.reshape(n_e, f_e, d))
    return combine(x2, meta_col, ys, plan).reshape(b, s, d)
```

```python
import functools

import numpy as np
import jax
import jax.numpy as jnp
from jax import lax
from jax.experimental import pallas as pl
from jax.experimental.pallas import tpu as pltpu

F32 = jnp.float32
BF16 = jnp.bfloat16

EPS = 1e-6
NEG = -1e30
CONV_WIDTH = 31
CONV_HALO = 32
N_HEADS = 16
HEAD_DIM = 64
DILATED_GROUPS = ((128, 1), (512, 4), (2048, 16))
BLK = 128
ATTN_BLOCKS_PER_STEP = 4
N_BUCKETS = 32
MAX_DISTANCE = 2048
N_EXPERTS = 8
LANE = 128
SUBLANES = 8
VMEM_LIMIT_BYTES = 56 * 1024 * 1024

ROW_TILE = 512
CONV_CHUNK = 64
CONV_SPAN = CONV_CHUNK + SUBLANES * ((CONV_WIDTH - 1) // SUBLANES)
CONV_WINDOW = 128
MOE_ROW_TILE = 1024
MOE_FF_CHUNK = 1024
MOE_SUB_ROWS = 256
MOE_TOKEN_TILE = 512
MOE_DISPATCH_WINDOW = 160
MOE_SMALL_WINDOW = 256
META_FIELDS = 8


def _params(*sem):
    return pltpu.CompilerParams(dimension_semantics=sem, vmem_limit_bytes=VMEM_LIMIT_BYTES)


def _resident(shape):
    nd = len(shape)
    return pl.BlockSpec(shape, lambda *_: (0,) * nd, pipeline_mode=pl.Buffered(1))


def _rms(x, g):
    return x * lax.rsqrt(jnp.mean(x * x, axis=-1, keepdims=True) + EPS) * g


class _WeightCast:
    def __init__(self, weights, pads, steps, index):
        self.shapes = [w.shape for w in weights]
        self.pads = pads
        self.inputs, self.in_specs, self.out_shape, self.out_specs = [], [], [], []
        for w, pad in zip(weights, pads):
            rows, cols = w.shape
            assert rows % steps == 0, (rows, steps)
            block = lambda *grid: (index(*grid), 0, 0)
            self.inputs.append(w.reshape(steps, rows // steps, cols))
            self.in_specs.append(pl.BlockSpec((None, rows // steps, cols), block))
            self.out_shape.append(jax.ShapeDtypeStruct((steps, rows // steps, cols + pad), BF16))
            self.out_specs.append(pl.BlockSpec((None, rows // steps, cols + pad), block))

    @staticmethod
    def run(src_refs, dst_refs):
        for src, dst in zip(src_refs, dst_refs):
            cols = src.shape[-1]
            dst[:, :cols] = src[...].astype(dst.dtype)
            if dst.shape[-1] > cols:
                dst[:, cols:] = jnp.zeros((dst.shape[0], dst.shape[-1] - cols), dst.dtype)

    def results(self, outs):
        return [o.reshape(rows, cols + pad) for o, (rows, cols), pad in zip(outs, self.shapes, self.pads)]


def _conv_in_kernel(x_ref, g_ref, w_ref, b_ref, *refs):
    n_cast = (len(refs) - 1) // 2
    u_ref = refs[n_cast]
    h = _rms(x_ref[...], g_ref[...])
    y = jnp.dot(h.astype(BF16), w_ref[...], preferred_element_type=F32) + b_ref[...]
    d = u_ref.shape[-1]
    u_ref[...] = (y[:, :d] * jax.nn.sigmoid(y[:, d:])).astype(u_ref.dtype)
    _WeightCast.run(refs[:n_cast], refs[n_cast + 1:])


def conv_in(x2, g, w_in, b_in, later_weights, pads):
    t, d = x2.shape
    steps = t // ROW_TILE
    cast = _WeightCast(later_weights, pads, steps, lambda i: i)
    outs = pl.pallas_call(
        _conv_in_kernel,
        out_shape=(jax.ShapeDtypeStruct((t, d), BF16), *cast.out_shape),
        grid=(steps,),
        in_specs=[pl.BlockSpec((ROW_TILE, d), lambda i: (i, 0)),
                  _resident((1, d)), _resident((d, 2 * d)), _resident((1, 2 * d)), *cast.in_specs],
        out_specs=(pl.BlockSpec((ROW_TILE, d), lambda i: (i, 0)), *cast.out_specs),
        compiler_params=_params("parallel"),
        name="conv_in",
    )(x2, g, w_in, b_in, *cast.inputs)
    return outs[0], cast.results(outs[1:])


def _conv_out_kernel(u_ref, x_ref, dww_ref, dwb_ref, lng_ref, lnb_ref, w_ref, b_ref, shift_ref, *refs):
    n_cast = (len(refs) - 3) // 2
    o_ref = refs[n_cast]
    buf_ref, conv_ref = refs[-2:]
    _WeightCast.run(refs[:n_cast], refs[n_cast + 1:-2])
    ts, d = x_ref.shape
    j = pl.program_id(1)

    @pl.when(j == 0)
    def _():
        buf_ref[0:CONV_HALO, :] = jnp.zeros((CONV_HALO, d), buf_ref.dtype)
        buf_ref[CONV_HALO + ts:, :] = jnp.zeros((buf_ref.shape[0] - CONV_HALO - ts, d), buf_ref.dtype)

    @pl.when(j > 0)
    def _():
        buf_ref[0:CONV_HALO, :] = buf_ref[ts:ts + CONV_HALO, :]

    buf_ref[CONV_HALO:CONV_HALO + ts, :] = u_ref[...]

    def lane_chunk(c, carry):
        lanes = pl.ds(pl.multiple_of(c * LANE, LANE), LANE)
        taps = [dww_ref[k:k + 1, lanes] for k in range(CONV_WIDTH)]
        bias = dwb_ref[:, lanes]
        for r0 in range(0, ts, CONV_CHUNK):
            shifted = jnp.dot(shift_ref[...], buf_ref[r0:r0 + CONV_WINDOW, lanes],
                              preferred_element_type=F32)
            acc = jnp.broadcast_to(bias, (CONV_CHUNK, LANE))
            for phase in range(SUBLANES):
                for a, k in enumerate(range(phase, CONV_WIDTH, SUBLANES)):
                    lo = phase * CONV_SPAN + SUBLANES * a
                    acc = acc + taps[k] * shifted[lo:lo + CONV_CHUNK]
            conv_ref[r0:r0 + CONV_CHUNK, lanes] = acc
        return carry

    lax.fori_loop(0, d // LANE, lane_chunk, 0)
    acc = conv_ref[...]
    mu = jnp.mean(acc, axis=-1, keepdims=True)
    cen = acc - mu
    var = jnp.mean(cen * cen, axis=-1, keepdims=True)
    y = cen * lax.rsqrt(var + EPS) * lng_ref[...] + lnb_ref[...]
    act = (y * jax.nn.sigmoid(y)).astype(BF16)
    o_ref[...] = x_ref[...] + b_ref[...] + jnp.dot(act, w_ref[...], preferred_element_type=F32)


def conv_out(u3, x3, dw_w, dw_b, ln_g, ln_b, w_out, b_out, later_weights):
    b, s, d = x3.shape
    ts = ROW_TILE
    nj = s // ts
    cast = _WeightCast(later_weights, [0] * len(later_weights), b * nj, lambda bi, j: bi * nj + j)
    row = pl.BlockSpec((None, ts, d), lambda bi, j: (bi, j, 0))
    first_tap = CONV_HALO - (CONV_WIDTH - 1)
    shift = np.zeros((SUBLANES * CONV_SPAN, CONV_WINDOW), np.float32)
    for p in range(SUBLANES):
        i = np.arange(CONV_SPAN)
        shift[p * CONV_SPAN + i, first_tap + p + i] = 1.0
    outs = pl.pallas_call(
        _conv_out_kernel,
        out_shape=(jax.ShapeDtypeStruct((b, s, d), F32), *cast.out_shape),
        grid=(b, nj),
        in_specs=[row, row, _resident((CONV_WIDTH, d)), _resident((1, d)), _resident((1, d)),
                  _resident((1, d)), _resident((d, d)), _resident((1, d)), _resident(shift.shape),
                  *cast.in_specs],
        out_specs=(row, *cast.out_specs),
        scratch_shapes=[pltpu.VMEM((ts - CONV_CHUNK + CONV_WINDOW, d), BF16), pltpu.VMEM((ts, d), F32)],
        compiler_params=_params("parallel", "arbitrary"),
        name="conv_out",
    )(u3, x3, dw_w, dw_b, ln_g, ln_b, w_out, b_out, jnp.asarray(shift, BF16), *cast.inputs)
    return outs[0], cast.results(outs[1:])


def _ffn_kernel(x_ref, g_ref, wg_ref, wu_ref, wd_ref, o_ref):
    x = x_ref[...]
    h = _rms(x, g_ref[...]).astype(BF16)
    a = jnp.dot(h, wg_ref[...], preferred_element_type=F32)
    u = jnp.dot(h, wu_ref[...], preferred_element_type=F32)
    act = (a * jax.nn.sigmoid(a) * u).astype(BF16)
    o_ref[...] = x + jnp.dot(act, wd_ref[...], preferred_element_type=F32)


def ffn_dense(x2, g, w_gate, w_up, w_down):
    t, d = x2.shape
    f = w_gate.shape[1]
    row = pl.BlockSpec((ROW_TILE, d), lambda i: (i, 0))
    return pl.pallas_call(
        _ffn_kernel,
        out_shape=jax.ShapeDtypeStruct((t, d), F32),
        grid=(t // ROW_TILE,),
        in_specs=[row, _resident((1, d)), _resident((d, f)), _resident((d, f)), _resident((f, d))],
        out_specs=row,
        compiler_params=_params("parallel"),
        name="ffn_dense",
    )(x2, g, w_gate, w_up, w_down)


def _store_heads(y, out_ref, gain_ref=None, gain_off=0):
    dil, n, _ = out_ref.shape
    low = lax.broadcasted_iota(jnp.int32, (1, LANE), 1) < HEAD_DIM
    for c in range(y.shape[-1] // LANE):
        cols = slice(c * LANE, (c + 1) * LANE)
        blk = y[:, cols]
        if gain_ref is not None:
            sq = blk * blk
            ms_lo = jnp.sum(jnp.where(low, sq, 0.0), axis=-1, keepdims=True) * (1.0 / HEAD_DIM)
            ms_hi = jnp.sum(jnp.where(low, 0.0, sq), axis=-1, keepdims=True) * (1.0 / HEAD_DIM)
            inv = jnp.where(low, lax.rsqrt(ms_lo + EPS), lax.rsqrt(ms_hi + EPS))
            blk = blk * inv * gain_ref[:, gain_off + c * LANE:gain_off + (c + 1) * LANE]
        blk = blk.astype(out_ref.dtype)
        for r in range(dil):
            out_ref[r, :, cols] = blk[r * n:(r + 1) * n]


def _qkv_kernel(x_ref, gq_ref, gkv_ref, wq_ref, wkv_ref, qn_ref, kn_ref, *refs):
    out_refs, xn_ref = refs[:-1], refs[-1]
    x = x_ref[...]
    xn = x * lax.rsqrt(jnp.mean(x * x, axis=-1, keepdims=True) + EPS)
    n_cols = xn_ref.shape[0]
    for c in range(n_cols):
        xn_ref[c] = xn[:, c * LANE:(c + 1) * LANE]
    tm = x.shape[0]
    hw = N_HEADS * HEAD_DIM
    qw = len(DILATED_GROUPS) * hw
    for g, (_, dil) in enumerate(DILATED_GROUPS):
        n = tm // dil
        xg = xn if dil == 1 else jnp.concatenate(
            [jnp.concatenate([xn_ref[c, pl.ds(r, n, stride=dil), :] for r in range(dil)], axis=0)
             for c in range(n_cols)], axis=1)
        hq = (xg * gq_ref[...]).astype(BF16)
        hkv = (xg * gkv_ref[...]).astype(BF16)
        q_ref, k_ref, v_ref = out_refs[3 * g:3 * g + 3]
        cols = slice(g * hw, (g + 1) * hw)
        _store_heads(jnp.dot(hq, wq_ref[:, cols], preferred_element_type=F32), q_ref, qn_ref, g * hw)
        _store_heads(jnp.dot(hkv, wkv_ref[:, cols], preferred_element_type=F32), k_ref, kn_ref, g * hw)
        _store_heads(jnp.dot(hkv, wkv_ref[:, qw + g * hw:qw + (g + 1) * hw],
                             preferred_element_type=F32), v_ref)


def qkv_proj(x3, gq, gkv, w_q, w_kv, qn, kn):
    b, s, d = x3.shape
    qw = w_q.shape[1]
    hw = N_HEADS * HEAD_DIM
    tm = ROW_TILE
    out_shape, out_specs = [], []
    for _, dil in DILATED_GROUPS:
        for _ in range(3):
            out_shape.append(jax.ShapeDtypeStruct((b, dil, s // dil, hw), BF16))
            out_specs.append(pl.BlockSpec((None, dil, tm // dil, hw), lambda bi, j: (bi, 0, j, 0)))
    return pl.pallas_call(
        _qkv_kernel,
        out_shape=tuple(out_shape),
        grid=(b, s // tm),
        in_specs=[pl.BlockSpec((None, tm, d), lambda bi, j: (bi, j, 0)), _resident((1, d)),
                  _resident((1, d)), _resident((d, qw)), _resident((d, 2 * qw)), _resident((1, qw)),
                  _resident((1, qw))],
        out_specs=tuple(out_specs),
        scratch_shapes=[pltpu.VMEM((d // LANE, tm, LANE), F32)],
        compiler_params=_params("parallel", "parallel"),
        name="qkv_proj",
    )(x3, gq, gkv, w_q, w_kv, qn, kn)


def _t5_bucket(dist):
    n = np.asarray(dist)
    max_exact = N_BUCKETS // 2
    large = max_exact + (np.log(np.maximum(n, 1) / max_exact)
                         / np.log(MAX_DISTANCE / max_exact)
                         * (N_BUCKETS - max_exact)).astype(np.int32)
    large = np.minimum(large, N_BUCKETS - 1)
    return np.where(n < max_exact, n, large).astype(np.int32)


def band_bias(rel_bias, g):
    window, d = DILATED_GROUPS[g]
    span = window // d
    period = 3 * BLK
    diff = np.arange(period)
    diff = np.where(diff >= 2 * BLK, diff - period, diff)
    sub = BLK - diff
    in_band = (sub >= 0) & (sub <= span)
    bucket = _t5_bucket(np.clip(sub, 0, span) * d)
    table = rel_bias[:, g * N_HEADS:(g + 1) * N_HEADS].astype(F32)
    onehot = jnp.asarray(bucket[:, None] == np.arange(N_BUCKETS)[None, :], F32)
    line = jnp.dot(onehot, table, precision=lax.Precision.HIGHEST).T
    line = jnp.where(jnp.asarray(in_band)[None, :], line, NEG)
    flat = jnp.tile(line, (1, BLK))[:, :BLK * (period - 1)]
    general = flat.reshape(N_HEADS, BLK, period - 1)[:, :, :2 * BLK]
    has_prev = jnp.asarray(np.arange(2 * BLK) >= BLK)[None, None, :]
    first = jnp.where(has_prev, general, NEG)
    return jnp.stack([first, general])


def _attn_kernel(q_ref, k_ref, v_ref, bias_ref, hm_ref, o_ref, lse_ref,
                 kprev_ref, vprev_ref, s_ref, p_ref, m_ref):
    i = pl.program_id(1)

    @pl.when(i == 0)
    def _():
        kprev_ref[...] = jnp.zeros(kprev_ref.shape, kprev_ref.dtype)
        vprev_ref[...] = jnp.zeros(vprev_ref.shape, vprev_ref.dtype)

    n_cols = q_ref.shape[-1] // LANE
    lane = lax.broadcasted_iota(jnp.int32, (BLK, LANE), 1)
    low = lane < HEAD_DIM
    ones = jnp.ones((2 * BLK, LANE), BF16)

    for b in range(q_ref.shape[0] // BLK):
        rows = slice(b * BLK, (b + 1) * BLK)
        before = slice((b - 1) * BLK, b * BLK)
        variant = jnp.minimum(i, 1) if b == 0 else 1

        for c in range(n_cols):
            cols = slice(c * LANE, (c + 1) * LANE)
            q2 = q_ref[rows, cols]
            kcat = jnp.concatenate([kprev_ref[:, cols] if b == 0 else k_ref[before, cols],
                                    k_ref[rows, cols]], axis=0)
            for half in range(2):
                h = 2 * c + half
                s = lax.dot_general(q2 * hm_ref[half:half + 1, :], kcat, (((1,), (1,)), ((), ())),
                                    preferred_element_type=F32) + bias_ref[variant, h]
                s_ref[b, h] = s
                m_ref[b, h] = jnp.broadcast_to(jnp.max(s, axis=-1, keepdims=True), (BLK, LANE))

        for h in range(N_HEADS):
            m = m_ref[b, h]
            p_ref[b, h] = jnp.exp(s_ref[b, h] - jnp.concatenate([m, m], axis=-1)).astype(p_ref.dtype)

        m_tile = jnp.zeros((BLK, LANE), F32)
        l_tile = jnp.ones((BLK, LANE), F32)
        for c in range(n_cols):
            cols = slice(c * LANE, (c + 1) * LANE)
            vcat = jnp.concatenate([vprev_ref[:, cols] if b == 0 else v_ref[before, cols],
                                    v_ref[rows, cols]], axis=0)
            acc = []
            for half in range(2):
                rhs = jnp.concatenate([vcat * hm_ref[half:half + 1, :], ones], axis=-1)
                acc.append(jnp.dot(p_ref[b, 2 * c + half], rhs, preferred_element_type=F32))
            den = jnp.where(low, acc[0][:, LANE:], acc[1][:, LANE:])
            o_ref[rows, cols] = ((acc[0][:, :LANE] + acc[1][:, :LANE]) / den).astype(o_ref.dtype)
            for half in range(2):
                h = 2 * c + half
                m_tile = jnp.where(lane == h, m_ref[b, h], m_tile)
                l_tile = jnp.where(lane == h, acc[half][:, LANE:], l_tile)
        lse_ref[rows, :] = m_tile + jnp.log(l_tile)

    last = slice(q_ref.shape[0] - BLK, q_ref.shape[0])
    kprev_ref[...] = k_ref[last, :]
    vprev_ref[...] = v_ref[last, :]


def dilated_attention_group(q, k, v, bias):
    r, l, w = q.shape
    nb = min(ATTN_BLOCKS_PER_STEP, l // BLK)
    rows = nb * BLK
    assert l % rows == 0, (l, rows)
    blk = pl.BlockSpec((None, rows, w), lambda a, i: (a, i, 0))
    lane = np.arange(LANE)[None, :] < HEAD_DIM
    head_mask = jnp.asarray(np.concatenate([lane, ~lane]).astype(np.float32), BF16)
    return pl.pallas_call(
        _attn_kernel,
        out_shape=(jax.ShapeDtypeStruct((r, l, w), BF16),
                   jax.ShapeDtypeStruct((r, l, LANE), F32)),
        grid=(r, l // rows),
        in_specs=[blk, blk, blk, _resident(bias.shape), _resident((2, LANE))],
        out_specs=(blk, pl.BlockSpec((None, rows, LANE), lambda a, i: (a, i, 0))),
        scratch_shapes=[pltpu.VMEM((BLK, w), BF16), pltpu.VMEM((BLK, w), BF16),
                        pltpu.VMEM((nb, N_HEADS, BLK, 2 * BLK), F32),
                        pltpu.VMEM((nb, N_HEADS, BLK, 2 * BLK), BF16),
                        pltpu.VMEM((nb, N_HEADS, BLK, LANE), F32)],
        compiler_params=_params("parallel", "arbitrary"),
        name="dilated_attn",
    )(q, k, v, bias, head_mask)


def _attn_out_kernel(x_ref, o0_ref, o1_ref, o2_ref, l0_ref, l1_ref, l2_ref, e_ref, w_ref, o_ref,
                     on_ref, ln_ref):
    n_cols = on_ref.shape[1]
    for gi, (og_ref, lg_ref) in enumerate(((o1_ref, l1_ref), (o2_ref, l2_ref))):
        dil, n, _ = og_ref.shape
        for r in range(dil):
            ln_ref[gi, pl.ds(r, n, stride=dil), :] = lg_ref[r]
            for c in range(n_cols):
                on_ref[gi, c, pl.ds(r, n, stride=dil), :] = (
                    og_ref[r, :, c * LANE:(c + 1) * LANE].astype(F32))
    l0, l1, l2 = l0_ref[0], ln_ref[0], ln_ref[1]
    m = jnp.maximum(jnp.maximum(l0, l1), l2)
    e0, e1, e2 = jnp.exp(l0 - m), jnp.exp(l1 - m), jnp.exp(l2 - m)
    den = e0 + e1 + e2
    acc = None
    token_order = [jnp.concatenate([on_ref[gi, c] for c in range(n_cols)], axis=1) for gi in range(2)]
    for e, o in ((e0, o0_ref[0].astype(F32)), (e1, token_order[0]), (e2, token_order[1])):
        wgt = jnp.dot((e / den).astype(BF16), e_ref[...], preferred_element_type=F32)
        term = wgt * o
        acc = term if acc is None else acc + term
    o_ref[...] = x_ref[...] + jnp.dot(acc.astype(BF16), w_ref[...], preferred_element_type=F32)


def attn_out(x3, outs, lses, w_o):
    b, s, d = x3.shape
    tm = ROW_TILE
    expand = np.zeros((LANE, d), np.float32)
    expand[:N_HEADS] = np.repeat(np.eye(N_HEADS, dtype=np.float32), HEAD_DIM, axis=1)
    expand = jnp.asarray(expand, BF16)
    row = pl.BlockSpec((None, tm, d), lambda bi, j: (bi, j, 0))
    o_specs = [pl.BlockSpec((None, dil, tm // dil, d), lambda bi, j: (bi, 0, j, 0))
               for _, dil in DILATED_GROUPS]
    l_specs = [pl.BlockSpec((None, dil, tm // dil, LANE), lambda bi, j: (bi, 0, j, 0))
               for _, dil in DILATED_GROUPS]
    return pl.pallas_call(
        _attn_out_kernel,
        out_shape=jax.ShapeDtypeStruct((b, s, d), F32),
        grid=(b, s // tm),
        in_specs=[row, *o_specs, *l_specs, _resident((LANE, d)), _resident((d, d))],
        out_specs=row,
        scratch_shapes=[pltpu.VMEM((2, d // LANE, tm, LANE), F32), pltpu.VMEM((2, tm, LANE), F32)],
        compiler_params=_params("parallel", "parallel"),
        name="attn_out",
    )(x3, *outs, *lses, expand, w_o)


def _router_kernel(x_ref, g_ref, w_ref, tri_ref, col_ref, row_ref, cnt_ref):
    h = _rms(x_ref[...], g_ref[...])
    w = w_ref[...]
    h_hi = h.astype(BF16)
    h_lo = (h - h_hi.astype(F32)).astype(BF16)
    w_hi = w.astype(BF16)
    w_lo = (w - w_hi.astype(F32)).astype(BF16)
    logits = (jnp.dot(h_hi, w_hi, preferred_element_type=F32)
              + jnp.dot(h_lo, w_hi, preferred_element_type=F32)
              + jnp.dot(h_hi, w_lo, preferred_element_type=F32))
    ne = N_EXPERTS
    lane = lax.broadcasted_iota(jnp.int32, logits.shape, 1)
    logits = jnp.where(lane < ne, logits, -jnp.inf)
    m1 = jnp.max(logits, axis=-1, keepdims=True)
    i1 = jnp.min(jnp.where(logits == m1, lane, ne), axis=-1, keepdims=True)
    rest = jnp.where(lane == i1, -jnp.inf, logits)
    m2 = jnp.max(rest, axis=-1, keepdims=True)
    i2 = jnp.min(jnp.where(rest == m2, lane, ne), axis=-1, keepdims=True)
    e2 = jnp.exp(m2 - m1)
    g1 = 1.0 / (1.0 + e2)
    g2 = e2 / (1.0 + e2)
    both = (lane == i1) | (lane == i2)
    cnt = jnp.dot(tri_ref[...], jnp.where(both, 1.0, 0.0).astype(BF16), preferred_element_type=F32)
    r1 = jnp.sum(jnp.where(lane == i1, cnt, 0.0), axis=-1, keepdims=True)
    r2 = jnp.sum(jnp.where(lane == i2, cnt, 0.0), axis=-1, keepdims=True)
    cnt_ref[...] = jnp.sum(jnp.where(both, 1, 0), axis=0, keepdims=True)
    fields = (i1.astype(F32), i2.astype(F32), r1, r2, g1, g2)
    meta = jnp.zeros(logits.shape, F32)
    for f, val in enumerate(fields):
        meta = jnp.where(lane == f, val, meta)
    col_ref[...] = meta[:, :META_FIELDS]
    row_ref[...] = meta.T[:META_FIELDS, :]


def router(x2, g, w_router):
    t, d = x2.shape
    tt = MOE_TOKEN_TILE
    nt = t // tt
    tri = jnp.asarray(np.tril(np.ones((tt, tt), np.float32), -1), BF16)
    return pl.pallas_call(
        _router_kernel,
        out_shape=(jax.ShapeDtypeStruct((t, META_FIELDS), F32),
                   jax.ShapeDtypeStruct((nt, META_FIELDS, tt), F32),
                   jax.ShapeDtypeStruct((nt, 1, LANE), jnp.int32)),
        grid=(nt,),
        in_specs=[pl.BlockSpec((tt, d), lambda i: (i, 0)), _resident((1, d)),
                  _resident((d, LANE)), _resident((tt, tt))],
        out_specs=(pl.BlockSpec((tt, META_FIELDS), lambda i: (i, 0)),
                   pl.BlockSpec((None, META_FIELDS, tt), lambda i: (i, 0, 0)),
                   pl.BlockSpec((None, 1, LANE), lambda i: (i, 0, 0))),
        compiler_params=_params("parallel"),
        name="moe_router",
    )(x2, g, _pad_cols(w_router, LANE), tri)


def moe_plan(counts, n_tokens):
    tm = MOE_ROW_TILE
    counts = (counts + SUBLANES - 1) // SUBLANES * SUBLANES
    experts_ = jnp.arange(N_EXPERTS, dtype=jnp.int32)
    total = jnp.sum(counts, axis=0)
    owned = (total + MOE_TOKEN_TILE + tm - 1) // tm
    owned_end = jnp.cumsum(owned)
    region = (owned_end - owned) * tm
    base = region[None, :] + jnp.cumsum(counts, axis=0) - counts
    local = jnp.cumsum(counts, axis=1) - counts
    max_rows = 2 * n_tokens + (SUBLANES - 1) * N_EXPERTS * counts.shape[0] + N_EXPERTS * MOE_TOKEN_TILE
    n_tiles = -(-max_rows // tm) + N_EXPERTS + 1
    tile_id = jnp.arange(n_tiles, dtype=jnp.int32)
    tile_expert = jnp.minimum(jnp.sum(tile_id[:, None] >= owned_end[None, :], axis=1), N_EXPERTS - 1)
    pick = tile_expert[:, None] == experts_[None, :]
    rows_before = (tile_id - jnp.sum(jnp.where(pick, (owned_end - owned)[None, :], 0), axis=1)) * tm
    rows_left = jnp.sum(jnp.where(pick, total[None, :], 0), axis=1) - rows_before
    tile_rows = jnp.where(tile_id < owned_end[-1], jnp.clip(rows_left, 0, tm), 0).astype(jnp.int32)
    return dict(n_tiles=n_tiles, counts=counts.reshape(-1), base=base.reshape(-1).astype(jnp.int32),
                local=local.reshape(-1).astype(jnp.int32), tail=(region + total).astype(jnp.int32),
                tile_expert=tile_expert.astype(jnp.int32), tile_rows=tile_rows,
                tile_valid=(tile_rows > 0).astype(jnp.int32))


def _run_windows(n):
    return (((n > 0) & (n <= MOE_DISPATCH_WINDOW), MOE_DISPATCH_WINDOW),
            (n > MOE_DISPATCH_WINDOW, MOE_TOKEN_TILE))


def _dispatch_kernel(cnt_ref, local_ref, base_ref, tail_ref, valid_ref, x_ref, g_ref, meta_ref, xs_ref,
                     stage_ref, sem):
    i = pl.program_id(0)
    tt, d = x_ref.shape
    n_sorted = 2 * tt + SUBLANES * N_EXPERTS
    n_zero = stage_ref.shape[1] - n_sorted
    tm = MOE_ROW_TILE

    @pl.when(i == 0)
    def _():
        for side in range(2):
            stage_ref[side, n_sorted:, :] = jnp.zeros((n_zero, stage_ref.shape[2]), F32)

        def zero_copy(row, rows):
            return pltpu.make_async_copy(stage_ref.at[0, pl.ds(n_sorted, rows)],
                                         xs_ref.at[pl.ds(pl.multiple_of(row, SUBLANES), rows)], sem)

        pieces = [n_zero >> b for b in range((n_zero // SUBLANES).bit_length())]
        for action in ("start", "wait"):
            def empty_tile(k, carry, action=action):
                @pl.when(valid_ref[k] == 0)
                def _():
                    for part in range(tm // n_zero):
                        getattr(zero_copy(k * tm + part * n_zero, n_zero), action)()
                return carry

            lax.fori_loop(0, valid_ref.shape[0], empty_tile, 0)
            for e in range(N_EXPERTS):
                row = tail_ref[e]
                left = (-row) & (tm - 1)
                for rows in pieces:
                    @pl.when((left & rows) != 0)
                    def _(row=row, rows=rows):
                        getattr(zero_copy(row, rows), action)()
                    row = row + (left & rows)

    h = _rms(x_ref[...], g_ref[...]).astype(BF16)
    meta = meta_ref[...]
    e1, e2, r1, r2, g1, g2 = (meta[f:f + 1, :] for f in range(6))
    off1 = jnp.zeros_like(r1)
    off2 = jnp.zeros_like(r2)
    for e in range(N_EXPERTS):
        lo = local_ref[i * N_EXPERTS + e].astype(F32)
        off1 = jnp.where(e1 == e, lo, off1)
        off2 = jnp.where(e2 == e, lo, off2)
    slot = lax.broadcasted_iota(jnp.int32, (n_sorted, tt), 0).astype(F32)
    hit1 = slot == r1 + off1
    hit2 = slot == r2 + off2
    perm = jnp.where(hit1 | hit2, 1.0, 0.0).astype(BF16)
    half = i % 2
    stage_ref[half, 0:n_sorted, 0:d] = jnp.dot(perm, h, preferred_element_type=F32)
    gate = jnp.sum(jnp.where(hit1, g1, 0.0) + jnp.where(hit2, g2, 0.0), axis=-1, keepdims=True)
    stage_ref[half, 0:n_sorted, d:] = jnp.broadcast_to(gate, (n_sorted, LANE))

    def run_copies(step, action):
        for e in range(N_EXPERTS):
            k = step * N_EXPERTS + e
            for cond, rows in _run_windows(cnt_ref[k]):
                @pl.when(cond)
                def _(k=k, rows=rows):
                    copy = pltpu.make_async_copy(
                        stage_ref.at[step % 2, pl.ds(pl.multiple_of(local_ref[k], SUBLANES), rows)],
                        xs_ref.at[pl.ds(pl.multiple_of(base_ref[k], SUBLANES), rows)], sem)
                    getattr(copy, action)()

    @pl.when(i > 0)
    def _():
        run_copies(i - 1, "wait")

    run_copies(i, "start")

    @pl.when(i == pl.num_programs(0) - 1)
    def _():
        run_copies(i, "wait")


def dispatch(x2, g, meta_row, plan):
    t, d = x2.shape
    tt = MOE_TOKEN_TILE
    grid_spec = pltpu.PrefetchScalarGridSpec(
        num_scalar_prefetch=5,
        grid=(t // tt,),
        in_specs=[pl.BlockSpec((tt, d), lambda i, *_: (i, 0)),
                  pl.BlockSpec((1, d), lambda i, *_: (0, 0)),
                  pl.BlockSpec((None, META_FIELDS, tt), lambda i, *_: (i, 0, 0))],
        out_specs=pl.BlockSpec(memory_space=pl.ANY),
        scratch_shapes=[pltpu.VMEM((2, 2 * tt + SUBLANES * N_EXPERTS + MOE_TOKEN_TILE, d + LANE), F32),
                        pltpu.SemaphoreType.DMA(())],
    )
    return pl.pallas_call(
        _dispatch_kernel,
        out_shape=jax.ShapeDtypeStruct((plan["n_tiles"] * MOE_ROW_TILE, d + LANE), F32),
        grid_spec=grid_spec,
        compiler_params=_params("arbitrary"),
        name="moe_dispatch",
    )(plan["counts"], plan["local"], plan["base"], plan["tail"], plan["tile_valid"], x2, g, meta_row)


def _experts_kernel(te_ref, rows_ref, x_ref, wg_ref, wu_ref, wd_ref, o_ref):
    n = rows_ref[pl.program_id(0)]
    tm, d = o_ref.shape
    f = wg_ref.shape[-1]

    def ffn(r0, nrows):
        rows = slice(r0, r0 + nrows)
        xb = x_ref[rows, :d].astype(BF16)
        acc = None
        for c0 in range(0, f, MOE_FF_CHUNK):
            cols = slice(c0, min(c0 + MOE_FF_CHUNK, f))
            a = jnp.dot(xb, wg_ref[:, cols], preferred_element_type=F32)
            u = jnp.dot(xb, wu_ref[:, cols], preferred_element_type=F32)
            act = (a * jax.nn.sigmoid(a) * u).astype(BF16)
            part = jnp.dot(act, wd_ref[cols, :], preferred_element_type=F32)
            acc = part if acc is None else acc + part
        o_ref[rows, :] = acc * x_ref[rows, d:d + 1]

    last_piece = tm - MOE_SUB_ROWS

    @pl.when(n > last_piece)
    def _():
        ffn(0, tm)

    @pl.when(n <= last_piece)
    def _():
        for r0 in range(0, last_piece, MOE_SUB_ROWS):
            @pl.when(n > r0)
            def _(r0=r0):
                ffn(r0, MOE_SUB_ROWS)

            @pl.when(n <= r0)
            def _(r0=r0):
                o_ref[r0:r0 + MOE_SUB_ROWS, :] = jnp.zeros((MOE_SUB_ROWS, d), o_ref.dtype)
        o_ref[last_piece:, :] = jnp.zeros((MOE_SUB_ROWS, d), o_ref.dtype)


def experts(xs, plan, w_gate, w_up, w_down):
    n_rows, dw = xs.shape
    d = dw - LANE
    f = w_gate.shape[-1]
    tm = MOE_ROW_TILE
    weights = lambda shape: pl.BlockSpec((None,) + shape, lambda i, te, tv: (te[i], 0, 0),
                                         pipeline_mode=pl.Buffered(1))
    grid_spec = pltpu.PrefetchScalarGridSpec(
        num_scalar_prefetch=2,
        grid=(n_rows // tm,),
        in_specs=[pl.BlockSpec((tm, dw), lambda i, te, tv: (i, 0)),
                  weights((d, f)), weights((d, f)), weights((f, d))],
        out_specs=pl.BlockSpec((tm, d), lambda i, te, tv: (i, 0)),
    )
    return pl.pallas_call(
        _experts_kernel,
        out_shape=jax.ShapeDtypeStruct((n_rows, d), F32),
        grid_spec=grid_spec,
        compiler_params=_params("arbitrary"),
        name="moe_experts",
    )(plan["tile_expert"], plan["tile_rows"], xs, w_gate, w_up, w_down)


def _combine_kernel(cnt_ref, base_ref, x_ref, meta_ref, ys_ref, o_ref, small_ref, wide_ref, sems):
    i = pl.program_id(0)
    tt = x_ref.shape[0]
    meta = meta_ref[...]
    e1, e2, r1, r2 = (meta[:, f:f + 1] for f in range(4))
    widest = cnt_ref[i * N_EXPERTS]
    for e in range(1, N_EXPERTS):
        widest = jnp.maximum(widest, cnt_ref[i * N_EXPERTS + e])

    def small_copies(step):
        return [pltpu.make_async_copy(
            ys_ref.at[pl.ds(pl.multiple_of(base_ref[step * N_EXPERTS + e], SUBLANES), MOE_SMALL_WINDOW)],
            small_ref.at[step % 2, e], sems.at[step % 2]) for e in range(N_EXPERTS)]

    @pl.when(i == 0)
    def _():
        for c in small_copies(i):
            c.start()

    @pl.when(i + 1 < pl.num_programs(0))
    def _():
        for c in small_copies(i + 1):
            c.start()

    for c in small_copies(i):
        c.wait()

    def pick_rows(window_of):
        rows = window_of(0).shape[0]
        pos = lax.broadcasted_iota(jnp.int32, (tt, rows), 1).astype(F32)
        acc = x_ref[...]
        for e in range(N_EXPERTS):
            want1 = jnp.where(e1 == e, r1, -1.0)
            want2 = jnp.where(e2 == e, r2, -1.0)
            pick = jnp.where((pos == want1) | (pos == want2), 1.0, 0.0).astype(BF16)
            acc = acc + jnp.dot(pick, window_of(e).astype(BF16), preferred_element_type=F32)
        o_ref[...] = acc

    @pl.when(widest <= MOE_SMALL_WINDOW)
    def _():
        pick_rows(lambda e: small_ref[i % 2, e])

    @pl.when(widest > MOE_SMALL_WINDOW)
    def _():
        copies = [pltpu.make_async_copy(
            ys_ref.at[pl.ds(pl.multiple_of(base_ref[i * N_EXPERTS + e], SUBLANES), tt)],
            wide_ref.at[e], sems.at[2]) for e in range(N_EXPERTS)]
        for c in copies:
            c.start()
        for c in copies:
            c.wait()
        pick_rows(lambda e: wide_ref[e])


def combine(x2, meta_col, ys, plan):
    t, d = x2.shape
    tt = MOE_TOKEN_TILE
    grid_spec = pltpu.PrefetchScalarGridSpec(
        num_scalar_prefetch=2,
        grid=(t // tt,),
        in_specs=[pl.BlockSpec((tt, d), lambda i, *_: (i, 0)),
                  pl.BlockSpec((tt, META_FIELDS), lambda i, *_: (i, 0)),
                  pl.BlockSpec(memory_space=pl.ANY)],
        out_specs=pl.BlockSpec((tt, d), lambda i, *_: (i, 0)),
        scratch_shapes=[pltpu.VMEM((2, N_EXPERTS, MOE_SMALL_WINDOW, d), F32),
                        pltpu.VMEM((N_EXPERTS, tt, d), F32), pltpu.SemaphoreType.DMA((3,))],
    )
    return pl.pallas_call(
        _combine_kernel,
        out_shape=jax.ShapeDtypeStruct((t, d), F32),
        grid_spec=grid_spec,
        compiler_params=_params("arbitrary"),
        name="moe_combine",
    )(plan["counts"], plan["base"], x2, meta_col, ys)


def _pad_cols(w, mult):
    pad = (-w.shape[-1]) % mult
    return jnp.pad(w, ((0, 0),) * (w.ndim - 1) + ((0, pad),))


def kernel(x, rel_bias, conv_norm, conv_w_in, conv_b_in, conv_dw_w, conv_dw_b, conv_ln_g, conv_ln_b, conv_w_out, conv_b_out, kv_norm, w_kv, k_norm, attn_norm, w_q, q_norm, w_o, ffn_norm, ffn_w_gate, ffn_w_up, ffn_w_down, moe_router, moe_w_gate, moe_w_up, moe_w_down):
    b, s, d = x.shape
    t = b * s
    row = lambda v: v.reshape(1, -1)

    x2 = x.reshape(t, d)
    ff_pad = (-ffn_w_gate.shape[-1]) % LANE
    u, (w_conv_out, wg, wu, wq, wkv, wo) = conv_in(
        x2, row(conv_norm[0]), conv_w_in[0].astype(BF16), row(conv_b_in[0]),
        [conv_w_out[0], ffn_w_gate[0], ffn_w_up[0], w_q[0], w_kv, w_o[0]], [0, ff_pad, ff_pad, 0, 0, 0])
    n_e, _, f_e = moe_w_gate[0].shape
    x3, (we_gate, we_up, we_down) = conv_out(
        u.reshape(b, s, d), x, conv_dw_w[0], row(conv_dw_b[0]), row(conv_ln_g[0]), row(conv_ln_b[0]),
        w_conv_out, row(conv_b_out[0]),
        [moe_w_gate[0].reshape(n_e * d, f_e), moe_w_up[0].reshape(n_e * d, f_e),
         moe_w_down[0].reshape(n_e * f_e, d)])
    x2 = x3.reshape(t, d)
    wd = jnp.pad(ffn_w_down[0], ((0, ff_pad), (0, 0))).astype(BF16)
    x2 = ffn_dense(x2, row(ffn_norm[0]), wg, wu, wd)

    qn = row(jnp.tile(q_norm[0] * (HEAD_DIM ** -0.5), (1, N_HEADS)))
    kn = row(jnp.tile(k_norm, (1, N_HEADS)))
    x3 = x2.reshape(b, s, d)
    qkv = qkv_proj(x3, row(attn_norm[0]), row(kv_norm), wq, wkv, qn, kn)
    outs, lses = [], []
    for g, (_, dil) in enumerate(DILATED_GROUPS):
        qr, kr, vr = (a.reshape(b * dil, s // dil, a.shape[-1]) for a in qkv[3 * g:3 * g + 3])
        o_g, lse_g = dilated_attention_group(qr, kr, vr, band_bias(rel_bias, g))
        outs.append(o_g.reshape(b, dil, s // dil, o_g.shape[-1]))
        lses.append(lse_g.reshape(b, dil, s // dil, LANE))
    x2 = attn_out(x3, outs, lses, wo).reshape(t, d)

    g_moe = row(ffn_norm[1])
    meta_col, meta_row, counts = router(x2, g_moe, moe_router[0])
    plan = moe_plan(counts[:, 0, :N_EXPERTS], t)
    xs = dispatch(x2, g_moe, meta_row, plan)
    ys = experts(xs, plan, we_gate.reshape(n_e, d, f_e), we_up.reshape(n_e, d, f_e),
                 we_down.reshape(n_e, f_e, d))
    return combine(x2, meta_col, ys, plan).reshape(b, s, d)
```

```python
import functools

import numpy as np
import jax
import jax.numpy as jnp
from jax import lax
from jax.experimental import pallas as pl
from jax.experimental.pallas import tpu as pltpu

F32 = jnp.float32
BF16 = jnp.bfloat16

EPS = 1e-6
NEG = -1e30
CONV_WIDTH = 31
CONV_HALO = 32
N_HEADS = 16
HEAD_DIM = 64
DILATED_GROUPS = ((128, 1), (512, 4), (2048, 16))
BLK = 128
ATTN_BLOCKS_PER_STEP = 4
N_BUCKETS = 32
MAX_DISTANCE = 2048
N_EXPERTS = 8
LANE = 128
SUBLANES = 8
VMEM_LIMIT_BYTES = 56 * 1024 * 1024

ROW_TILE = 512
CONV_CHUNK = 64
CONV_SPAN = CONV_CHUNK + SUBLANES * ((CONV_WIDTH - 1) // SUBLANES)
CONV_WINDOW = 128
MOE_ROW_TILE = 1024
MOE_FF_CHUNK = 1024
MOE_SUB_ROWS = 256
MOE_TOKEN_TILE = 512
MOE_DISPATCH_WINDOW = 160
MOE_SMALL_WINDOW = 256
META_FIELDS = 8


def _params(*sem):
    return pltpu.CompilerParams(dimension_semantics=sem, vmem_limit_bytes=VMEM_LIMIT_BYTES)


def _resident(shape):
    nd = len(shape)
    return pl.BlockSpec(shape, lambda *_: (0,) * nd, pipeline_mode=pl.Buffered(1))


def _rms(x, g):
    return x * lax.rsqrt(jnp.mean(x * x, axis=-1, keepdims=True) + EPS) * g


class _WeightCast:
    def __init__(self, weights, pads, steps, index):
        self.shapes = [w.shape for w in weights]
        self.pads = pads
        self.inputs, self.in_specs, self.out_shape, self.out_specs = [], [], [], []
        for w, pad in zip(weights, pads):
            rows, cols = w.shape
            assert rows % steps == 0, (rows, steps)
            block = lambda *grid: (index(*grid), 0, 0)
            self.inputs.append(w.reshape(steps, rows // steps, cols))
            self.in_specs.append(pl.BlockSpec((None, rows // steps, cols), block))
            self.out_shape.append(jax.ShapeDtypeStruct((steps, rows // steps, cols + pad), BF16))
            self.out_specs.append(pl.BlockSpec((None, rows // steps, cols + pad), block))

    @staticmethod
    def run(src_refs, dst_refs):
        for src, dst in zip(src_refs, dst_refs):
            cols = src.shape[-1]
            dst[:, :cols] = src[...].astype(dst.dtype)
            if dst.shape[-1] > cols:
                dst[:, cols:] = jnp.zeros((dst.shape[0], dst.shape[-1] - cols), dst.dtype)

    def results(self, outs):
        return [o.reshape(rows, cols + pad) for o, (rows, cols), pad in zip(outs, self.shapes, self.pads)]


def _conv_in_kernel(x_ref, g_ref, w_ref, b_ref, *refs):
    n_cast = (len(refs) - 1) // 2
    u_ref = refs[n_cast]
    h = _rms(x_ref[...], g_ref[...])
    y = jnp.dot(h.astype(BF16), w_ref[...], preferred_element_type=F32) + b_ref[...]
    d = u_ref.shape[-1]
    u_ref[...] = (y[:, :d] * jax.nn.sigmoid(y[:, d:])).astype(u_ref.dtype)
    _WeightCast.run(refs[:n_cast], refs[n_cast + 1:])


def conv_in(x2, g, w_in, b_in, later_weights, pads):
    t, d = x2.shape
    steps = t // ROW_TILE
    cast = _WeightCast(later_weights, pads, steps, lambda i: i)
    outs = pl.pallas_call(
        _conv_in_kernel,
        out_shape=(jax.ShapeDtypeStruct((t, d), BF16), *cast.out_shape),
        grid=(steps,),
        in_specs=[pl.BlockSpec((ROW_TILE, d), lambda i: (i, 0)),
                  _resident((1, d)), _resident((d, 2 * d)), _resident((1, 2 * d)), *cast.in_specs],
        out_specs=(pl.BlockSpec((ROW_TILE, d), lambda i: (i, 0)), *cast.out_specs),
        compiler_params=_params("parallel"),
        name="conv_in",
    )(x2, g, w_in, b_in, *cast.inputs)
    return outs[0], cast.results(outs[1:])


def _conv_out_kernel(u_ref, x_ref, dww_ref, dwb_ref, lng_ref, lnb_ref, w_ref, b_ref, shift_ref, *refs):
    n_cast = (len(refs) - 3) // 2
    o_ref = refs[n_cast]
    buf_ref, conv_ref = refs[-2:]
    _WeightCast.run(refs[:n_cast], refs[n_cast + 1:-2])
    ts, d = x_ref.shape
    j = pl.program_id(1)

    @pl.when(j == 0)
    def _():
        buf_ref[0:CONV_HALO, :] = jnp.zeros((CONV_HALO, d), buf_ref.dtype)
        buf_ref[CONV_HALO + ts:, :] = jnp.zeros((buf_ref.shape[0] - CONV_HALO - ts, d), buf_ref.dtype)

    @pl.when(j > 0)
    def _():
        buf_ref[0:CONV_HALO, :] = buf_ref[ts:ts + CONV_HALO, :]

    buf_ref[CONV_HALO:CONV_HALO + ts, :] = u_ref[...]

    def lane_chunk(c, carry):
        lanes = pl.ds(pl.multiple_of(c * LANE, LANE), LANE)
        taps = [dww_ref[k:k + 1, lanes] for k in range(CONV_WIDTH)]
        bias = dwb_ref[:, lanes]
        for r0 in range(0, ts, CONV_CHUNK):
            shifted = jnp.dot(shift_ref[...], buf_ref[r0:r0 + CONV_WINDOW, lanes],
                              preferred_element_type=F32)
            acc = jnp.broadcast_to(bias, (CONV_CHUNK, LANE))
            for phase in range(SUBLANES):
                for a, k in enumerate(range(phase, CONV_WIDTH, SUBLANES)):
                    lo = phase * CONV_SPAN + SUBLANES * a
                    acc = acc + taps[k] * shifted[lo:lo + CONV_CHUNK]
            conv_ref[r0:r0 + CONV_CHUNK, lanes] = acc
        return carry

    lax.fori_loop(0, d // LANE, lane_chunk, 0)
    acc = conv_ref[...]
    mu = jnp.mean(acc, axis=-1, keepdims=True)
    cen = acc - mu
    var = jnp.mean(cen * cen, axis=-1, keepdims=True)
    y = cen * lax.rsqrt(var + EPS) * lng_ref[...] + lnb_ref[...]
    act = (y * jax.nn.sigmoid(y)).astype(BF16)
    o_ref[...] = x_ref[...] + b_ref[...] + jnp.dot(act, w_ref[...], preferred_element_type=F32)


def conv_out(u3, x3, dw_w, dw_b, ln_g, ln_b, w_out, b_out, later_weights):
    b, s, d = x3.shape
    ts = ROW_TILE
    nj = s // ts
    cast = _WeightCast(later_weights, [0] * len(later_weights), b * nj, lambda bi, j: bi * nj + j)
    row = pl.BlockSpec((None, ts, d), lambda bi, j: (bi, j, 0))
    first_tap = CONV_HALO - (CONV_WIDTH - 1)
    shift = np.zeros((SUBLANES * CONV_SPAN, CONV_WINDOW), np.float32)
    for p in range(SUBLANES):
        i = np.arange(CONV_SPAN)
        shift[p * CONV_SPAN + i, first_tap + p + i] = 1.0
    outs = pl.pallas_call(
        _conv_out_kernel,
        out_shape=(jax.ShapeDtypeStruct((b, s, d), F32), *cast.out_shape),
        grid=(b, nj),
        in_specs=[row, row, _resident((CONV_WIDTH, d)), _resident((1, d)), _resident((1, d)),
                  _resident((1, d)), _resident((d, d)), _resident((1, d)), _resident(shift.shape),
                  *cast.in_specs],
        out_specs=(row, *cast.out_specs),
        scratch_shapes=[pltpu.VMEM((ts - CONV_CHUNK + CONV_WINDOW, d), BF16), pltpu.VMEM((ts, d), F32)],
        compiler_params=_params("parallel", "arbitrary"),
        name="conv_out",
    )(u3, x3, dw_w, dw_b, ln_g, ln_b, w_out, b_out, jnp.asarray(shift, BF16), *cast.inputs)
    return outs[0], cast.results(outs[1:])


def _ffn_kernel(x_ref, g_ref, wg_ref, wu_ref, wd_ref, o_ref):
    x = x_ref[...]
    h = _rms(x, g_ref[...]).astype(BF16)
    a = jnp.dot(h, wg_ref[...], preferred_element_type=F32)
    u = jnp.dot(h, wu_ref[...], preferred_element_type=F32)
    act = (a * jax.nn.sigmoid(a) * u).astype(BF16)
    o_ref[...] = x + jnp.dot(act, wd_ref[...], preferred_element_type=F32)


def ffn_dense(x2, g, w_gate, w_up, w_down):
    t, d = x2.shape
    f = w_gate.shape[1]
    row = pl.BlockSpec((ROW_TILE, d), lambda i: (i, 0))
    return pl.pallas_call(
        _ffn_kernel,
        out_shape=jax.ShapeDtypeStruct((t, d), F32),
        grid=(t // ROW_TILE,),
        in_specs=[row, _resident((1, d)), _resident((d, f)), _resident((d, f)), _resident((f, d))],
        out_specs=row,
        compiler_params=_params("parallel"),
        name="ffn_dense",
    )(x2, g, w_gate, w_up, w_down)


def _store_heads(y, out_ref, gain_ref=None, gain_off=0):
    dil, n, _ = out_ref.shape
    low = lax.broadcasted_iota(jnp.int32, (1, LANE), 1) < HEAD_DIM
    for c in range(y.shape[-1] // LANE):
        cols = slice(c * LANE, (c + 1) * LANE)
        blk = y[:, cols]
        if gain_ref is not None:
            sq = blk * blk
            ms_lo = jnp.sum(jnp.where(low, sq, 0.0), axis=-1, keepdims=True) * (1.0 / HEAD_DIM)
            ms_hi = jnp.sum(jnp.where(low, 0.0, sq), axis=-1, keepdims=True) * (1.0 / HEAD_DIM)
            inv = jnp.where(low, lax.rsqrt(ms_lo + EPS), lax.rsqrt(ms_hi + EPS))
            blk = blk * inv * gain_ref[:, gain_off + c * LANE:gain_off + (c + 1) * LANE]
        blk = blk.astype(out_ref.dtype)
        for r in range(dil):
            out_ref[r, :, cols] = blk[r * n:(r + 1) * n]


def _qkv_kernel(x_ref, gq_ref, gkv_ref, wq_ref, wkv_ref, qn_ref, kn_ref, *refs):
    out_refs, xn_ref = refs[:-1], refs[-1]
    x = x_ref[...]
    xn = x * lax.rsqrt(jnp.mean(x * x, axis=-1, keepdims=True) + EPS)
    n_cols = xn_ref.shape[0]
    for c in range(n_cols):
        xn_ref[c] = xn[:, c * LANE:(c + 1) * LANE]
    tm = x.shape[0]
    hw = N_HEADS * HEAD_DIM
    qw = len(DILATED_GROUPS) * hw
    for g, (_, dil) in enumerate(DILATED_GROUPS):
        n = tm // dil
        xg = xn if dil == 1 else jnp.concatenate(
            [jnp.concatenate([xn_ref[c, pl.ds(r, n, stride=dil), :] for r in range(dil)], axis=0)
             for c in range(n_cols)], axis=1)
        hq = (xg * gq_ref[...]).astype(BF16)
        hkv = (xg * gkv_ref[...]).astype(BF16)
        q_ref, k_ref, v_ref = out_refs[3 * g:3 * g + 3]
        cols = slice(g * hw, (g + 1) * hw)
        _store_heads(jnp.dot(hq, wq_ref[:, cols], preferred_element_type=F32), q_ref, qn_ref, g * hw)
        _store_heads(jnp.dot(hkv, wkv_ref[:, cols], preferred_element_type=F32), k_ref, kn_ref, g * hw)
        _store_heads(jnp.dot(hkv, wkv_ref[:, qw + g * hw:qw + (g + 1) * hw],
                             preferred_element_type=F32), v_ref)


def qkv_proj(x3, gq, gkv, w_q, w_kv, qn, kn):
    b, s, d = x3.shape
    qw = w_q.shape[1]
    hw = N_HEADS * HEAD_DIM
    tm = ROW_TILE
    out_shape, out_specs = [], []
    for _, dil in DILATED_GROUPS:
        for _ in range(3):
            out_shape.append(jax.ShapeDtypeStruct((b, dil, s // dil, hw), BF16))
            out_specs.append(pl.BlockSpec((None, dil, tm // dil, hw), lambda bi, j: (bi, 0, j, 0)))
    return pl.pallas_call(
        _qkv_kernel,
        out_shape=tuple(out_shape),
        grid=(b, s // tm),
        in_specs=[pl.BlockSpec((None, tm, d), lambda bi, j: (bi, j, 0)), _resident((1, d)),
                  _resident((1, d)), _resident((d, qw)), _resident((d, 2 * qw)), _resident((1, qw)),
                  _resident((1, qw))],
        out_specs=tuple(out_specs),
        scratch_shapes=[pltpu.VMEM((d // LANE, tm, LANE), F32)],
        compiler_params=_params("parallel", "parallel"),
        name="qkv_proj",
    )(x3, gq, gkv, w_q, w_kv, qn, kn)


def _t5_bucket(dist):
    n = np.asarray(dist)
    max_exact = N_BUCKETS // 2
    large = max_exact + (np.log(np.maximum(n, 1) / max_exact)
                         / np.log(MAX_DISTANCE / max_exact)
                         * (N_BUCKETS - max_exact)).astype(np.int32)
    large = np.minimum(large, N_BUCKETS - 1)
    return np.where(n < max_exact, n, large).astype(np.int32)


def band_bias(rel_bias, g):
    window, d = DILATED_GROUPS[g]
    span = window // d
    period = 3 * BLK
    diff = np.arange(period)
    diff = np.where(diff >= 2 * BLK, diff - period, diff)
    sub = BLK - diff
    in_band = (sub >= 0) & (sub <= span)
    bucket = _t5_bucket(np.clip(sub, 0, span) * d)
    table = rel_bias[:, g * N_HEADS:(g + 1) * N_HEADS].astype(F32)
    onehot = jnp.asarray(bucket[:, None] == np.arange(N_BUCKETS)[None, :], F32)
    line = jnp.dot(onehot, table, precision=lax.Precision.HIGHEST).T
    line = jnp.where(jnp.asarray(in_band)[None, :], line, NEG)
    flat = jnp.tile(line, (1, BLK))[:, :BLK * (period - 1)]
    general = flat.reshape(N_HEADS, BLK, period - 1)[:, :, :2 * BLK]
    has_prev = jnp.asarray(np.arange(2 * BLK) >= BLK)[None, None, :]
    first = jnp.where(has_prev, general, NEG)
    return jnp.stack([first, general])


def _attn_kernel(q_ref, k_ref, v_ref, bias_ref, hm_ref, o_ref, lse_ref,
                 kprev_ref, vprev_ref, s_ref, p_ref, m_ref):
    i = pl.program_id(1)

    @pl.when(i == 0)
    def _():
        kprev_ref[...] = jnp.zeros(kprev_ref.shape, kprev_ref.dtype)
        vprev_ref[...] = jnp.zeros(vprev_ref.shape, vprev_ref.dtype)

    n_cols = q_ref.shape[-1] // LANE
    lane = lax.broadcasted_iota(jnp.int32, (BLK, LANE), 1)
    low = lane < HEAD_DIM
    ones = jnp.ones((2 * BLK, LANE), BF16)

    for b in range(q_ref.shape[0] // BLK):
        rows = slice(b * BLK, (b + 1) * BLK)
        before = slice((b - 1) * BLK, b * BLK)
        variant = jnp.minimum(i, 1) if b == 0 else 1

        for c in range(n_cols):
            cols = slice(c * LANE, (c + 1) * LANE)
            q2 = q_ref[rows, cols]
            kcat = jnp.concatenate([kprev_ref[:, cols] if b == 0 else k_ref[before, cols],
                                    k_ref[rows, cols]], axis=0)
            for half in range(2):
                h = 2 * c + half
                s = lax.dot_general(q2 * hm_ref[half:half + 1, :], kcat, (((1,), (1,)), ((), ())),
                                    preferred_element_type=F32) + bias_ref[variant, h]
                s_ref[b, h] = s
                m_ref[b, h] = jnp.broadcast_to(jnp.max(s, axis=-1, keepdims=True), (BLK, LANE))

        for h in range(N_HEADS):
            m = m_ref[b, h]
            p_ref[b, h] = jnp.exp(s_ref[b, h] - jnp.concatenate([m, m], axis=-1)).astype(p_ref.dtype)

        m_tile = jnp.zeros((BLK, LANE), F32)
        l_tile = jnp.ones((BLK, LANE), F32)
        for c in range(n_cols):
            cols = slice(c * LANE, (c + 1) * LANE)
            vcat = jnp.concatenate([vprev_ref[:, cols] if b == 0 else v_ref[before, cols],
                                    v_ref[rows, cols]], axis=0)
            acc = []
            for half in range(2):
                rhs = jnp.concatenate([vcat * hm_ref[half:half + 1, :], ones], axis=-1)
                acc.append(jnp.dot(p_ref[b, 2 * c + half], rhs, preferred_element_type=F32))
            den = jnp.where(low, acc[0][:, LANE:], acc[1][:, LANE:])
            o_ref[rows, cols] = ((acc[0][:, :LANE] + acc[1][:, :LANE]) / den).astype(o_ref.dtype)
            for half in range(2):
                h = 2 * c + half
                m_tile = jnp.where(lane == h, m_ref[b, h], m_tile)
                l_tile = jnp.where(lane == h, acc[half][:, LANE:], l_tile)
        lse_ref[rows, :] = m_tile + jnp.log(l_tile)

    last = slice(q_ref.shape[0] - BLK, q_ref.shape[0])
    kprev_ref[...] = k_ref[last, :]
    vprev_ref[...] = v_ref[last, :]


def dilated_attention_group(q, k, v, bias):
    r, l, w = q.shape
    nb = min(ATTN_BLOCKS_PER_STEP, l // BLK)
    rows = nb * BLK
    assert l % rows == 0, (l, rows)
    blk = pl.BlockSpec((None, rows, w), lambda a, i: (a, i, 0))
    lane = np.arange(LANE)[None, :] < HEAD_DIM
    head_mask = jnp.asarray(np.concatenate([lane, ~lane]).astype(np.float32), BF16)
    return pl.pallas_call(
        _attn_kernel,
        out_shape=(jax.ShapeDtypeStruct((r, l, w), BF16),
                   jax.ShapeDtypeStruct((r, l, LANE), F32)),
        grid=(r, l // rows),
        in_specs=[blk, blk, blk, _resident(bias.shape), _resident((2, LANE))],
        out_specs=(blk, pl.BlockSpec((None, rows, LANE), lambda a, i: (a, i, 0))),
        scratch_shapes=[pltpu.VMEM((BLK, w), BF16), pltpu.VMEM((BLK, w), BF16),
                        pltpu.VMEM((nb, N_HEADS, BLK, 2 * BLK), F32),
                        pltpu.VMEM((nb, N_HEADS, BLK, 2 * BLK), BF16),
                        pltpu.VMEM((nb, N_HEADS, BLK, LANE), F32)],
        compiler_params=_params("parallel", "arbitrary"),
        name="dilated_attn",
    )(q, k, v, bias, head_mask)


def _attn_out_kernel(x_ref, o0_ref, o1_ref, o2_ref, l0_ref, l1_ref, l2_ref, e_ref, w_ref, o_ref,
                     on_ref, ln_ref):
    n_cols = on_ref.shape[1]
    for gi, (og_ref, lg_ref) in enumerate(((o1_ref, l1_ref), (o2_ref, l2_ref))):
        dil, n, _ = og_ref.shape
        for r in range(dil):
            ln_ref[gi, pl.ds(r, n, stride=dil), :] = lg_ref[r]
            for c in range(n_cols):
                on_ref[gi, c, pl.ds(r, n, stride=dil), :] = (
                    og_ref[r, :, c * LANE:(c + 1) * LANE].astype(F32))
    l0, l1, l2 = l0_ref[0], ln_ref[0], ln_ref[1]
    m = jnp.maximum(jnp.maximum(l0, l1), l2)
    e0, e1, e2 = jnp.exp(l0 - m), jnp.exp(l1 - m), jnp.exp(l2 - m)
    den = e0 + e1 + e2
    acc = None
    token_order = [jnp.concatenate([on_ref[gi, c] for c in range(n_cols)], axis=1) for gi in range(2)]
    for e, o in ((e0, o0_ref[0].astype(F32)), (e1, token_order[0]), (e2, token_order[1])):
        wgt = jnp.dot((e / den).astype(BF16), e_ref[...], preferred_element_type=F32)
        term = wgt * o
        acc = term if acc is None else acc + term
    o_ref[...] = x_ref[...] + jnp.dot(acc.astype(BF16), w_ref[...], preferred_element_type=F32)


def attn_out(x3, outs, lses, w_o):
    b, s, d = x3.shape
    tm = ROW_TILE
    expand = np.zeros((LANE, d), np.float32)
    expand[:N_HEADS] = np.repeat(np.eye(N_HEADS, dtype=np.float32), HEAD_DIM, axis=1)
    expand = jnp.asarray(expand, BF16)
    row = pl.BlockSpec((None, tm, d), lambda bi, j: (bi, j, 0))
    o_specs = [pl.BlockSpec((None, dil, tm // dil, d), lambda bi, j: (bi, 0, j, 0))
               for _, dil in DILATED_GROUPS]
    l_specs = [pl.BlockSpec((None, dil, tm // dil, LANE), lambda bi, j: (bi, 0, j, 0))
               for _, dil in DILATED_GROUPS]
    return pl.pallas_call(
        _attn_out_kernel,
        out_shape=jax.ShapeDtypeStruct((b, s, d), F32),
        grid=(b, s // tm),
        in_specs=[row, *o_specs, *l_specs, _resident((LANE, d)), _resident((d, d))],
        out_specs=row,
        scratch_shapes=[pltpu.VMEM((2, d // LANE, tm, LANE), F32), pltpu.VMEM((2, tm, LANE), F32)],
        compiler_params=_params("parallel", "parallel"),
        name="attn_out",
    )(x3, *outs, *lses, expand, w_o)


def _router_kernel(x_ref, g_ref, w_ref, tri_ref, col_ref, row_ref, cnt_ref):
    h = _rms(x_ref[...], g_ref[...])
    w = w_ref[...]
    h_hi = h.astype(BF16)
    h_lo = (h - h_hi.astype(F32)).astype(BF16)
    w_hi = w.astype(BF16)
    w_lo = (w - w_hi.astype(F32)).astype(BF16)
    logits = (jnp.dot(h_hi, w_hi, preferred_element_type=F32)
              + jnp.dot(h_lo, w_hi, preferred_element_type=F32)
              + jnp.dot(h_hi, w_lo, preferred_element_type=F32))
    ne = N_EXPERTS
    lane = lax.broadcasted_iota(jnp.int32, logits.shape, 1)
    logits = jnp.where(lane < ne, logits, -jnp.inf)
    m1 = jnp.max(logits, axis=-1, keepdims=True)
    i1 = jnp.min(jnp.where(logits == m1, lane, ne), axis=-1, keepdims=True)
    rest = jnp.where(lane == i1, -jnp.inf, logits)
    m2 = jnp.max(rest, axis=-1, keepdims=True)
    i2 = jnp.min(jnp.where(rest == m2, lane, ne), axis=-1, keepdims=True)
    e2 = jnp.exp(m2 - m1)
    g1 = 1.0 / (1.0 + e2)
    g2 = e2 / (1.0 + e2)
    both = (lane == i1) | (lane == i2)
    cnt = jnp.dot(tri_ref[...], jnp.where(both, 1.0, 0.0).astype(BF16), preferred_element_type=F32)
    r1 = jnp.sum(jnp.where(lane == i1, cnt, 0.0), axis=-1, keepdims=True)
    r2 = jnp.sum(jnp.where(lane == i2, cnt, 0.0), axis=-1, keepdims=True)
    cnt_ref[...] = jnp.sum(jnp.where(both, 1, 0), axis=0, keepdims=True)
    fields = (i1.astype(F32), i2.astype(F32), r1, r2, g1, g2)
    meta = jnp.zeros(logits.shape, F32)
    for f, val in enumerate(fields):
        meta = jnp.where(lane == f, val, meta)
    col_ref[...] = meta[:, :META_FIELDS]
    row_ref[...] = meta.T[:META_FIELDS, :]


def router(x2, g, w_router):
    t, d = x2.shape
    tt = MOE_TOKEN_TILE
    nt = t // tt
    tri = jnp.asarray(np.tril(np.ones((tt, tt), np.float32), -1), BF16)
    return pl.pallas_call(
        _router_kernel,
        out_shape=(jax.ShapeDtypeStruct((t, META_FIELDS), F32),
                   jax.ShapeDtypeStruct((nt, META_FIELDS, tt), F32),
                   jax.ShapeDtypeStruct((nt, 1, LANE), jnp.int32)),
        grid=(nt,),
        in_specs=[pl.BlockSpec((tt, d), lambda i: (i, 0)), _resident((1, d)),
                  _resident((d, LANE)), _resident((tt, tt))],
        out_specs=(pl.BlockSpec((tt, META_FIELDS), lambda i: (i, 0)),
                   pl.BlockSpec((None, META_FIELDS, tt), lambda i: (i, 0, 0)),
                   pl.BlockSpec((None, 1, LANE), lambda i: (i, 0, 0))),
        compiler_params=_params("parallel"),
        name="moe_router",
    )(x2, g, _pad_cols(w_router, LANE), tri)


def moe_plan(counts, n_tokens):
    tm = MOE_ROW_TILE
    counts = (counts + SUBLANES - 1) // SUBLANES * SUBLANES
    experts_ = jnp.arange(N_EXPERTS, dtype=jnp.int32)
    total = jnp.sum(counts, axis=0)
    owned = (total + MOE_TOKEN_TILE + tm - 1) // tm
    owned_end = jnp.cumsum(owned)
    region = (owned_end - owned) * tm
    base = region[None, :] + jnp.cumsum(counts, axis=0) - counts
    local = jnp.cumsum(counts, axis=1) - counts
    max_rows = 2 * n_tokens + (SUBLANES - 1) * N_EXPERTS * counts.shape[0] + N_EXPERTS * MOE_TOKEN_TILE
    n_tiles = -(-max_rows // tm) + N_EXPERTS + 1
    tile_id = jnp.arange(n_tiles, dtype=jnp.int32)
    tile_expert = jnp.minimum(jnp.sum(tile_id[:, None] >= owned_end[None, :], axis=1), N_EXPERTS - 1)
    pick = tile_expert[:, None] == experts_[None, :]
    rows_before = (tile_id - jnp.sum(jnp.where(pick, (owned_end - owned)[None, :], 0), axis=1)) * tm
    rows_left = jnp.sum(jnp.where(pick, total[None, :], 0), axis=1) - rows_before
    tile_rows = jnp.where(tile_id < owned_end[-1], jnp.clip(rows_left, 0, tm), 0).astype(jnp.int32)
    return dict(n_tiles=n_tiles, counts=counts.reshape(-1), base=base.reshape(-1).astype(jnp.int32),
                local=local.reshape(-1).astype(jnp.int32), tail=(region + total).astype(jnp.int32),
                tile_expert=tile_expert.astype(jnp.int32), tile_rows=tile_rows,
                tile_valid=(tile_rows > 0).astype(jnp.int32))


def _run_windows(n):
    return (((n > 0) & (n <= MOE_DISPATCH_WINDOW), MOE_DISPATCH_WINDOW),
            (n > MOE_DISPATCH_WINDOW, MOE_TOKEN_TILE))


def _dispatch_kernel(cnt_ref, local_ref, base_ref, tail_ref, valid_ref, x_ref, g_ref, meta_ref, xs_ref,
                     stage_ref, sem):
    i = pl.program_id(0)
    tt, d = x_ref.shape
    n_sorted = 2 * tt + SUBLANES * N_EXPERTS
    n_zero = stage_ref.shape[1] - n_sorted
    tm = MOE_ROW_TILE

    @pl.when(i == 0)
    def _():
        for side in range(2):
            stage_ref[side, n_sorted:, :] = jnp.zeros((n_zero, stage_ref.shape[2]), F32)

        def zero_copy(row, rows):
            return pltpu.make_async_copy(stage_ref.at[0, pl.ds(n_sorted, rows)],
                                         xs_ref.at[pl.ds(pl.multiple_of(row, SUBLANES), rows)], sem)

        pieces = [n_zero >> b for b in range((n_zero // SUBLANES).bit_length())]
        for action in ("start", "wait"):
            def empty_tile(k, carry, action=action):
                @pl.when(valid_ref[k] == 0)
                def _():
                    for part in range(tm // n_zero):
                        getattr(zero_copy(k * tm + part * n_zero, n_zero), action)()
                return carry

            lax.fori_loop(0, valid_ref.shape[0], empty_tile, 0)
            for e in range(N_EXPERTS):
                row = tail_ref[e]
                left = (-row) & (tm - 1)
                for rows in pieces:
                    @pl.when((left & rows) != 0)
                    def _(row=row, rows=rows):
                        getattr(zero_copy(row, rows), action)()
                    row = row + (left & rows)

    h = _rms(x_ref[...], g_ref[...]).astype(BF16)
    meta = meta_ref[...]
    e1, e2, r1, r2, g1, g2 = (meta[f:f + 1, :] for f in range(6))
    off1 = jnp.zeros_like(r1)
    off2 = jnp.zeros_like(r2)
    for e in range(N_EXPERTS):
        lo = local_ref[i * N_EXPERTS + e].astype(F32)
        off1 = jnp.where(e1 == e, lo, off1)
        off2 = jnp.where(e2 == e, lo, off2)
    slot = lax.broadcasted_iota(jnp.int32, (n_sorted, tt), 0).astype(F32)
    hit1 = slot == r1 + off1
    hit2 = slot == r2 + off2
    perm = jnp.where(hit1 | hit2, 1.0, 0.0).astype(BF16)
    half = i % 2
    stage_ref[half, 0:n_sorted, 0:d] = jnp.dot(perm, h, preferred_element_type=F32)
    gate = jnp.sum(jnp.where(hit1, g1, 0.0) + jnp.where(hit2, g2, 0.0), axis=-1, keepdims=True)
    stage_ref[half, 0:n_sorted, d:] = jnp.broadcast_to(gate, (n_sorted, LANE))

    def run_copies(step, action):
        for e in range(N_EXPERTS):
            k = step * N_EXPERTS + e
            for cond, rows in _run_windows(cnt_ref[k]):
                @pl.when(cond)
                def _(k=k, rows=rows):
                    copy = pltpu.make_async_copy(
                        stage_ref.at[step % 2, pl.ds(pl.multiple_of(local_ref[k], SUBLANES), rows)],
                        xs_ref.at[pl.ds(pl.multiple_of(base_ref[k], SUBLANES), rows)], sem)
                    getattr(copy, action)()

    @pl.when(i > 0)
    def _():
        run_copies(i - 1, "wait")

    run_copies(i, "start")

    @pl.when(i == pl.num_programs(0) - 1)
    def _():
        run_copies(i, "wait")


def dispatch(x2, g, meta_row, plan):
    t, d = x2.shape
    tt = MOE_TOKEN_TILE
    grid_spec = pltpu.PrefetchScalarGridSpec(
        num_scalar_prefetch=5,
        grid=(t // tt,),
        in_specs=[pl.BlockSpec((tt, d), lambda i, *_: (i, 0)),
                  pl.BlockSpec((1, d), lambda i, *_: (0, 0)),
                  pl.BlockSpec((None, META_FIELDS, tt), lambda i, *_: (i, 0, 0))],
        out_specs=pl.BlockSpec(memory_space=pl.ANY),
        scratch_shapes=[pltpu.VMEM((2, 2 * tt + SUBLANES * N_EXPERTS + MOE_TOKEN_TILE, d + LANE), F32),
                        pltpu.SemaphoreType.DMA(())],
    )
    return pl.pallas_call(
        _dispatch_kernel,
        out_shape=jax.ShapeDtypeStruct((plan["n_tiles"] * MOE_ROW_TILE, d + LANE), F32),
        grid_spec=grid_spec,
        compiler_params=_params("arbitrary"),
        name="moe_dispatch",
    )(plan["counts"], plan["local"], plan["base"], plan["tail"], plan["tile_valid"], x2, g, meta_row)


def _experts_kernel(te_ref, rows_ref, x_ref, wg_hbm, wu_hbm, wd_hbm, o_ref, wg_ref, wu_ref, wd_ref, sems):
    i = pl.program_id(0)
    n = rows_ref[i]
    expert = te_ref[i]
    fresh = (i == 0) | (te_ref[jnp.maximum(i - 1, 0)] != expert)
    tm, d = o_ref.shape
    f = wg_ref.shape[-1]
    chunks = [slice(c0, min(c0 + MOE_FF_CHUNK, f)) for c0 in range(0, f, MOE_FF_CHUNK)]

    def chunk_copies(c):
        cols = chunks[c]
        return (pltpu.make_async_copy(wg_hbm.at[expert, :, cols], wg_ref.at[:, cols], sems.at[c, 0]),
                pltpu.make_async_copy(wu_hbm.at[expert, :, cols], wu_ref.at[:, cols], sems.at[c, 1]),
                pltpu.make_async_copy(wd_hbm.at[expert, cols, :], wd_ref.at[cols, :], sems.at[c, 2]))

    @pl.when(fresh)
    def _():
        for c in range(len(chunks)):
            for copy in chunk_copies(c):
                copy.start()

    def arrive(c):
        @pl.when(fresh)
        def _():
            for copy in chunk_copies(c):
                copy.wait()

    def ffn(r0, nrows, wait_weights=False):
        rows = slice(r0, r0 + nrows)
        xb = x_ref[rows, :d].astype(BF16)
        acc = None
        for c, cols in enumerate(chunks):
            if wait_weights:
                arrive(c)
            a = jnp.dot(xb, wg_ref[:, cols], preferred_element_type=F32)
            u = jnp.dot(xb, wu_ref[:, cols], preferred_element_type=F32)
            act = (a * jax.nn.sigmoid(a) * u).astype(BF16)
            part = jnp.dot(act, wd_ref[cols, :], preferred_element_type=F32)
            acc = part if acc is None else acc + part
        o_ref[rows, :] = acc * x_ref[rows, d:d + 1]

    last_piece = tm - MOE_SUB_ROWS

    @pl.when(n > last_piece)
    def _():
        ffn(0, tm, wait_weights=True)

    @pl.when(n <= last_piece)
    def _():
        for c in range(len(chunks)):
            arrive(c)
        for r0 in range(0, last_piece, MOE_SUB_ROWS):
            @pl.when(n > r0)
            def _(r0=r0):
                ffn(r0, MOE_SUB_ROWS)

            @pl.when(n <= r0)
            def _(r0=r0):
                o_ref[r0:r0 + MOE_SUB_ROWS, :] = jnp.zeros((MOE_SUB_ROWS, d), o_ref.dtype)
        o_ref[last_piece:, :] = jnp.zeros((MOE_SUB_ROWS, d), o_ref.dtype)


def experts(xs, plan, w_gate, w_up, w_down):
    n_rows, dw = xs.shape
    d = dw - LANE
    f = w_gate.shape[-1]
    tm = MOE_ROW_TILE
    n_chunks = -(-f // MOE_FF_CHUNK)
    in_hbm = pl.BlockSpec(memory_space=pl.ANY)
    grid_spec = pltpu.PrefetchScalarGridSpec(
        num_scalar_prefetch=2,
        grid=(n_rows // tm,),
        in_specs=[pl.BlockSpec((tm, dw), lambda i, te, tv: (i, 0)), in_hbm, in_hbm, in_hbm],
        out_specs=pl.BlockSpec((tm, d), lambda i, te, tv: (i, 0)),
        scratch_shapes=[pltpu.VMEM((d, f), BF16), pltpu.VMEM((d, f), BF16), pltpu.VMEM((f, d), BF16),
                        pltpu.SemaphoreType.DMA((n_chunks, 3))],
    )
    return pl.pallas_call(
        _experts_kernel,
        out_shape=jax.ShapeDtypeStruct((n_rows, d), F32),
        grid_spec=grid_spec,
        compiler_params=_params("arbitrary"),
        name="moe_experts",
    )(plan["tile_expert"], plan["tile_rows"], xs, w_gate, w_up, w_down)


def _combine_kernel(cnt_ref, base_ref, x_ref, meta_ref, ys_ref, o_ref, small_ref, wide_ref, sems):
    i = pl.program_id(0)
    tt = x_ref.shape[0]
    meta = meta_ref[...]
    e1, e2, r1, r2 = (meta[:, f:f + 1] for f in range(4))
    widest = cnt_ref[i * N_EXPERTS]
    for e in range(1, N_EXPERTS):
        widest = jnp.maximum(widest, cnt_ref[i * N_EXPERTS + e])

    def small_copies(step):
        return [pltpu.make_async_copy(
            ys_ref.at[pl.ds(pl.multiple_of(base_ref[step * N_EXPERTS + e], SUBLANES), MOE_SMALL_WINDOW)],
            small_ref.at[step % 2, e], sems.at[step % 2]) for e in range(N_EXPERTS)]

    @pl.when(i == 0)
    def _():
        for c in small_copies(i):
            c.start()

    @pl.when(i + 1 < pl.num_programs(0))
    def _():
        for c in small_copies(i + 1):
            c.start()

    for c in small_copies(i):
        c.wait()

    def pick_rows(window_of):
        rows = window_of(0).shape[0]
        pos = lax.broadcasted_iota(jnp.int32, (tt, rows), 1).astype(F32)
        acc = x_ref[...]
        for e in range(N_EXPERTS):
            want1 = jnp.where(e1 == e, r1, -1.0)
            want2 = jnp.where(e2 == e, r2, -1.0)
            pick = jnp.where((pos == want1) | (pos == want2), 1.0, 0.0).astype(BF16)
            acc = acc + jnp.dot(pick, window_of(e).astype(BF16), preferred_element_type=F32)
        o_ref[...] = acc

    @pl.when(widest <= MOE_SMALL_WINDOW)
    def _():
        pick_rows(lambda e: small_ref[i % 2, e])

    @pl.when(widest > MOE_SMALL_WINDOW)
    def _():
        copies = [pltpu.make_async_copy(
            ys_ref.at[pl.ds(pl.multiple_of(base_ref[i * N_EXPERTS + e], SUBLANES), tt)],
            wide_ref.at[e], sems.at[2]) for e in range(N_EXPERTS)]
        for c in copies:
            c.start()
        for c in copies:
            c.wait()
        pick_rows(lambda e: wide_ref[e])


def combine(x2, meta_col, ys, plan):
    t, d = x2.shape
    tt = MOE_TOKEN_TILE
    grid_spec = pltpu.PrefetchScalarGridSpec(
        num_scalar_prefetch=2,
        grid=(t // tt,),
        in_specs=[pl.BlockSpec((tt, d), lambda i, *_: (i, 0)),
                  pl.BlockSpec((tt, META_FIELDS), lambda i, *_: (i, 0)),
                  pl.BlockSpec(memory_space=pl.ANY)],
        out_specs=pl.BlockSpec((tt, d), lambda i, *_: (i, 0)),
        scratch_shapes=[pltpu.VMEM((2, N_EXPERTS, MOE_SMALL_WINDOW, d), F32),
                        pltpu.VMEM((N_EXPERTS, tt, d), F32), pltpu.SemaphoreType.DMA((3,))],
    )
    return pl.pallas_call(
        _combine_kernel,
        out_shape=jax.ShapeDtypeStruct((t, d), F32),
        grid_spec=grid_spec,
        compiler_params=_params("arbitrary"),
        name="moe_combine",
    )(plan["counts"], plan["base"], x2, meta_col, ys)


def _pad_cols(w, mult):
    pad = (-w.shape[-1]) % mult
    return jnp.pad(w, ((0, 0),) * (w.ndim - 1) + ((0, pad),))


def kernel(x, rel_bias, conv_norm, conv_w_in, conv_b_in, conv_dw_w, conv_dw_b, conv_ln_g, conv_ln_b, conv_w_out, conv_b_out, kv_norm, w_kv, k_norm, attn_norm, w_q, q_norm, w_o, ffn_norm, ffn_w_gate, ffn_w_up, ffn_w_down, moe_router, moe_w_gate, moe_w_up, moe_w_down):
    b, s, d = x.shape
    t = b * s
    row = lambda v: v.reshape(1, -1)

    x2 = x.reshape(t, d)
    ff_pad = (-ffn_w_gate.shape[-1]) % LANE
    u, (w_conv_out, wg, wu, wq, wkv, wo) = conv_in(
        x2, row(conv_norm[0]), conv_w_in[0].astype(BF16), row(conv_b_in[0]),
        [conv_w_out[0], ffn_w_gate[0], ffn_w_up[0], w_q[0], w_kv, w_o[0]], [0, ff_pad, ff_pad, 0, 0, 0])
    n_e, _, f_e = moe_w_gate[0].shape
    x3, (we_gate, we_up, we_down) = conv_out(
        u.reshape(b, s, d), x, conv_dw_w[0], row(conv_dw_b[0]), row(conv_ln_g[0]), row(conv_ln_b[0]),
        w_conv_out, row(conv_b_out[0]),
        [moe_w_gate[0].reshape(n_e * d, f_e), moe_w_up[0].reshape(n_e * d, f_e),
         moe_w_down[0].reshape(n_e * f_e, d)])
    x2 = x3.reshape(t, d)
    wd = jnp.pad(ffn_w_down[0], ((0, ff_pad), (0, 0))).astype(BF16)
    x2 = ffn_dense(x2, row(ffn_norm[0]), wg, wu, wd)

    qn = row(jnp.tile(q_norm[0] * (HEAD_DIM ** -0.5), (1, N_HEADS)))
    kn = row(jnp.tile(k_norm, (1, N_HEADS)))
    x3 = x2.reshape(b, s, d)
    qkv = qkv_proj(x3, row(attn_norm[0]), row(kv_norm), wq, wkv, qn, kn)
    outs, lses = [], []
    for g, (_, dil) in enumerate(DILATED_GROUPS):
        qr, kr, vr = (a.reshape(b * dil, s // dil, a.shape[-1]) for a in qkv[3 * g:3 * g + 3])
        o_g, lse_g = dilated_attention_group(qr, kr, vr, band_bias(rel_bias, g))
        outs.append(o_g.reshape(b, dil, s // dil, o_g.shape[-1]))
        lses.append(lse_g.reshape(b, dil, s // dil, LANE))
    x2 = attn_out(x3, outs, lses, wo).reshape(t, d)

    g_moe = row(ffn_norm[1])
    meta_col, meta_row, counts = router(x2, g_moe, moe_router[0])
    plan = moe_plan(counts[:, 0, :N_EXPERTS], t)
    xs = dispatch(x2, g_moe, meta_row, plan)
    ys = experts(xs, plan, we_gate.reshape(n_e, d, f_e), we_up.reshape(n_e, d, f_e),
                 we_down.reshape(n_e, f_e, d))
    return combine(x2, meta_col, ys, plan).reshape(b, s, d)
```

```python
import functools

import numpy as np
import jax
import jax.numpy as jnp
from jax import lax
from jax.experimental import pallas as pl
from jax.experimental.pallas import tpu as pltpu

F32 = jnp.float32
BF16 = jnp.bfloat16

EPS = 1e-6
NEG = -1e30
CONV_WIDTH = 31
CONV_HALO = 32
N_HEADS = 16
HEAD_DIM = 64
DILATED_GROUPS = ((128, 1), (512, 4), (2048, 16))
BLK = 128
ATTN_BLOCKS_PER_STEP = 4
N_BUCKETS = 32
MAX_DISTANCE = 2048
N_EXPERTS = 8
LANE = 128
SUBLANES = 8
VMEM_LIMIT_BYTES = 56 * 1024 * 1024

ROW_TILE = 512
CONV_CHUNK = 64
CONV_SPAN = CONV_CHUNK + SUBLANES * ((CONV_WIDTH - 1) // SUBLANES)
CONV_WINDOW = 128
MOE_ROW_TILE = 1024
MOE_FF_CHUNK = 1024
MOE_SUB_ROWS = 256
MOE_TOKEN_TILE = 512
MOE_DISPATCH_WINDOW = 160
MOE_SMALL_WINDOW = 256
META_FIELDS = 8


def _params(*sem):
    return pltpu.CompilerParams(dimension_semantics=sem, vmem_limit_bytes=VMEM_LIMIT_BYTES)


def _resident(shape):
    nd = len(shape)
    return pl.BlockSpec(shape, lambda *_: (0,) * nd, pipeline_mode=pl.Buffered(1))


def _rms(x, g):
    return x * lax.rsqrt(jnp.mean(x * x, axis=-1, keepdims=True) + EPS) * g


class _WeightCast:
    def __init__(self, weights, pads, steps, index):
        self.shapes = [w.shape for w in weights]
        self.pads = pads
        self.inputs, self.in_specs, self.out_shape, self.out_specs = [], [], [], []
        for w, pad in zip(weights, pads):
            rows, cols = w.shape
            assert rows % steps == 0, (rows, steps)
            block = lambda *grid: (index(*grid), 0, 0)
            self.inputs.append(w.reshape(steps, rows // steps, cols))
            self.in_specs.append(pl.BlockSpec((None, rows // steps, cols), block))
            self.out_shape.append(jax.ShapeDtypeStruct((steps, rows // steps, cols + pad), BF16))
            self.out_specs.append(pl.BlockSpec((None, rows // steps, cols + pad), block))

    @staticmethod
    def run(src_refs, dst_refs):
        for src, dst in zip(src_refs, dst_refs):
            cols = src.shape[-1]
            dst[:, :cols] = src[...].astype(dst.dtype)
            if dst.shape[-1] > cols:
                dst[:, cols:] = jnp.zeros((dst.shape[0], dst.shape[-1] - cols), dst.dtype)

    def results(self, outs):
        return [o.reshape(rows, cols + pad) for o, (rows, cols), pad in zip(outs, self.shapes, self.pads)]


def _conv_in_kernel(x_ref, g_ref, w_ref, b_ref, *refs):
    n_cast = (len(refs) - 1) // 2
    u_ref = refs[n_cast]
    h = _rms(x_ref[...], g_ref[...])
    y = jnp.dot(h.astype(BF16), w_ref[...], preferred_element_type=F32) + b_ref[...]
    d = u_ref.shape[-1]
    u_ref[...] = (y[:, :d] * jax.nn.sigmoid(y[:, d:])).astype(u_ref.dtype)
    _WeightCast.run(refs[:n_cast], refs[n_cast + 1:])


def conv_in(x2, g, w_in, b_in, later_weights, pads):
    t, d = x2.shape
    steps = t // ROW_TILE
    cast = _WeightCast(later_weights, pads, steps, lambda i: i)
    outs = pl.pallas_call(
        _conv_in_kernel,
        out_shape=(jax.ShapeDtypeStruct((t, d), BF16), *cast.out_shape),
        grid=(steps,),
        in_specs=[pl.BlockSpec((ROW_TILE, d), lambda i: (i, 0)),
                  _resident((1, d)), _resident((d, 2 * d)), _resident((1, 2 * d)), *cast.in_specs],
        out_specs=(pl.BlockSpec((ROW_TILE, d), lambda i: (i, 0)), *cast.out_specs),
        compiler_params=_params("parallel"),
        name="conv_in",
    )(x2, g, w_in, b_in, *cast.inputs)
    return outs[0], cast.results(outs[1:])


def _conv_out_kernel(u_ref, x_ref, dww_ref, dwb_ref, lng_ref, lnb_ref, w_ref, b_ref, shift_ref, *refs):
    n_cast = (len(refs) - 3) // 2
    o_ref = refs[n_cast]
    buf_ref, conv_ref = refs[-2:]
    _WeightCast.run(refs[:n_cast], refs[n_cast + 1:-2])
    ts, d = x_ref.shape
    j = pl.program_id(1)

    @pl.when(j == 0)
    def _():
        buf_ref[0:CONV_HALO, :] = jnp.zeros((CONV_HALO, d), buf_ref.dtype)
        buf_ref[CONV_HALO + ts:, :] = jnp.zeros((buf_ref.shape[0] - CONV_HALO - ts, d), buf_ref.dtype)

    @pl.when(j > 0)
    def _():
        buf_ref[0:CONV_HALO, :] = buf_ref[ts:ts + CONV_HALO, :]

    buf_ref[CONV_HALO:CONV_HALO + ts, :] = u_ref[...]

    def lane_chunk(c, carry):
        lanes = pl.ds(pl.multiple_of(c * LANE, LANE), LANE)
        taps = [dww_ref[k:k + 1, lanes] for k in range(CONV_WIDTH)]
        bias = dwb_ref[:, lanes]
        for r0 in range(0, ts, CONV_CHUNK):
            shifted = jnp.dot(shift_ref[...], buf_ref[r0:r0 + CONV_WINDOW, lanes],
                              preferred_element_type=F32)
            acc = jnp.broadcast_to(bias, (CONV_CHUNK, LANE))
            for phase in range(SUBLANES):
                for a, k in enumerate(range(phase, CONV_WIDTH, SUBLANES)):
                    lo = phase * CONV_SPAN + SUBLANES * a
                    acc = acc + taps[k] * shifted[lo:lo + CONV_CHUNK]
            conv_ref[r0:r0 + CONV_CHUNK, lanes] = acc
        return carry

    lax.fori_loop(0, d // LANE, lane_chunk, 0)
    acc = conv_ref[...]
    mu = jnp.mean(acc, axis=-1, keepdims=True)
    cen = acc - mu
    var = jnp.mean(cen * cen, axis=-1, keepdims=True)
    y = cen * lax.rsqrt(var + EPS) * lng_ref[...] + lnb_ref[...]
    act = (y * jax.nn.sigmoid(y)).astype(BF16)
    o_ref[...] = x_ref[...] + b_ref[...] + jnp.dot(act, w_ref[...], preferred_element_type=F32)


def conv_out(u3, x3, dw_w, dw_b, ln_g, ln_b, w_out, b_out, later_weights):
    b, s, d = x3.shape
    ts = ROW_TILE
    nj = s // ts
    cast = _WeightCast(later_weights, [0] * len(later_weights), b * nj, lambda bi, j: bi * nj + j)
    row = pl.BlockSpec((None, ts, d), lambda bi, j: (bi, j, 0))
    first_tap = CONV_HALO - (CONV_WIDTH - 1)
    shift = np.zeros((SUBLANES * CONV_SPAN, CONV_WINDOW), np.float32)
    for p in range(SUBLANES):
        i = np.arange(CONV_SPAN)
        shift[p * CONV_SPAN + i, first_tap + p + i] = 1.0
    outs = pl.pallas_call(
        _conv_out_kernel,
        out_shape=(jax.ShapeDtypeStruct((b, s, d), F32), *cast.out_shape),
        grid=(b, nj),
        in_specs=[row, row, _resident((CONV_WIDTH, d)), _resident((1, d)), _resident((1, d)),
                  _resident((1, d)), _resident((d, d)), _resident((1, d)), _resident(shift.shape),
                  *cast.in_specs],
        out_specs=(row, *cast.out_specs),
        scratch_shapes=[pltpu.VMEM((ts - CONV_CHUNK + CONV_WINDOW, d), BF16), pltpu.VMEM((ts, d), F32)],
        compiler_params=_params("parallel", "arbitrary"),
        name="conv_out",
    )(u3, x3, dw_w, dw_b, ln_g, ln_b, w_out, b_out, jnp.asarray(shift, BF16), *cast.inputs)
    return outs[0], cast.results(outs[1:])


def _ffn_kernel(x_ref, g_ref, wg_ref, wu_ref, wd_ref, o_ref):
    x = x_ref[...]
    h = _rms(x, g_ref[...]).astype(BF16)
    a = jnp.dot(h, wg_ref[...], preferred_element_type=F32)
    u = jnp.dot(h, wu_ref[...], preferred_element_type=F32)
    act = (a * jax.nn.sigmoid(a) * u).astype(BF16)
    o_ref[...] = x + jnp.dot(act, wd_ref[...], preferred_element_type=F32)


def ffn_dense(x2, g, w_gate, w_up, w_down):
    t, d = x2.shape
    f = w_gate.shape[1]
    row = pl.BlockSpec((ROW_TILE, d), lambda i: (i, 0))
    return pl.pallas_call(
        _ffn_kernel,
        out_shape=jax.ShapeDtypeStruct((t, d), F32),
        grid=(t // ROW_TILE,),
        in_specs=[row, _resident((1, d)), _resident((d, f)), _resident((d, f)), _resident((f, d))],
        out_specs=row,
        compiler_params=_params("parallel"),
        name="ffn_dense",
    )(x2, g, w_gate, w_up, w_down)


def _store_heads(y, out_ref, gain_ref=None, gain_off=0):
    dil, n, _ = out_ref.shape
    low = lax.broadcasted_iota(jnp.int32, (1, LANE), 1) < HEAD_DIM
    for c in range(y.shape[-1] // LANE):
        cols = slice(c * LANE, (c + 1) * LANE)
        blk = y[:, cols]
        if gain_ref is not None:
            sq = blk * blk
            ms_lo = jnp.sum(jnp.where(low, sq, 0.0), axis=-1, keepdims=True) * (1.0 / HEAD_DIM)
            ms_hi = jnp.sum(jnp.where(low, 0.0, sq), axis=-1, keepdims=True) * (1.0 / HEAD_DIM)
            inv = jnp.where(low, lax.rsqrt(ms_lo + EPS), lax.rsqrt(ms_hi + EPS))
            blk = blk * inv * gain_ref[:, gain_off + c * LANE:gain_off + (c + 1) * LANE]
        blk = blk.astype(out_ref.dtype)
        for r in range(dil):
            out_ref[r, :, cols] = blk[r * n:(r + 1) * n]


def _qkv_kernel(x_ref, gq_ref, gkv_ref, wq_ref, wkv_ref, qn_ref, kn_ref, *refs):
    out_refs, xn_ref = refs[:-1], refs[-1]
    x = x_ref[...]
    xn = x * lax.rsqrt(jnp.mean(x * x, axis=-1, keepdims=True) + EPS)
    n_cols = xn_ref.shape[0]
    for c in range(n_cols):
        xn_ref[c] = xn[:, c * LANE:(c + 1) * LANE]
    tm = x.shape[0]
    hw = N_HEADS * HEAD_DIM
    qw = len(DILATED_GROUPS) * hw
    for g, (_, dil) in enumerate(DILATED_GROUPS):
        n = tm // dil
        xg = xn if dil == 1 else jnp.concatenate(
            [jnp.concatenate([xn_ref[c, pl.ds(r, n, stride=dil), :] for r in range(dil)], axis=0)
             for c in range(n_cols)], axis=1)
        hq = (xg * gq_ref[...]).astype(BF16)
        hkv = (xg * gkv_ref[...]).astype(BF16)
        q_ref, k_ref, v_ref = out_refs[3 * g:3 * g + 3]
        cols = slice(g * hw, (g + 1) * hw)
        _store_heads(jnp.dot(hq, wq_ref[:, cols], preferred_element_type=F32), q_ref, qn_ref, g * hw)
        _store_heads(jnp.dot(hkv, wkv_ref[:, cols], preferred_element_type=F32), k_ref, kn_ref, g * hw)
        _store_heads(jnp.dot(hkv, wkv_ref[:, qw + g * hw:qw + (g + 1) * hw],
                             preferred_element_type=F32), v_ref)


def qkv_proj(x3, gq, gkv, w_q, w_kv, qn, kn):
    b, s, d = x3.shape
    qw = w_q.shape[1]
    hw = N_HEADS * HEAD_DIM
    tm = ROW_TILE
    out_shape, out_specs = [], []
    for _, dil in DILATED_GROUPS:
        for _ in range(3):
            out_shape.append(jax.ShapeDtypeStruct((b, dil, s // dil, hw), BF16))
            out_specs.append(pl.BlockSpec((None, dil, tm // dil, hw), lambda bi, j: (bi, 0, j, 0)))
    return pl.pallas_call(
        _qkv_kernel,
        out_shape=tuple(out_shape),
        grid=(b, s // tm),
        in_specs=[pl.BlockSpec((None, tm, d), lambda bi, j: (bi, j, 0)), _resident((1, d)),
                  _resident((1, d)), _resident((d, qw)), _resident((d, 2 * qw)), _resident((1, qw)),
                  _resident((1, qw))],
        out_specs=tuple(out_specs),
        scratch_shapes=[pltpu.VMEM((d // LANE, tm, LANE), F32)],
        compiler_params=_params("parallel", "parallel"),
        name="qkv_proj",
    )(x3, gq, gkv, w_q, w_kv, qn, kn)


def _t5_bucket(dist):
    n = np.asarray(dist)
    max_exact = N_BUCKETS // 2
    large = max_exact + (np.log(np.maximum(n, 1) / max_exact)
                         / np.log(MAX_DISTANCE / max_exact)
                         * (N_BUCKETS - max_exact)).astype(np.int32)
    large = np.minimum(large, N_BUCKETS - 1)
    return np.where(n < max_exact, n, large).astype(np.int32)


def band_bias(rel_bias, g):
    window, d = DILATED_GROUPS[g]
    span = window // d
    period = 3 * BLK
    diff = np.arange(period)
    diff = np.where(diff >= 2 * BLK, diff - period, diff)
    sub = BLK - diff
    in_band = (sub >= 0) & (sub <= span)
    bucket = _t5_bucket(np.clip(sub, 0, span) * d)
    table = rel_bias[:, g * N_HEADS:(g + 1) * N_HEADS].astype(F32)
    onehot = jnp.asarray(bucket[:, None] == np.arange(N_BUCKETS)[None, :], F32)
    line = jnp.dot(onehot, table, precision=lax.Precision.HIGHEST).T
    line = jnp.where(jnp.asarray(in_band)[None, :], line, NEG)
    flat = jnp.tile(line, (1, BLK))[:, :BLK * (period - 1)]
    general = flat.reshape(N_HEADS, BLK, period - 1)[:, :, :2 * BLK]
    has_prev = jnp.asarray(np.arange(2 * BLK) >= BLK)[None, None, :]
    first = jnp.where(has_prev, general, NEG)
    return jnp.stack([first, general])


def _attn_kernel(q_ref, k_ref, v_ref, bias_ref, hm_ref, o_ref, lse_ref,
                 kprev_ref, vprev_ref, s_ref, p_ref, m_ref, *, seq_blocks):
    i = pl.program_id(1)

    @pl.when(i == 0)
    def _():
        kprev_ref[...] = jnp.zeros(kprev_ref.shape, kprev_ref.dtype)
        vprev_ref[...] = jnp.zeros(vprev_ref.shape, vprev_ref.dtype)

    n_cols = q_ref.shape[-1] // LANE
    lane = lax.broadcasted_iota(jnp.int32, (BLK, LANE), 1)
    low = lane < HEAD_DIM
    ones = jnp.ones((2 * BLK, LANE), BF16)

    for b in range(q_ref.shape[0] // BLK):
        rows = slice(b * BLK, (b + 1) * BLK)
        before = slice((b - 1) * BLK, b * BLK)
        variant = jnp.minimum(i, 1) if b == 0 else (0 if b % seq_blocks == 0 else 1)

        for c in range(n_cols):
            cols = slice(c * LANE, (c + 1) * LANE)
            q2 = q_ref[rows, cols]
            kcat = jnp.concatenate([kprev_ref[:, cols] if b == 0 else k_ref[before, cols],
                                    k_ref[rows, cols]], axis=0)
            for half in range(2):
                h = 2 * c + half
                s = lax.dot_general(q2 * hm_ref[half:half + 1, :], kcat, (((1,), (1,)), ((), ())),
                                    preferred_element_type=F32) + bias_ref[variant, h]
                s_ref[b, h] = s
                m_ref[b, h] = jnp.broadcast_to(jnp.max(s, axis=-1, keepdims=True), (BLK, LANE))

        for h in range(N_HEADS):
            m = m_ref[b, h]
            p_ref[b, h] = jnp.exp(s_ref[b, h] - jnp.concatenate([m, m], axis=-1)).astype(p_ref.dtype)

        m_tile = jnp.zeros((BLK, LANE), F32)
        l_tile = jnp.ones((BLK, LANE), F32)
        for c in range(n_cols):
            cols = slice(c * LANE, (c + 1) * LANE)
            vcat = jnp.concatenate([vprev_ref[:, cols] if b == 0 else v_ref[before, cols],
                                    v_ref[rows, cols]], axis=0)
            acc = []
            for half in range(2):
                rhs = jnp.concatenate([vcat * hm_ref[half:half + 1, :], ones], axis=-1)
                acc.append(jnp.dot(p_ref[b, 2 * c + half], rhs, preferred_element_type=F32))
            den = jnp.where(low, acc[0][:, LANE:], acc[1][:, LANE:])
            o_ref[rows, cols] = ((acc[0][:, :LANE] + acc[1][:, :LANE]) / den).astype(o_ref.dtype)
            for half in range(2):
                h = 2 * c + half
                m_tile = jnp.where(lane == h, m_ref[b, h], m_tile)
                l_tile = jnp.where(lane == h, acc[half][:, LANE:], l_tile)
        lse_ref[rows, :] = m_tile + jnp.log(l_tile)

    last = slice(q_ref.shape[0] - BLK, q_ref.shape[0])
    kprev_ref[...] = k_ref[last, :]
    vprev_ref[...] = v_ref[last, :]


def dilated_attention_group(q, k, v, bias):
    n_seq, seq_len, w = q.shape
    seq_blocks = seq_len // BLK
    merge = max(1, ATTN_BLOCKS_PER_STEP // seq_blocks)
    assert n_seq % merge == 0, (n_seq, merge)
    r, l = n_seq // merge, seq_len * merge
    q, k, v = (a.reshape(r, l, w) for a in (q, k, v))
    nb = min(ATTN_BLOCKS_PER_STEP, l // BLK)
    rows = nb * BLK
    assert l % rows == 0, (l, rows)
    blk = pl.BlockSpec((None, rows, w), lambda a, i: (a, i, 0))
    lane = np.arange(LANE)[None, :] < HEAD_DIM
    head_mask = jnp.asarray(np.concatenate([lane, ~lane]).astype(np.float32), BF16)
    o, lse = pl.pallas_call(
        functools.partial(_attn_kernel, seq_blocks=seq_blocks),
        out_shape=(jax.ShapeDtypeStruct((r, l, w), BF16),
                   jax.ShapeDtypeStruct((r, l, LANE), F32)),
        grid=(r, l // rows),
        in_specs=[blk, blk, blk, _resident(bias.shape), _resident((2, LANE))],
        out_specs=(blk, pl.BlockSpec((None, rows, LANE), lambda a, i: (a, i, 0))),
        scratch_shapes=[pltpu.VMEM((BLK, w), BF16), pltpu.VMEM((BLK, w), BF16),
                        pltpu.VMEM((nb, N_HEADS, BLK, 2 * BLK), F32),
                        pltpu.VMEM((nb, N_HEADS, BLK, 2 * BLK), BF16),
                        pltpu.VMEM((nb, N_HEADS, BLK, LANE), F32)],
        compiler_params=_params("parallel", "arbitrary"),
        name="dilated_attn",
    )(q, k, v, bias, head_mask)
    return o.reshape(n_seq, seq_len, w), lse.reshape(n_seq, seq_len, LANE)


def _attn_out_kernel(x_ref, o0_ref, o1_ref, o2_ref, l0_ref, l1_ref, l2_ref, e_ref, w_ref, o_ref,
                     on_ref, ln_ref):
    n_cols = on_ref.shape[1]
    for gi, (og_ref, lg_ref) in enumerate(((o1_ref, l1_ref), (o2_ref, l2_ref))):
        dil, n, _ = og_ref.shape
        for r in range(dil):
            ln_ref[gi, pl.ds(r, n, stride=dil), :] = lg_ref[r]
            for c in range(n_cols):
                on_ref[gi, c, pl.ds(r, n, stride=dil), :] = (
                    og_ref[r, :, c * LANE:(c + 1) * LANE].astype(F32))
    l0, l1, l2 = l0_ref[0], ln_ref[0], ln_ref[1]
    m = jnp.maximum(jnp.maximum(l0, l1), l2)
    e0, e1, e2 = jnp.exp(l0 - m), jnp.exp(l1 - m), jnp.exp(l2 - m)
    den = e0 + e1 + e2
    acc = None
    token_order = [jnp.concatenate([on_ref[gi, c] for c in range(n_cols)], axis=1) for gi in range(2)]
    for e, o in ((e0, o0_ref[0].astype(F32)), (e1, token_order[0]), (e2, token_order[1])):
        wgt = jnp.dot((e / den).astype(BF16), e_ref[...], preferred_element_type=F32)
        term = wgt * o
        acc = term if acc is None else acc + term
    o_ref[...] = x_ref[...] + jnp.dot(acc.astype(BF16), w_ref[...], preferred_element_type=F32)


def attn_out(x3, outs, lses, w_o):
    b, s, d = x3.shape
    tm = ROW_TILE
    expand = np.zeros((LANE, d), np.float32)
    expand[:N_HEADS] = np.repeat(np.eye(N_HEADS, dtype=np.float32), HEAD_DIM, axis=1)
    expand = jnp.asarray(expand, BF16)
    row = pl.BlockSpec((None, tm, d), lambda bi, j: (bi, j, 0))
    o_specs = [pl.BlockSpec((None, dil, tm // dil, d), lambda bi, j: (bi, 0, j, 0))
               for _, dil in DILATED_GROUPS]
    l_specs = [pl.BlockSpec((None, dil, tm // dil, LANE), lambda bi, j: (bi, 0, j, 0))
               for _, dil in DILATED_GROUPS]
    return pl.pallas_call(
        _attn_out_kernel,
        out_shape=jax.ShapeDtypeStruct((b, s, d), F32),
        grid=(b, s // tm),
        in_specs=[row, *o_specs, *l_specs, _resident((LANE, d)), _resident((d, d))],
        out_specs=row,
        scratch_shapes=[pltpu.VMEM((2, d // LANE, tm, LANE), F32), pltpu.VMEM((2, tm, LANE), F32)],
        compiler_params=_params("parallel", "parallel"),
        name="attn_out",
    )(x3, *outs, *lses, expand, w_o)


def _router_kernel(x_ref, g_ref, w_ref, tri_ref, col_ref, row_ref, cnt_ref):
    h = _rms(x_ref[...], g_ref[...])
    w = w_ref[...]
    h_hi = h.astype(BF16)
    h_lo = (h - h_hi.astype(F32)).astype(BF16)
    w_hi = w.astype(BF16)
    w_lo = (w - w_hi.astype(F32)).astype(BF16)
    logits = (jnp.dot(h_hi, w_hi, preferred_element_type=F32)
              + jnp.dot(h_lo, w_hi, preferred_element_type=F32)
              + jnp.dot(h_hi, w_lo, preferred_element_type=F32))
    ne = N_EXPERTS
    lane = lax.broadcasted_iota(jnp.int32, logits.shape, 1)
    logits = jnp.where(lane < ne, logits, -jnp.inf)
    m1 = jnp.max(logits, axis=-1, keepdims=True)
    i1 = jnp.min(jnp.where(logits == m1, lane, ne), axis=-1, keepdims=True)
    rest = jnp.where(lane == i1, -jnp.inf, logits)
    m2 = jnp.max(rest, axis=-1, keepdims=True)
    i2 = jnp.min(jnp.where(rest == m2, lane, ne), axis=-1, keepdims=True)
    e2 = jnp.exp(m2 - m1)
    g1 = 1.0 / (1.0 + e2)
    g2 = e2 / (1.0 + e2)
    both = (lane == i1) | (lane == i2)
    cnt = jnp.dot(tri_ref[...], jnp.where(both, 1.0, 0.0).astype(BF16), preferred_element_type=F32)
    r1 = jnp.sum(jnp.where(lane == i1, cnt, 0.0), axis=-1, keepdims=True)
    r2 = jnp.sum(jnp.where(lane == i2, cnt, 0.0), axis=-1, keepdims=True)
    cnt_ref[...] = jnp.sum(jnp.where(both, 1, 0), axis=0, keepdims=True)
    fields = (i1.astype(F32), i2.astype(F32), r1, r2, g1, g2)
    meta = jnp.zeros(logits.shape, F32)
    for f, val in enumerate(fields):
        meta = jnp.where(lane == f, val, meta)
    col_ref[...] = meta[:, :META_FIELDS]
    row_ref[...] = meta.T[:META_FIELDS, :]


def router(x2, g, w_router):
    t, d = x2.shape
    tt = MOE_TOKEN_TILE
    nt = t // tt
    tri = jnp.asarray(np.tril(np.ones((tt, tt), np.float32), -1), BF16)
    return pl.pallas_call(
        _router_kernel,
        out_shape=(jax.ShapeDtypeStruct((t, META_FIELDS), F32),
                   jax.ShapeDtypeStruct((nt, META_FIELDS, tt), F32),
                   jax.ShapeDtypeStruct((nt, 1, LANE), jnp.int32)),
        grid=(nt,),
        in_specs=[pl.BlockSpec((tt, d), lambda i: (i, 0)), _resident((1, d)),
                  _resident((d, LANE)), _resident((tt, tt))],
        out_specs=(pl.BlockSpec((tt, META_FIELDS), lambda i: (i, 0)),
                   pl.BlockSpec((None, META_FIELDS, tt), lambda i: (i, 0, 0)),
                   pl.BlockSpec((None, 1, LANE), lambda i: (i, 0, 0))),
        compiler_params=_params("parallel"),
        name="moe_router",
    )(x2, g, _pad_cols(w_router, LANE), tri)


def moe_plan(counts, n_tokens):
    tm = MOE_ROW_TILE
    counts = (counts + SUBLANES - 1) // SUBLANES * SUBLANES
    experts_ = jnp.arange(N_EXPERTS, dtype=jnp.int32)
    total = jnp.sum(counts, axis=0)
    owned = (total + MOE_TOKEN_TILE + tm - 1) // tm
    owned_end = jnp.cumsum(owned)
    region = (owned_end - owned) * tm
    base = region[None, :] + jnp.cumsum(counts, axis=0) - counts
    local = jnp.cumsum(counts, axis=1) - counts
    max_rows = 2 * n_tokens + (SUBLANES - 1) * N_EXPERTS * counts.shape[0] + N_EXPERTS * MOE_TOKEN_TILE
    n_tiles = -(-max_rows // tm) + N_EXPERTS + 1
    tile_id = jnp.arange(n_tiles, dtype=jnp.int32)
    tile_expert = jnp.minimum(jnp.sum(tile_id[:, None] >= owned_end[None, :], axis=1), N_EXPERTS - 1)
    pick = tile_expert[:, None] == experts_[None, :]
    rows_before = (tile_id - jnp.sum(jnp.where(pick, (owned_end - owned)[None, :], 0), axis=1)) * tm
    rows_left = jnp.sum(jnp.where(pick, total[None, :], 0), axis=1) - rows_before
    tile_rows = jnp.where(tile_id < owned_end[-1], jnp.clip(rows_left, 0, tm), 0).astype(jnp.int32)
    return dict(n_tiles=n_tiles, counts=counts.reshape(-1), base=base.reshape(-1).astype(jnp.int32),
                local=local.reshape(-1).astype(jnp.int32), tail=(region + total).astype(jnp.int32),
                tile_expert=tile_expert.astype(jnp.int32), tile_rows=tile_rows,
                tile_valid=(tile_rows > 0).astype(jnp.int32))


def _run_windows(n):
    return (((n > 0) & (n <= MOE_DISPATCH_WINDOW), MOE_DISPATCH_WINDOW),
            (n > MOE_DISPATCH_WINDOW, MOE_TOKEN_TILE))


def _dispatch_kernel(cnt_ref, local_ref, base_ref, tail_ref, valid_ref, x_ref, g_ref, meta_ref, xs_ref,
                     stage_ref, sem):
    i = pl.program_id(0)
    tt, d = x_ref.shape
    n_sorted = 2 * tt + SUBLANES * N_EXPERTS
    n_zero = stage_ref.shape[1] - n_sorted
    tm = MOE_ROW_TILE

    @pl.when(i == 0)
    def _():
        for side in range(2):
            stage_ref[side, n_sorted:, :] = jnp.zeros((n_zero, stage_ref.shape[2]), F32)

        def zero_copy(row, rows):
            return pltpu.make_async_copy(stage_ref.at[0, pl.ds(n_sorted, rows)],
                                         xs_ref.at[pl.ds(pl.multiple_of(row, SUBLANES), rows)], sem)

        pieces = [n_zero >> b for b in range((n_zero // SUBLANES).bit_length())]
        for action in ("start", "wait"):
            def empty_tile(k, carry, action=action):
                @pl.when(valid_ref[k] == 0)
                def _():
                    for part in range(tm // n_zero):
                        getattr(zero_copy(k * tm + part * n_zero, n_zero), action)()
                return carry

            lax.fori_loop(0, valid_ref.shape[0], empty_tile, 0)
            for e in range(N_EXPERTS):
                row = tail_ref[e]
                left = (-row) & (tm - 1)
                for rows in pieces:
                    @pl.when((left & rows) != 0)
                    def _(row=row, rows=rows):
                        getattr(zero_copy(row, rows), action)()
                    row = row + (left & rows)

    h = _rms(x_ref[...], g_ref[...]).astype(BF16)
    meta = meta_ref[...]
    e1, e2, r1, r2, g1, g2 = (meta[f:f + 1, :] for f in range(6))
    off1 = jnp.zeros_like(r1)
    off2 = jnp.zeros_like(r2)
    for e in range(N_EXPERTS):
        lo = local_ref[i * N_EXPERTS + e].astype(F32)
        off1 = jnp.where(e1 == e, lo, off1)
        off2 = jnp.where(e2 == e, lo, off2)
    slot = lax.broadcasted_iota(jnp.int32, (n_sorted, tt), 0).astype(F32)
    hit1 = slot == r1 + off1
    hit2 = slot == r2 + off2
    perm = jnp.where(hit1 | hit2, 1.0, 0.0).astype(BF16)
    half = i % 2
    stage_ref[half, 0:n_sorted, 0:d] = jnp.dot(perm, h, preferred_element_type=F32)
    gate = jnp.sum(jnp.where(hit1, g1, 0.0) + jnp.where(hit2, g2, 0.0), axis=-1, keepdims=True)
    stage_ref[half, 0:n_sorted, d:] = jnp.broadcast_to(gate, (n_sorted, LANE))

    def run_copies(step, action):
        for e in range(N_EXPERTS):
            k = step * N_EXPERTS + e
            for cond, rows in _run_windows(cnt_ref[k]):
                @pl.when(cond)
                def _(k=k, rows=rows):
                    copy = pltpu.make_async_copy(
                        stage_ref.at[step % 2, pl.ds(pl.multiple_of(local_ref[k], SUBLANES), rows)],
                        xs_ref.at[pl.ds(pl.multiple_of(base_ref[k], SUBLANES), rows)], sem)
                    getattr(copy, action)()

    @pl.when(i > 0)
    def _():
        run_copies(i - 1, "wait")

    run_copies(i, "start")

    @pl.when(i == pl.num_programs(0) - 1)
    def _():
        run_copies(i, "wait")


def dispatch(x2, g, meta_row, plan):
    t, d = x2.shape
    tt = MOE_TOKEN_TILE
    grid_spec = pltpu.PrefetchScalarGridSpec(
        num_scalar_prefetch=5,
        grid=(t // tt,),
        in_specs=[pl.BlockSpec((tt, d), lambda i, *_: (i, 0)),
                  pl.BlockSpec((1, d), lambda i, *_: (0, 0)),
                  pl.BlockSpec((None, META_FIELDS, tt), lambda i, *_: (i, 0, 0))],
        out_specs=pl.BlockSpec(memory_space=pl.ANY),
        scratch_shapes=[pltpu.VMEM((2, 2 * tt + SUBLANES * N_EXPERTS + MOE_TOKEN_TILE, d + LANE), F32),
                        pltpu.SemaphoreType.DMA(())],
    )
    return pl.pallas_call(
        _dispatch_kernel,
        out_shape=jax.ShapeDtypeStruct((plan["n_tiles"] * MOE_ROW_TILE, d + LANE), F32),
        grid_spec=grid_spec,
        compiler_params=_params("arbitrary"),
        name="moe_dispatch",
    )(plan["counts"], plan["local"], plan["base"], plan["tail"], plan["tile_valid"], x2, g, meta_row)


def _experts_kernel(te_ref, rows_ref, x_ref, wg_ref, wu_ref, wd_ref, o_ref):
    n = rows_ref[pl.program_id(0)]
    tm, d = o_ref.shape
    f = wg_ref.shape[-1]

    def ffn(r0, nrows):
        rows = slice(r0, r0 + nrows)
        xb = x_ref[rows, :d].astype(BF16)
        acc = None
        for c0 in range(0, f, MOE_FF_CHUNK):
            cols = slice(c0, min(c0 + MOE_FF_CHUNK, f))
            a = jnp.dot(xb, wg_ref[:, cols], preferred_element_type=F32)
            u = jnp.dot(xb, wu_ref[:, cols], preferred_element_type=F32)
            act = (a * jax.nn.sigmoid(a) * u).astype(BF16)
            part = jnp.dot(act, wd_ref[cols, :], preferred_element_type=F32)
            acc = part if acc is None else acc + part
        o_ref[rows, :] = acc * x_ref[rows, d:d + 1]

    last_piece = tm - MOE_SUB_ROWS

    @pl.when(n > last_piece)
    def _():
        ffn(0, tm)

    @pl.when(n <= last_piece)
    def _():
        for r0 in range(0, last_piece, MOE_SUB_ROWS):
            @pl.when(n > r0)
            def _(r0=r0):
                ffn(r0, MOE_SUB_ROWS)

            @pl.when(n <= r0)
            def _(r0=r0):
                o_ref[r0:r0 + MOE_SUB_ROWS, :] = jnp.zeros((MOE_SUB_ROWS, d), o_ref.dtype)
        o_ref[last_piece:, :] = jnp.zeros((MOE_SUB_ROWS, d), o_ref.dtype)


def experts(xs, plan, w_gate, w_up, w_down):
    n_rows, dw = xs.shape
    d = dw - LANE
    f = w_gate.shape[-1]
    tm = MOE_ROW_TILE
    weights = lambda shape: pl.BlockSpec((None,) + shape, lambda i, te, tv: (te[i], 0, 0),
                                         pipeline_mode=pl.Buffered(1))
    grid_spec = pltpu.PrefetchScalarGridSpec(
        num_scalar_prefetch=2,
        grid=(n_rows // tm,),
        in_specs=[pl.BlockSpec((tm, dw), lambda i, te, tv: (i, 0)),
                  weights((d, f)), weights((d, f)), weights((f, d))],
        out_specs=pl.BlockSpec((tm, d), lambda i, te, tv: (i, 0)),
    )
    return pl.pallas_call(
        _experts_kernel,
        out_shape=jax.ShapeDtypeStruct((n_rows, d), F32),
        grid_spec=grid_spec,
        compiler_params=_params("arbitrary"),
        name="moe_experts",
    )(plan["tile_expert"], plan["tile_rows"], xs, w_gate, w_up, w_down)


def _combine_kernel(cnt_ref, base_ref, x_ref, meta_ref, ys_ref, o_ref, small_ref, wide_ref, sems):
    i = pl.program_id(0)
    tt = x_ref.shape[0]
    meta = meta_ref[...]
    e1, e2, r1, r2 = (meta[:, f:f + 1] for f in range(4))
    widest = cnt_ref[i * N_EXPERTS]
    for e in range(1, N_EXPERTS):
        widest = jnp.maximum(widest, cnt_ref[i * N_EXPERTS + e])

    def small_copies(step):
        return [pltpu.make_async_copy(
            ys_ref.at[pl.ds(pl.multiple_of(base_ref[step * N_EXPERTS + e], SUBLANES), MOE_SMALL_WINDOW)],
            small_ref.at[step % 2, e], sems.at[step % 2]) for e in range(N_EXPERTS)]

    @pl.when(i == 0)
    def _():
        for c in small_copies(i):
            c.start()

    @pl.when(i + 1 < pl.num_programs(0))
    def _():
        for c in small_copies(i + 1):
            c.start()

    for c in small_copies(i):
        c.wait()

    def pick_rows(window_of):
        rows = window_of(0).shape[0]
        pos = lax.broadcasted_iota(jnp.int32, (tt, rows), 1).astype(F32)
        acc = x_ref[...]
        for e in range(N_EXPERTS):
            want1 = jnp.where(e1 == e, r1, -1.0)
            want2 = jnp.where(e2 == e, r2, -1.0)
            pick = jnp.where((pos == want1) | (pos == want2), 1.0, 0.0).astype(BF16)
            acc = acc + jnp.dot(pick, window_of(e).astype(BF16), preferred_element_type=F32)
        o_ref[...] = acc

    @pl.when(widest <= MOE_SMALL_WINDOW)
    def _():
        pick_rows(lambda e: small_ref[i % 2, e])

    @pl.when(widest > MOE_SMALL_WINDOW)
    def _():
        copies = [pltpu.make_async_copy(
            ys_ref.at[pl.ds(pl.multiple_of(base_ref[i * N_EXPERTS + e], SUBLANES), tt)],
            wide_ref.at[e], sems.at[2]) for e in range(N_EXPERTS)]
        for c in copies:
            c.start()
        for c in copies:
            c.wait()
        pick_rows(lambda e: wide_ref[e])


def combine(x2, meta_col, ys, plan):
    t, d = x2.shape
    tt = MOE_TOKEN_TILE
    grid_spec = pltpu.PrefetchScalarGridSpec(
        num_scalar_prefetch=2,
        grid=(t // tt,),
        in_specs=[pl.BlockSpec((tt, d), lambda i, *_: (i, 0)),
                  pl.BlockSpec((tt, META_FIELDS), lambda i, *_: (i, 0)),
                  pl.BlockSpec(memory_space=pl.ANY)],
        out_specs=pl.BlockSpec((tt, d), lambda i, *_: (i, 0)),
        scratch_shapes=[pltpu.VMEM((2, N_EXPERTS, MOE_SMALL_WINDOW, d), F32),
                        pltpu.VMEM((N_EXPERTS, tt, d), F32), pltpu.SemaphoreType.DMA((3,))],
    )
    return pl.pallas_call(
        _combine_kernel,
        out_shape=jax.ShapeDtypeStruct((t, d), F32),
        grid_spec=grid_spec,
        compiler_params=_params("arbitrary"),
        name="moe_combine",
    )(plan["counts"], plan["base"], x2, meta_col, ys)


def _pad_cols(w, mult):
    pad = (-w.shape[-1]) % mult
    return jnp.pad(w, ((0, 0),) * (w.ndim - 1) + ((0, pad),))


def kernel(x, rel_bias, conv_norm, conv_w_in, conv_b_in, conv_dw_w, conv_dw_b, conv_ln_g, conv_ln_b, conv_w_out, conv_b_out, kv_norm, w_kv, k_norm, attn_norm, w_q, q_norm, w_o, ffn_norm, ffn_w_gate, ffn_w_up, ffn_w_down, moe_router, moe_w_gate, moe_w_up, moe_w_down):
    b, s, d = x.shape
    t = b * s
    row = lambda v: v.reshape(1, -1)

    x2 = x.reshape(t, d)
    ff_pad = (-ffn_w_gate.shape[-1]) % LANE
    u, (w_conv_out, wg, wu, wq, wkv, wo) = conv_in(
        x2, row(conv_norm[0]), conv_w_in[0].astype(BF16), row(conv_b_in[0]),
        [conv_w_out[0], ffn_w_gate[0], ffn_w_up[0], w_q[0], w_kv, w_o[0]], [0, ff_pad, ff_pad, 0, 0, 0])
    n_e, _, f_e = moe_w_gate[0].shape
    x3, (we_gate, we_up, we_down) = conv_out(
        u.reshape(b, s, d), x, conv_dw_w[0], row(conv_dw_b[0]), row(conv_ln_g[0]), row(conv_ln_b[0]),
        w_conv_out, row(conv_b_out[0]),
        [moe_w_gate[0].reshape(n_e * d, f_e), moe_w_up[0].reshape(n_e * d, f_e),
         moe_w_down[0].reshape(n_e * f_e, d)])
    x2 = x3.reshape(t, d)
    wd = jnp.pad(ffn_w_down[0], ((0, ff_pad), (0, 0))).astype(BF16)
    x2 = ffn_dense(x2, row(ffn_norm[0]), wg, wu, wd)

    qn = row(jnp.tile(q_norm[0] * (HEAD_DIM ** -0.5), (1, N_HEADS)))
    kn = row(jnp.tile(k_norm, (1, N_HEADS)))
    x3 = x2.reshape(b, s, d)
    qkv = qkv_proj(x3, row(attn_norm[0]), row(kv_norm), wq, wkv, qn, kn)
    outs, lses = [], []
    for g, (_, dil) in enumerate(DILATED_GROUPS):
        qr, kr, vr = (a.reshape(b * dil, s // dil, a.shape[-1]) for a in qkv[3 * g:3 * g + 3])
        o_g, lse_g = dilated_attention_group(qr, kr, vr, band_bias(rel_bias, g))
        outs.append(o_g.reshape(b, dil, s // dil, o_g.shape[-1]))
        lses.append(lse_g.reshape(b, dil, s // dil, LANE))
    x2 = attn_out(x3, outs, lses, wo).reshape(t, d)

    g_moe = row(ffn_norm[1])
    meta_col, meta_row, counts = router(x2, g_moe, moe_router[0])
    plan = moe_plan(counts[:, 0, :N_EXPERTS], t)
    xs = dispatch(x2, g_moe, meta_row, plan)
    ys = experts(xs, plan, we_gate.reshape(n_e, d, f_e), we_up.reshape(n_e, d, f_e),
                 we_down.reshape(n_e, f_e, d))
    return combine(x2, meta_col, ys, plan).reshape(b, s, d)
```

```python
import functools

import numpy as np
import jax
import jax.numpy as jnp
from jax import lax
from jax.experimental import pallas as pl
from jax.experimental.pallas import tpu as pltpu

F32 = jnp.float32
BF16 = jnp.bfloat16

EPS = 1e-6
NEG = -1e30
CONV_WIDTH = 31
CONV_HALO = 32
N_HEADS = 16
HEAD_DIM = 64
DILATED_GROUPS = ((128, 1), (512, 4), (2048, 16))
BLK = 128
ATTN_BLOCKS_PER_STEP = 4
N_BUCKETS = 32
MAX_DISTANCE = 2048
N_EXPERTS = 8
LANE = 128
SUBLANES = 8
VMEM_LIMIT_BYTES = 56 * 1024 * 1024

ROW_TILE = 512
CONV_IN_TILE = 1024
CONV_CHUNK = 64
CONV_SPAN = CONV_CHUNK + SUBLANES * ((CONV_WIDTH - 1) // SUBLANES)
CONV_WINDOW = 128
MOE_ROW_TILE = 1024
MOE_FF_CHUNK = 1024
MOE_SUB_ROWS = 256
MOE_TOKEN_TILE = 512
MOE_DISPATCH_WINDOW = 160
MOE_SMALL_WINDOW = 256
META_FIELDS = 8


def _params(*sem):
    return pltpu.CompilerParams(dimension_semantics=sem, vmem_limit_bytes=VMEM_LIMIT_BYTES)


def _resident(shape):
    nd = len(shape)
    return pl.BlockSpec(shape, lambda *_: (0,) * nd, pipeline_mode=pl.Buffered(1))


def _rms(x, g):
    return x * lax.rsqrt(jnp.mean(x * x, axis=-1, keepdims=True) + EPS) * g


class _WeightCast:
    def __init__(self, weights, pads, steps, index):
        self.shapes = [w.shape for w in weights]
        self.pads = pads
        self.inputs, self.in_specs, self.out_shape, self.out_specs = [], [], [], []
        for w, pad in zip(weights, pads):
            rows, cols = w.shape
            assert rows % steps == 0, (rows, steps)
            block = lambda *grid: (index(*grid), 0, 0)
            self.inputs.append(w.reshape(steps, rows // steps, cols))
            self.in_specs.append(pl.BlockSpec((None, rows // steps, cols), block))
            self.out_shape.append(jax.ShapeDtypeStruct((steps, rows // steps, cols + pad), BF16))
            self.out_specs.append(pl.BlockSpec((None, rows // steps, cols + pad), block))

    @staticmethod
    def run(src_refs, dst_refs):
        for src, dst in zip(src_refs, dst_refs):
            cols = src.shape[-1]
            dst[:, :cols] = src[...].astype(dst.dtype)
            if dst.shape[-1] > cols:
                dst[:, cols:] = jnp.zeros((dst.shape[0], dst.shape[-1] - cols), dst.dtype)

    def results(self, outs):
        return [o.reshape(rows, cols + pad) for o, (rows, cols), pad in zip(outs, self.shapes, self.pads)]


def _conv_in_kernel(x_ref, g_ref, w_ref, b_ref, *refs):
    n_cast = (len(refs) - 1) // 2
    u_ref = refs[n_cast]
    h = _rms(x_ref[...], g_ref[...])
    y = jnp.dot(h.astype(BF16), w_ref[...], preferred_element_type=F32) + b_ref[...]
    d = u_ref.shape[-1]
    u_ref[...] = (y[:, :d] * jax.nn.sigmoid(y[:, d:])).astype(u_ref.dtype)
    _WeightCast.run(refs[:n_cast], refs[n_cast + 1:])


def conv_in(x2, g, w_in, b_in, later_weights, pads):
    t, d = x2.shape
    tm = CONV_IN_TILE
    steps = t // tm
    cast = _WeightCast(later_weights, pads, steps, lambda i: i)
    outs = pl.pallas_call(
        _conv_in_kernel,
        out_shape=(jax.ShapeDtypeStruct((t, d), BF16), *cast.out_shape),
        grid=(steps,),
        in_specs=[pl.BlockSpec((tm, d), lambda i: (i, 0)),
                  _resident((1, d)), _resident((d, 2 * d)), _resident((1, 2 * d)), *cast.in_specs],
        out_specs=(pl.BlockSpec((tm, d), lambda i: (i, 0)), *cast.out_specs),
        compiler_params=_params("parallel"),
        name="conv_in",
    )(x2, g, w_in, b_in, *cast.inputs)
    return outs[0], cast.results(outs[1:])


def _conv_out_kernel(u_ref, x_ref, dww_ref, dwb_ref, lng_ref, lnb_ref, w_ref, b_ref, shift_ref, *refs):
    n_cast = (len(refs) - 3) // 2
    o_ref = refs[n_cast]
    buf_ref, conv_ref = refs[-2:]
    _WeightCast.run(refs[:n_cast], refs[n_cast + 1:-2])
    ts, d = x_ref.shape
    j = pl.program_id(1)

    @pl.when(j == 0)
    def _():
        buf_ref[0:CONV_HALO, :] = jnp.zeros((CONV_HALO, d), buf_ref.dtype)
        buf_ref[CONV_HALO + ts:, :] = jnp.zeros((buf_ref.shape[0] - CONV_HALO - ts, d), buf_ref.dtype)

    @pl.when(j > 0)
    def _():
        buf_ref[0:CONV_HALO, :] = buf_ref[ts:ts + CONV_HALO, :]

    buf_ref[CONV_HALO:CONV_HALO + ts, :] = u_ref[...]

    def lane_chunk(c, carry):
        lanes = pl.ds(pl.multiple_of(c * LANE, LANE), LANE)
        taps = [dww_ref[k:k + 1, lanes] for k in range(CONV_WIDTH)]
        bias = dwb_ref[:, lanes]
        for r0 in range(0, ts, CONV_CHUNK):
            shifted = jnp.dot(shift_ref[...], buf_ref[r0:r0 + CONV_WINDOW, lanes],
                              preferred_element_type=F32)
            acc = jnp.broadcast_to(bias, (CONV_CHUNK, LANE))
            for phase in range(SUBLANES):
                for a, k in enumerate(range(phase, CONV_WIDTH, SUBLANES)):
                    lo = phase * CONV_SPAN + SUBLANES * a
                    acc = acc + taps[k] * shifted[lo:lo + CONV_CHUNK]
            conv_ref[r0:r0 + CONV_CHUNK, lanes] = acc
        return carry

    lax.fori_loop(0, d // LANE, lane_chunk, 0)
    acc = conv_ref[...]
    mu = jnp.mean(acc, axis=-1, keepdims=True)
    cen = acc - mu
    var = jnp.mean(cen * cen, axis=-1, keepdims=True)
    y = cen * lax.rsqrt(var + EPS) * lng_ref[...] + lnb_ref[...]
    act = (y * jax.nn.sigmoid(y)).astype(BF16)
    o_ref[...] = x_ref[...] + b_ref[...] + jnp.dot(act, w_ref[...], preferred_element_type=F32)


def conv_out(u3, x3, dw_w, dw_b, ln_g, ln_b, w_out, b_out, later_weights):
    b, s, d = x3.shape
    ts = ROW_TILE
    nj = s // ts
    cast = _WeightCast(later_weights, [0] * len(later_weights), b * nj, lambda bi, j: bi * nj + j)
    row = pl.BlockSpec((None, ts, d), lambda bi, j: (bi, j, 0))
    first_tap = CONV_HALO - (CONV_WIDTH - 1)
    shift = np.zeros((SUBLANES * CONV_SPAN, CONV_WINDOW), np.float32)
    for p in range(SUBLANES):
        i = np.arange(CONV_SPAN)
        shift[p * CONV_SPAN + i, first_tap + p + i] = 1.0
    outs = pl.pallas_call(
        _conv_out_kernel,
        out_shape=(jax.ShapeDtypeStruct((b, s, d), F32), *cast.out_shape),
        grid=(b, nj),
        in_specs=[row, row, _resident((CONV_WIDTH, d)), _resident((1, d)), _resident((1, d)),
                  _resident((1, d)), _resident((d, d)), _resident((1, d)), _resident(shift.shape),
                  *cast.in_specs],
        out_specs=(row, *cast.out_specs),
        scratch_shapes=[pltpu.VMEM((ts - CONV_CHUNK + CONV_WINDOW, d), BF16), pltpu.VMEM((ts, d), F32)],
        compiler_params=_params("parallel", "arbitrary"),
        name="conv_out",
    )(u3, x3, dw_w, dw_b, ln_g, ln_b, w_out, b_out, jnp.asarray(shift, BF16), *cast.inputs)
    return outs[0], cast.results(outs[1:])


def _ffn_kernel(x_ref, g_ref, wg_ref, wu_ref, wd_ref, o_ref):
    x = x_ref[...]
    h = _rms(x, g_ref[...]).astype(BF16)
    a = jnp.dot(h, wg_ref[...], preferred_element_type=F32)
    u = jnp.dot(h, wu_ref[...], preferred_element_type=F32)
    act = (a * jax.nn.sigmoid(a) * u).astype(BF16)
    o_ref[...] = x + jnp.dot(act, wd_ref[...], preferred_element_type=F32)


def ffn_dense(x2, g, w_gate, w_up, w_down):
    t, d = x2.shape
    f = w_gate.shape[1]
    row = pl.BlockSpec((ROW_TILE, d), lambda i: (i, 0))
    return pl.pallas_call(
        _ffn_kernel,
        out_shape=jax.ShapeDtypeStruct((t, d), F32),
        grid=(t // ROW_TILE,),
        in_specs=[row, _resident((1, d)), _resident((d, f)), _resident((d, f)), _resident((f, d))],
        out_specs=row,
        compiler_params=_params("parallel"),
        name="ffn_dense",
    )(x2, g, w_gate, w_up, w_down)


def _store_heads(y, out_ref, gain_ref=None, gain_off=0):
    dil, n, _ = out_ref.shape
    low = lax.broadcasted_iota(jnp.int32, (1, LANE), 1) < HEAD_DIM
    for c in range(y.shape[-1] // LANE):
        cols = slice(c * LANE, (c + 1) * LANE)
        blk = y[:, cols]
        if gain_ref is not None:
            sq = blk * blk
            ms_lo = jnp.sum(jnp.where(low, sq, 0.0), axis=-1, keepdims=True) * (1.0 / HEAD_DIM)
            ms_hi = jnp.sum(jnp.where(low, 0.0, sq), axis=-1, keepdims=True) * (1.0 / HEAD_DIM)
            inv = jnp.where(low, lax.rsqrt(ms_lo + EPS), lax.rsqrt(ms_hi + EPS))
            blk = blk * inv * gain_ref[:, gain_off + c * LANE:gain_off + (c + 1) * LANE]
        blk = blk.astype(out_ref.dtype)
        for r in range(dil):
            out_ref[r, :, cols] = blk[r * n:(r + 1) * n]


def _qkv_kernel(x_ref, gq_ref, gkv_ref, wq_ref, wkv_ref, qn_ref, kn_ref, *refs):
    out_refs, xn_ref = refs[:-1], refs[-1]
    x = x_ref[...]
    xn = x * lax.rsqrt(jnp.mean(x * x, axis=-1, keepdims=True) + EPS)
    n_cols = xn_ref.shape[0]
    for c in range(n_cols):
        xn_ref[c] = xn[:, c * LANE:(c + 1) * LANE]
    tm = x.shape[0]
    hw = N_HEADS * HEAD_DIM
    qw = len(DILATED_GROUPS) * hw
    for g, (_, dil) in enumerate(DILATED_GROUPS):
        n = tm // dil
        xg = xn if dil == 1 else jnp.concatenate(
            [jnp.concatenate([xn_ref[c, pl.ds(r, n, stride=dil), :] for r in range(dil)], axis=0)
             for c in range(n_cols)], axis=1)
        hq = (xg * gq_ref[...]).astype(BF16)
        hkv = (xg * gkv_ref[...]).astype(BF16)
        q_ref, k_ref, v_ref = out_refs[3 * g:3 * g + 3]
        cols = slice(g * hw, (g + 1) * hw)
        _store_heads(jnp.dot(hq, wq_ref[:, cols], preferred_element_type=F32), q_ref, qn_ref, g * hw)
        _store_heads(jnp.dot(hkv, wkv_ref[:, cols], preferred_element_type=F32), k_ref, kn_ref, g * hw)
        _store_heads(jnp.dot(hkv, wkv_ref[:, qw + g * hw:qw + (g + 1) * hw],
                             preferred_element_type=F32), v_ref)


def qkv_proj(x3, gq, gkv, w_q, w_kv, qn, kn):
    b, s, d = x3.shape
    qw = w_q.shape[1]
    hw = N_HEADS * HEAD_DIM
    tm = ROW_TILE
    out_shape, out_specs = [], []
    for _, dil in DILATED_GROUPS:
        for _ in range(3):
            out_shape.append(jax.ShapeDtypeStruct((b, dil, s // dil, hw), BF16))
            out_specs.append(pl.BlockSpec((None, dil, tm // dil, hw), lambda bi, j: (bi, 0, j, 0)))
    return pl.pallas_call(
        _qkv_kernel,
        out_shape=tuple(out_shape),
        grid=(b, s // tm),
        in_specs=[pl.BlockSpec((None, tm, d), lambda bi, j: (bi, j, 0)), _resident((1, d)),
                  _resident((1, d)), _resident((d, qw)), _resident((d, 2 * qw)), _resident((1, qw)),
                  _resident((1, qw))],
        out_specs=tuple(out_specs),
        scratch_shapes=[pltpu.VMEM((d // LANE, tm, LANE), F32)],
        compiler_params=_params("parallel", "parallel"),
        name="qkv_proj",
    )(x3, gq, gkv, w_q, w_kv, qn, kn)


def _t5_bucket(dist):
    n = np.asarray(dist)
    max_exact = N_BUCKETS // 2
    large = max_exact + (np.log(np.maximum(n, 1) / max_exact)
                         / np.log(MAX_DISTANCE / max_exact)
                         * (N_BUCKETS - max_exact)).astype(np.int32)
    large = np.minimum(large, N_BUCKETS - 1)
    return np.where(n < max_exact, n, large).astype(np.int32)


def band_bias(rel_bias, g):
    window, d = DILATED_GROUPS[g]
    span = window // d
    period = 3 * BLK
    diff = np.arange(period)
    diff = np.where(diff >= 2 * BLK, diff - period, diff)
    sub = BLK - diff
    in_band = (sub >= 0) & (sub <= span)
    bucket = _t5_bucket(np.clip(sub, 0, span) * d)
    table = rel_bias[:, g * N_HEADS:(g + 1) * N_HEADS].astype(F32)
    onehot = jnp.asarray(bucket[:, None] == np.arange(N_BUCKETS)[None, :], F32)
    line = jnp.dot(onehot, table, precision=lax.Precision.HIGHEST).T
    line = jnp.where(jnp.asarray(in_band)[None, :], line, NEG)
    flat = jnp.tile(line, (1, BLK))[:, :BLK * (period - 1)]
    general = flat.reshape(N_HEADS, BLK, period - 1)[:, :, :2 * BLK]
    has_prev = jnp.asarray(np.arange(2 * BLK) >= BLK)[None, None, :]
    first = jnp.where(has_prev, general, NEG)
    return jnp.stack([first, general])


def _attn_kernel(q_ref, k_ref, v_ref, bias_ref, hm_ref, o_ref, lse_ref,
                 kprev_ref, vprev_ref, s_ref, p_ref, m_ref, *, seq_blocks):
    i = pl.program_id(1)

    @pl.when(i == 0)
    def _():
        kprev_ref[...] = jnp.zeros(kprev_ref.shape, kprev_ref.dtype)
        vprev_ref[...] = jnp.zeros(vprev_ref.shape, vprev_ref.dtype)

    n_cols = q_ref.shape[-1] // LANE
    lane = lax.broadcasted_iota(jnp.int32, (BLK, LANE), 1)
    low = lane < HEAD_DIM
    ones = jnp.ones((2 * BLK, LANE), BF16)

    for b in range(q_ref.shape[0] // BLK):
        rows = slice(b * BLK, (b + 1) * BLK)
        before = slice((b - 1) * BLK, b * BLK)
        variant = jnp.minimum(i, 1) if b == 0 else (0 if b % seq_blocks == 0 else 1)

        for c in range(n_cols):
            cols = slice(c * LANE, (c + 1) * LANE)
            q2 = q_ref[rows, cols]
            kcat = jnp.concatenate([kprev_ref[:, cols] if b == 0 else k_ref[before, cols],
                                    k_ref[rows, cols]], axis=0)
            for half in range(2):
                h = 2 * c + half
                s = lax.dot_general(q2 * hm_ref[half:half + 1, :], kcat, (((1,), (1,)), ((), ())),
                                    preferred_element_type=F32) + bias_ref[variant, h]
                s_ref[b, h] = s
                m_ref[b, h] = jnp.broadcast_to(jnp.max(s, axis=-1, keepdims=True), (BLK, LANE))

        for h in range(N_HEADS):
            m = m_ref[b, h]
            p_ref[b, h] = jnp.exp(s_ref[b, h] - jnp.concatenate([m, m], axis=-1)).astype(p_ref.dtype)

        m_tile = jnp.zeros((BLK, LANE), F32)
        l_tile = jnp.ones((BLK, LANE), F32)
        for c in range(n_cols):
            cols = slice(c * LANE, (c + 1) * LANE)
            vcat = jnp.concatenate([vprev_ref[:, cols] if b == 0 else v_ref[before, cols],
                                    v_ref[rows, cols]], axis=0)
            acc = []
            for half in range(2):
                rhs = jnp.concatenate([vcat * hm_ref[half:half + 1, :], ones], axis=-1)
                acc.append(jnp.dot(p_ref[b, 2 * c + half], rhs, preferred_element_type=F32))
            den = jnp.where(low, acc[0][:, LANE:], acc[1][:, LANE:])
            o_ref[rows, cols] = ((acc[0][:, :LANE] + acc[1][:, :LANE]) / den).astype(o_ref.dtype)
            for half in range(2):
                h = 2 * c + half
                m_tile = jnp.where(lane == h, m_ref[b, h], m_tile)
                l_tile = jnp.where(lane == h, acc[half][:, LANE:], l_tile)
        lse_ref[rows, :] = m_tile + jnp.log(l_tile)

    last = slice(q_ref.shape[0] - BLK, q_ref.shape[0])
    kprev_ref[...] = k_ref[last, :]
    vprev_ref[...] = v_ref[last, :]


def dilated_attention_group(q, k, v, bias):
    n_seq, seq_len, w = q.shape
    seq_blocks = seq_len // BLK
    merge = max(1, ATTN_BLOCKS_PER_STEP // seq_blocks)
    assert n_seq % merge == 0, (n_seq, merge)
    r, l = n_seq // merge, seq_len * merge
    q, k, v = (a.reshape(r, l, w) for a in (q, k, v))
    nb = min(ATTN_BLOCKS_PER_STEP, l // BLK)
    rows = nb * BLK
    assert l % rows == 0, (l, rows)
    blk = pl.BlockSpec((None, rows, w), lambda a, i: (a, i, 0))
    lane = np.arange(LANE)[None, :] < HEAD_DIM
    head_mask = jnp.asarray(np.concatenate([lane, ~lane]).astype(np.float32), BF16)
    o, lse = pl.pallas_call(
        functools.partial(_attn_kernel, seq_blocks=seq_blocks),
        out_shape=(jax.ShapeDtypeStruct((r, l, w), BF16),
                   jax.ShapeDtypeStruct((r, l, LANE), F32)),
        grid=(r, l // rows),
        in_specs=[blk, blk, blk, _resident(bias.shape), _resident((2, LANE))],
        out_specs=(blk, pl.BlockSpec((None, rows, LANE), lambda a, i: (a, i, 0))),
        scratch_shapes=[pltpu.VMEM((BLK, w), BF16), pltpu.VMEM((BLK, w), BF16),
                        pltpu.VMEM((nb, N_HEADS, BLK, 2 * BLK), F32),
                        pltpu.VMEM((nb, N_HEADS, BLK, 2 * BLK), BF16),
                        pltpu.VMEM((nb, N_HEADS, BLK, LANE), F32)],
        compiler_params=_params("parallel", "arbitrary"),
        name="dilated_attn",
    )(q, k, v, bias, head_mask)
    return o.reshape(n_seq, seq_len, w), lse.reshape(n_seq, seq_len, LANE)


def _attn_out_kernel(x_ref, o0_ref, o1_ref, o2_ref, l0_ref, l1_ref, l2_ref, e_ref, w_ref, o_ref,
                     on_ref, ln_ref):
    n_cols = on_ref.shape[1]
    for gi, (og_ref, lg_ref) in enumerate(((o1_ref, l1_ref), (o2_ref, l2_ref))):
        dil, n, _ = og_ref.shape
        for r in range(dil):
            ln_ref[gi, pl.ds(r, n, stride=dil), :] = lg_ref[r]
            for c in range(n_cols):
                on_ref[gi, c, pl.ds(r, n, stride=dil), :] = (
                    og_ref[r, :, c * LANE:(c + 1) * LANE].astype(F32))
    l0, l1, l2 = l0_ref[0], ln_ref[0], ln_ref[1]
    m = jnp.maximum(jnp.maximum(l0, l1), l2)
    e0, e1, e2 = jnp.exp(l0 - m), jnp.exp(l1 - m), jnp.exp(l2 - m)
    den = e0 + e1 + e2
    acc = None
    token_order = [jnp.concatenate([on_ref[gi, c] for c in range(n_cols)], axis=1) for gi in range(2)]
    for e, o in ((e0, o0_ref[0].astype(F32)), (e1, token_order[0]), (e2, token_order[1])):
        wgt = jnp.dot((e / den).astype(BF16), e_ref[...], preferred_element_type=F32)
        term = wgt * o
        acc = term if acc is None else acc + term
    o_ref[...] = x_ref[...] + jnp.dot(acc.astype(BF16), w_ref[...], preferred_element_type=F32)


def attn_out(x3, outs, lses, w_o):
    b, s, d = x3.shape
    tm = ROW_TILE
    expand = np.zeros((LANE, d), np.float32)
    expand[:N_HEADS] = np.repeat(np.eye(N_HEADS, dtype=np.float32), HEAD_DIM, axis=1)
    expand = jnp.asarray(expand, BF16)
    row = pl.BlockSpec((None, tm, d), lambda bi, j: (bi, j, 0))
    o_specs = [pl.BlockSpec((None, dil, tm // dil, d), lambda bi, j: (bi, 0, j, 0))
               for _, dil in DILATED_GROUPS]
    l_specs = [pl.BlockSpec((None, dil, tm // dil, LANE), lambda bi, j: (bi, 0, j, 0))
               for _, dil in DILATED_GROUPS]
    return pl.pallas_call(
        _attn_out_kernel,
        out_shape=jax.ShapeDtypeStruct((b, s, d), F32),
        grid=(b, s // tm),
        in_specs=[row, *o_specs, *l_specs, _resident((LANE, d)), _resident((d, d))],
        out_specs=row,
        scratch_shapes=[pltpu.VMEM((2, d // LANE, tm, LANE), F32), pltpu.VMEM((2, tm, LANE), F32)],
        compiler_params=_params("parallel", "parallel"),
        name="attn_out",
    )(x3, *outs, *lses, expand, w_o)


def _router_kernel(x_ref, g_ref, w_ref, tri_ref, col_ref, row_ref, cnt_ref):
    h = _rms(x_ref[...], g_ref[...])
    w = w_ref[...]
    h_hi = h.astype(BF16)
    h_lo = (h - h_hi.astype(F32)).astype(BF16)
    w_hi = w.astype(BF16)
    w_lo = (w - w_hi.astype(F32)).astype(BF16)
    logits = (jnp.dot(h_hi, w_hi, preferred_element_type=F32)
              + jnp.dot(h_lo, w_hi, preferred_element_type=F32)
              + jnp.dot(h_hi, w_lo, preferred_element_type=F32))
    ne = N_EXPERTS
    lane = lax.broadcasted_iota(jnp.int32, logits.shape, 1)
    logits = jnp.where(lane < ne, logits, -jnp.inf)
    m1 = jnp.max(logits, axis=-1, keepdims=True)
    i1 = jnp.min(jnp.where(logits == m1, lane, ne), axis=-1, keepdims=True)
    rest = jnp.where(lane == i1, -jnp.inf, logits)
    m2 = jnp.max(rest, axis=-1, keepdims=True)
    i2 = jnp.min(jnp.where(rest == m2, lane, ne), axis=-1, keepdims=True)
    e2 = jnp.exp(m2 - m1)
    g1 = 1.0 / (1.0 + e2)
    g2 = e2 / (1.0 + e2)
    both = (lane == i1) | (lane == i2)
    cnt = jnp.dot(tri_ref[...], jnp.where(both, 1.0, 0.0).astype(BF16), preferred_element_type=F32)
    r1 = jnp.sum(jnp.where(lane == i1, cnt, 0.0), axis=-1, keepdims=True)
    r2 = jnp.sum(jnp.where(lane == i2, cnt, 0.0), axis=-1, keepdims=True)
    cnt_ref[...] = jnp.sum(jnp.where(both, 1, 0), axis=0, keepdims=True)
    fields = (i1.astype(F32), i2.astype(F32), r1, r2, g1, g2)
    meta = jnp.zeros(logits.shape, F32)
    for f, val in enumerate(fields):
        meta = jnp.where(lane == f, val, meta)
    col_ref[...] = meta[:, :META_FIELDS]
    row_ref[...] = meta.T[:META_FIELDS, :]


def router(x2, g, w_router):
    t, d = x2.shape
    tt = MOE_TOKEN_TILE
    nt = t // tt
    tri = jnp.asarray(np.tril(np.ones((tt, tt), np.float32), -1), BF16)
    return pl.pallas_call(
        _router_kernel,
        out_shape=(jax.ShapeDtypeStruct((t, META_FIELDS), F32),
                   jax.ShapeDtypeStruct((nt, META_FIELDS, tt), F32),
                   jax.ShapeDtypeStruct((nt, 1, LANE), jnp.int32)),
        grid=(nt,),
        in_specs=[pl.BlockSpec((tt, d), lambda i: (i, 0)), _resident((1, d)),
                  _resident((d, LANE)), _resident((tt, tt))],
        out_specs=(pl.BlockSpec((tt, META_FIELDS), lambda i: (i, 0)),
                   pl.BlockSpec((None, META_FIELDS, tt), lambda i: (i, 0, 0)),
                   pl.BlockSpec((None, 1, LANE), lambda i: (i, 0, 0))),
        compiler_params=_params("parallel"),
        name="moe_router",
    )(x2, g, _pad_cols(w_router, LANE), tri)


def moe_plan(counts, n_tokens):
    tm = MOE_ROW_TILE
    counts = (counts + SUBLANES - 1) // SUBLANES * SUBLANES
    experts_ = jnp.arange(N_EXPERTS, dtype=jnp.int32)
    total = jnp.sum(counts, axis=0)
    owned = (total + MOE_TOKEN_TILE + tm - 1) // tm
    owned_end = jnp.cumsum(owned)
    region = (owned_end - owned) * tm
    base = region[None, :] + jnp.cumsum(counts, axis=0) - counts
    local = jnp.cumsum(counts, axis=1) - counts
    max_rows = 2 * n_tokens + (SUBLANES - 1) * N_EXPERTS * counts.shape[0] + N_EXPERTS * MOE_TOKEN_TILE
    n_tiles = -(-max_rows // tm) + N_EXPERTS + 1
    tile_id = jnp.arange(n_tiles, dtype=jnp.int32)
    tile_expert = jnp.minimum(jnp.sum(tile_id[:, None] >= owned_end[None, :], axis=1), N_EXPERTS - 1)
    pick = tile_expert[:, None] == experts_[None, :]
    rows_before = (tile_id - jnp.sum(jnp.where(pick, (owned_end - owned)[None, :], 0), axis=1)) * tm
    rows_left = jnp.sum(jnp.where(pick, total[None, :], 0), axis=1) - rows_before
    tile_rows = jnp.where(tile_id < owned_end[-1], jnp.clip(rows_left, 0, tm), 0).astype(jnp.int32)
    return dict(n_tiles=n_tiles, counts=counts.reshape(-1), base=base.reshape(-1).astype(jnp.int32),
                local=local.reshape(-1).astype(jnp.int32), tail=(region + total).astype(jnp.int32),
                tile_expert=tile_expert.astype(jnp.int32), tile_rows=tile_rows,
                tile_valid=(tile_rows > 0).astype(jnp.int32))


def _run_windows(n):
    return (((n > 0) & (n <= MOE_DISPATCH_WINDOW), MOE_DISPATCH_WINDOW),
            (n > MOE_DISPATCH_WINDOW, MOE_TOKEN_TILE))


def _dispatch_kernel(cnt_ref, local_ref, base_ref, tail_ref, valid_ref, x_ref, g_ref, meta_ref, xs_ref,
                     stage_ref, sem):
    i = pl.program_id(0)
    tt, d = x_ref.shape
    n_sorted = 2 * tt + SUBLANES * N_EXPERTS
    n_zero = stage_ref.shape[1] - n_sorted
    tm = MOE_ROW_TILE

    @pl.when(i == 0)
    def _():
        for side in range(2):
            stage_ref[side, n_sorted:, :] = jnp.zeros((n_zero, stage_ref.shape[2]), F32)

        def zero_copy(row, rows):
            return pltpu.make_async_copy(stage_ref.at[0, pl.ds(n_sorted, rows)],
                                         xs_ref.at[pl.ds(pl.multiple_of(row, SUBLANES), rows)], sem)

        pieces = [n_zero >> b for b in range((n_zero // SUBLANES).bit_length())]
        for action in ("start", "wait"):
            def empty_tile(k, carry, action=action):
                @pl.when(valid_ref[k] == 0)
                def _():
                    for part in range(tm // n_zero):
                        getattr(zero_copy(k * tm + part * n_zero, n_zero), action)()
                return carry

            lax.fori_loop(0, valid_ref.shape[0], empty_tile, 0)
            for e in range(N_EXPERTS):
                row = tail_ref[e]
                left = (-row) & (tm - 1)
                for rows in pieces:
                    @pl.when((left & rows) != 0)
                    def _(row=row, rows=rows):
                        getattr(zero_copy(row, rows), action)()
                    row = row + (left & rows)

    h = _rms(x_ref[...], g_ref[...]).astype(BF16)
    meta = meta_ref[...]
    e1, e2, r1, r2, g1, g2 = (meta[f:f + 1, :] for f in range(6))
    off1 = jnp.zeros_like(r1)
    off2 = jnp.zeros_like(r2)
    for e in range(N_EXPERTS):
        lo = local_ref[i * N_EXPERTS + e].astype(F32)
        off1 = jnp.where(e1 == e, lo, off1)
        off2 = jnp.where(e2 == e, lo, off2)
    slot = lax.broadcasted_iota(jnp.int32, (n_sorted, tt), 0).astype(F32)
    hit1 = slot == r1 + off1
    hit2 = slot == r2 + off2
    perm = jnp.where(hit1 | hit2, 1.0, 0.0).astype(BF16)
    half = i % 2
    stage_ref[half, 0:n_sorted, 0:d] = jnp.dot(perm, h, preferred_element_type=F32)
    gate = jnp.sum(jnp.where(hit1, g1, 0.0) + jnp.where(hit2, g2, 0.0), axis=-1, keepdims=True)
    stage_ref[half, 0:n_sorted, d:] = jnp.broadcast_to(gate, (n_sorted, LANE))

    def run_copies(step, action):
        for e in range(N_EXPERTS):
            k = step * N_EXPERTS + e
            for cond, rows in _run_windows(cnt_ref[k]):
                @pl.when(cond)
                def _(k=k, rows=rows):
                    copy = pltpu.make_async_copy(
                        stage_ref.at[step % 2, pl.ds(pl.multiple_of(local_ref[k], SUBLANES), rows)],
                        xs_ref.at[pl.ds(pl.multiple_of(base_ref[k], SUBLANES), rows)], sem)
                    getattr(copy, action)()

    @pl.when(i > 0)
    def _():
        run_copies(i - 1, "wait")

    run_copies(i, "start")

    @pl.when(i == pl.num_programs(0) - 1)
    def _():
        run_copies(i, "wait")


def dispatch(x2, g, meta_row, plan):
    t, d = x2.shape
    tt = MOE_TOKEN_TILE
    grid_spec = pltpu.PrefetchScalarGridSpec(
        num_scalar_prefetch=5,
        grid=(t // tt,),
        in_specs=[pl.BlockSpec((tt, d), lambda i, *_: (i, 0)),
                  pl.BlockSpec((1, d), lambda i, *_: (0, 0)),
                  pl.BlockSpec((None, META_FIELDS, tt), lambda i, *_: (i, 0, 0))],
        out_specs=pl.BlockSpec(memory_space=pl.ANY),
        scratch_shapes=[pltpu.VMEM((2, 2 * tt + SUBLANES * N_EXPERTS + MOE_TOKEN_TILE, d + LANE), F32),
                        pltpu.SemaphoreType.DMA(())],
    )
    return pl.pallas_call(
        _dispatch_kernel,
        out_shape=jax.ShapeDtypeStruct((plan["n_tiles"] * MOE_ROW_TILE, d + LANE), F32),
        grid_spec=grid_spec,
        compiler_params=_params("arbitrary"),
        name="moe_dispatch",
    )(plan["counts"], plan["local"], plan["base"], plan["tail"], plan["tile_valid"], x2, g, meta_row)


def _experts_kernel(te_ref, rows_ref, x_ref, wg_ref, wu_ref, wd_ref, o_ref):
    n = rows_ref[pl.program_id(0)]
    tm, d = o_ref.shape
    f = wg_ref.shape[-1]

    def ffn(r0, nrows):
        rows = slice(r0, r0 + nrows)
        xb = x_ref[rows, :d].astype(BF16)
        acc = None
        for c0 in range(0, f, MOE_FF_CHUNK):
            cols = slice(c0, min(c0 + MOE_FF_CHUNK, f))
            a = jnp.dot(xb, wg_ref[:, cols], preferred_element_type=F32)
            u = jnp.dot(xb, wu_ref[:, cols], preferred_element_type=F32)
            act = (a * jax.nn.sigmoid(a) * u).astype(BF16)
            part = jnp.dot(act, wd_ref[cols, :], preferred_element_type=F32)
            acc = part if acc is None else acc + part
        o_ref[rows, :] = acc * x_ref[rows, d:d + 1]

    last_piece = tm - MOE_SUB_ROWS

    @pl.when(n > last_piece)
    def _():
        ffn(0, tm)

    @pl.when(n <= last_piece)
    def _():
        for r0 in range(0, last_piece, MOE_SUB_ROWS):
            @pl.when(n > r0)
            def _(r0=r0):
                ffn(r0, MOE_SUB_ROWS)

            @pl.when(n <= r0)
            def _(r0=r0):
                o_ref[r0:r0 + MOE_SUB_ROWS, :] = jnp.zeros((MOE_SUB_ROWS, d), o_ref.dtype)
        o_ref[last_piece:, :] = jnp.zeros((MOE_SUB_ROWS, d), o_ref.dtype)


def experts(xs, plan, w_gate, w_up, w_down):
    n_rows, dw = xs.shape
    d = dw - LANE
    f = w_gate.shape[-1]
    tm = MOE_ROW_TILE
    weights = lambda shape: pl.BlockSpec((None,) + shape, lambda i, te, tv: (te[i], 0, 0),
                                         pipeline_mode=pl.Buffered(1))
    grid_spec = pltpu.PrefetchScalarGridSpec(
        num_scalar_prefetch=2,
        grid=(n_rows // tm,),
        in_specs=[pl.BlockSpec((tm, dw), lambda i, te, tv: (i, 0)),
                  weights((d, f)), weights((d, f)), weights((f, d))],
        out_specs=pl.BlockSpec((tm, d), lambda i, te, tv: (i, 0)),
    )
    return pl.pallas_call(
        _experts_kernel,
        out_shape=jax.ShapeDtypeStruct((n_rows, d), F32),
        grid_spec=grid_spec,
        compiler_params=_params("arbitrary"),
        name="moe_experts",
    )(plan["tile_expert"], plan["tile_rows"], xs, w_gate, w_up, w_down)


def _combine_kernel(cnt_ref, base_ref, x_ref, meta_ref, ys_ref, o_ref, small_ref, wide_ref, sems):
    i = pl.program_id(0)
    tt = x_ref.shape[0]
    meta = meta_ref[...]
    e1, e2, r1, r2 = (meta[:, f:f + 1] for f in range(4))
    widest = cnt_ref[i * N_EXPERTS]
    for e in range(1, N_EXPERTS):
        widest = jnp.maximum(widest, cnt_ref[i * N_EXPERTS + e])

    def small_copies(step):
        return [pltpu.make_async_copy(
            ys_ref.at[pl.ds(pl.multiple_of(base_ref[step * N_EXPERTS + e], SUBLANES), MOE_SMALL_WINDOW)],
            small_ref.at[step % 2, e], sems.at[step % 2]) for e in range(N_EXPERTS)]

    @pl.when(i == 0)
    def _():
        for c in small_copies(i):
            c.start()

    @pl.when(i + 1 < pl.num_programs(0))
    def _():
        for c in small_copies(i + 1):
            c.start()

    for c in small_copies(i):
        c.wait()

    def pick_rows(window_of):
        rows = window_of(0).shape[0]
        pos = lax.broadcasted_iota(jnp.int32, (tt, rows), 1).astype(F32)
        acc = x_ref[...]
        for e in range(N_EXPERTS):
            want1 = jnp.where(e1 == e, r1, -1.0)
            want2 = jnp.where(e2 == e, r2, -1.0)
            pick = jnp.where((pos == want1) | (pos == want2), 1.0, 0.0).astype(BF16)
            acc = acc + jnp.dot(pick, window_of(e).astype(BF16), preferred_element_type=F32)
        o_ref[...] = acc

    @pl.when(widest <= MOE_SMALL_WINDOW)
    def _():
        pick_rows(lambda e: small_ref[i % 2, e])

    @pl.when(widest > MOE_SMALL_WINDOW)
    def _():
        copies = [pltpu.make_async_copy(
            ys_ref.at[pl.ds(pl.multiple_of(base_ref[i * N_EXPERTS + e], SUBLANES), tt)],
            wide_ref.at[e], sems.at[2]) for e in range(N_EXPERTS)]
        for c in copies:
            c.start()
        for c in copies:
            c.wait()
        pick_rows(lambda e: wide_ref[e])


def combine(x2, meta_col, ys, plan):
    t, d = x2.shape
    tt = MOE_TOKEN_TILE
    grid_spec = pltpu.PrefetchScalarGridSpec(
        num_scalar_prefetch=2,
        grid=(t // tt,),
        in_specs=[pl.BlockSpec((tt, d), lambda i, *_: (i, 0)),
                  pl.BlockSpec((tt, META_FIELDS), lambda i, *_: (i, 0)),
                  pl.BlockSpec(memory_space=pl.ANY)],
        out_specs=pl.BlockSpec((tt, d), lambda i, *_: (i, 0)),
        scratch_shapes=[pltpu.VMEM((2, N_EXPERTS, MOE_SMALL_WINDOW, d), F32),
                        pltpu.VMEM((N_EXPERTS, tt, d), F32), pltpu.SemaphoreType.DMA((3,))],
    )
    return pl.pallas_call(
        _combine_kernel,
        out_shape=jax.ShapeDtypeStruct((t, d), F32),
        grid_spec=grid_spec,
        compiler_params=_params("arbitrary"),
        name="moe_combine",
    )(plan["counts"], plan["base"], x2, meta_col, ys)


def _pad_cols(w, mult):
    pad = (-w.shape[-1]) % mult
    return jnp.pad(w, ((0, 0),) * (w.ndim - 1) + ((0, pad),))


def kernel(x, rel_bias, conv_norm, conv_w_in, conv_b_in, conv_dw_w, conv_dw_b, conv_ln_g, conv_ln_b, conv_w_out, conv_b_out, kv_norm, w_kv, k_norm, attn_norm, w_q, q_norm, w_o, ffn_norm, ffn_w_gate, ffn_w_up, ffn_w_down, moe_router, moe_w_gate, moe_w_up, moe_w_down):
    b, s, d = x.shape
    t = b * s
    row = lambda v: v.reshape(1, -1)

    x2 = x.reshape(t, d)
    ff_pad = (-ffn_w_gate.shape[-1]) % LANE
    u, (w_conv_out, wg, wu, wq, wkv, wo) = conv_in(
        x2, row(conv_norm[0]), conv_w_in[0].astype(BF16), row(conv_b_in[0]),
        [conv_w_out[0], ffn_w_gate[0], ffn_w_up[0], w_q[0], w_kv, w_o[0]], [0, ff_pad, ff_pad, 0, 0, 0])
    n_e, _, f_e = moe_w_gate[0].shape
    x3, (we_gate, we_up, we_down) = conv_out(
        u.reshape(b, s, d), x, conv_dw_w[0], row(conv_dw_b[0]), row(conv_ln_g[0]), row(conv_ln_b[0]),
        w_conv_out, row(conv_b_out[0]),
        [moe_w_gate[0].reshape(n_e * d, f_e), moe_w_up[0].reshape(n_e * d, f_e),
         moe_w_down[0].reshape(n_e * f_e, d)])
    x2 = x3.reshape(t, d)
    wd = jnp.pad(ffn_w_down[0], ((0, ff_pad), (0, 0))).astype(BF16)
    x2 = ffn_dense(x2, row(ffn_norm[0]), wg, wu, wd)

    qn = row(jnp.tile(q_norm[0] * (HEAD_DIM ** -0.5), (1, N_HEADS)))
    kn = row(jnp.tile(k_norm, (1, N_HEADS)))
    x3 = x2.reshape(b, s, d)
    qkv = qkv_proj(x3, row(attn_norm[0]), row(kv_norm), wq, wkv, qn, kn)
    outs, lses = [], []
    for g, (_, dil) in enumerate(DILATED_GROUPS):
        qr, kr, vr = (a.reshape(b * dil, s // dil, a.shape[-1]) for a in qkv[3 * g:3 * g + 3])
        o_g, lse_g = dilated_attention_group(qr, kr, vr, band_bias(rel_bias, g))
        outs.append(o_g.reshape(b, dil, s // dil, o_g.shape[-1]))
        lses.append(lse_g.reshape(b, dil, s // dil, LANE))
    x2 = attn_out(x3, outs, lses, wo).reshape(t, d)

    g_moe = row(ffn_norm[1])
    meta_col, meta_row, counts = router(x2, g_moe, moe_router[0])
    plan = moe_plan(counts[:, 0, :N_EXPERTS], t)
    xs = dispatch(x2, g_moe, meta_row, plan)
    ys = experts(xs, plan, we_gate.reshape(n_e, d, f_e), we_up.reshape(n_e, d, f_e),
                 we_down.reshape(n_e, f_e, d))
    return combine(x2, meta_col, ys, plan).reshape(b, s, d)
```
